```python
import math
import jax, jax.numpy as jnp
from jax import lax
import numpy as np

D_MODEL = 2048
BATCH = 8
SEQ = 2048
DEPTH = 2
DEC_BATCH = 32
DEC_SEQ = 1
PAST_LEN = 8192
PAGE_SIZE = 128

HEAD_DIM = 128
N_BRANCH = 4
BR_WIDTH = D_MODEL // N_BRANCH
ROPE_THETA = 10000.0
NORM_EPS = 1e-6
A_HEADS = BR_WIDTH // HEAD_DIM
A_IDX_HEADS = 8
A_IDX_DIM = 64
A_TOPK = 256
A_QBLOCK = 128
B_WIDTH = BR_WIDTH
B_GROUP = 16
B_GROUPS = B_WIDTH // B_GROUP
B_STATE = 64
C_HEADS = BR_WIDTH // HEAD_DIM
C_CMP_STRIDE = 16
C_CMP_LEN = 2 * C_CMP_STRIDE
C_SLC_BLOCK = 64
C_TOPN = 16
C_WINDOW = 512
C_QBLOCK = 64
C_FORCE = 1e4
D_WIDTH = BR_WIDTH
D_CONV = 31
D_FF = 4 * D_MODEL
A_CACHE_DIM = 2 * HEAD_DIM + A_IDX_DIM
C_CACHE_DIM = 4 * HEAD_DIM
C_WIN_DIM = 2 * HEAD_DIM
IN_WIDTHS = (A_HEADS * HEAD_DIM, HEAD_DIM, HEAD_DIM, A_IDX_HEADS * A_IDX_DIM, A_IDX_DIM, A_IDX_HEADS,
             B_WIDTH, C_HEADS * HEAD_DIM, 6 * HEAD_DIM, 3 * C_HEADS, 2 * D_WIDTH, N_BRANCH * D_MODEL)
IN_TOTAL = sum(IN_WIDTHS)
IN_OFFSETS = tuple(int(o) for o in np.cumsum(IN_WIDTHS)[:-1])

kernel_name = 'hybrid_dsa_s5_nsa_conformer_step'


def rms_norm(x, g):
    xf = x.astype(jnp.float32)
    y = xf * lax.rsqrt(jnp.mean(xf * xf, axis=-1, keepdims=True) + NORM_EPS)
    return (y * g.astype(jnp.float32)).astype(x.dtype)


def layer_norm(x, g, b):
    xf = x.astype(jnp.float32)
    xc = xf - jnp.mean(xf, axis=-1, keepdims=True)
    y = xc * lax.rsqrt(jnp.mean(xc * xc, axis=-1, keepdims=True) + NORM_EPS)
    return (y * g.astype(jnp.float32) + b.astype(jnp.float32)).astype(x.dtype)


def rope(x, pos):
    half = x.shape[-1] // 2
    inv = ROPE_THETA ** (-jnp.arange(half, dtype=jnp.float32) / half)
    ang = pos.astype(jnp.float32)[:, None] * inv
    cos = jnp.cos(ang)[:, None, :]
    sin = jnp.sin(ang)[:, None, :]
    xf = x.astype(jnp.float32)
    x1, x2 = xf[..., :half], xf[..., half:]
    return jnp.concatenate([x1 * cos - x2 * sin, x2 * cos + x1 * sin], axis=-1).astype(x.dtype)


def masked_softmax(s, mask):
    s = jnp.where(mask, s.astype(jnp.float32), -1e30)
    m = jnp.max(s, axis=-1, keepdims=True)
    e = jnp.where(mask, jnp.exp(s - m), 0.0)
    return e / jnp.maximum(jnp.sum(e, axis=-1, keepdims=True), 1e-30)


def gather_rows(rows, idx):
    return jax.vmap(lambda r, i: r[i])(rows, idx)


def map_query_blocks(fn, qargs, pos, qb):
    B, T = qargs[0].shape[:2]
    nb = T // qb
    blocks = tuple(jnp.swapaxes(a.reshape((B, nb, qb) + a.shape[2:]), 0, 1) for a in qargs)
    out = lax.map(lambda args: fn(*args), blocks + (pos.reshape(nb, qb),))
    return jnp.swapaxes(out, 0, 1).reshape((B, T) + out.shape[3:])


def project(x, pos, p):
    B, T, _ = x.shape
    xn = rms_norm(x, p['norm_mix'])
    z = xn @ p['w_in']
    aq, ak, av, aiq, aik, aiw, bu, cq, ckv, cg, dglu, gt = jnp.split(z, IN_OFFSETS, axis=-1)
    aq = rope(rms_norm(aq.reshape(B, T, A_HEADS, HEAD_DIM), p['a_gq']), pos)
    ak = rope(rms_norm(ak.reshape(B, T, 1, HEAD_DIM), p['a_gk']), pos)[:, :, 0]
    aiq = rope(aiq.reshape(B, T, A_IDX_HEADS, A_IDX_DIM), pos)
    aik = rope(aik.reshape(B, T, 1, A_IDX_DIM), pos)[:, :, 0]
    aiw = aiw * A_IDX_HEADS ** -0.5
    cq = rope(rms_norm(cq.reshape(B, T, C_HEADS, HEAD_DIM), p['c_gq']), pos)
    ckv = ckv.reshape(B, T, 3, 2, HEAD_DIM)
    ck = rope(rms_norm(ckv[:, :, :, 0], p['c_gk']), pos)
    kv = jnp.stack([ck, ckv[:, :, :, 1]], axis=3)
    cg = jax.nn.sigmoid(cg.reshape(B, T, C_HEADS, 3))
    da = dglu[..., :D_WIDTH] * jax.nn.sigmoid(dglu[..., D_WIDTH:])
    gt = jax.nn.sigmoid(gt.reshape(B, T, N_BRANCH, D_MODEL))
    return aq, ak, av, aiq, aik, aiw, bu, cq, kv, cg, da, gt


def dsa_attend(q, iq, iw, k, v, ik, q_pos):
    L = k.shape[1]
    n_sel = min(A_TOPK, L // 4)
    k_pos = jnp.arange(L)
    logits = jnp.einsum('bqhd,bsd->bqhs', iq, ik, preferred_element_type=jnp.float32) * A_IDX_DIM ** -0.5
    score = jnp.einsum('bqhs,bqh->bqs', jax.nn.relu(logits), iw.astype(jnp.float32))
    score = jnp.where((k_pos[None, :] <= q_pos[:, None])[None], score, -jnp.inf)
    _, idx = lax.top_k(score, n_sel)
    valid = idx <= q_pos[None, :, None]
    k_sel = gather_rows(k, idx)
    v_sel = gather_rows(v, idx)
    s = jnp.einsum('bqhd,bqnd->bqhn', q, k_sel, preferred_element_type=jnp.float32) * HEAD_DIM ** -0.5
    pr = masked_softmax(s, valid[:, :, None, :])
    return jnp.einsum('bqhn,bqnd->bqhd', pr.astype(v_sel.dtype), v_sel)


def compress(rows, a, w):
    B, L, dh = rows.shape
    n = (L - C_CMP_LEN) // C_CMP_STRIDE + 1
    sub = rows[:, :(n + 1) * C_CMP_STRIDE].reshape(B, n + 1, C_CMP_STRIDE, dh)
    lo = jnp.einsum('bjrd,rd->bjd', sub, a[:C_CMP_STRIDE])
    hi = jnp.einsum('bjrd,rd->bjd', sub, a[C_CMP_STRIDE:])
    return (lo[:, :-1] + hi[:, 1:]) @ w


def to_blocks(rows):
    B, L, d = rows.shape
    n = -(-L // C_SLC_BLOCK)
    rows = jnp.pad(rows, ((0, 0), (0, n * C_SLC_BLOCK - L), (0, 0)))
    return rows.reshape(B, n, C_SLC_BLOCK, d)


def overlap_matrix(n_cmp, n_blk):
    start = np.arange(n_cmp)[:, None] * C_CMP_STRIDE
    blk = np.arange(n_blk)[None, :]
    m = (start <= (blk + 1) * C_SLC_BLOCK - 1) & (start + C_CMP_LEN - 1 >= blk * C_SLC_BLOCK)
    return jnp.asarray(m.astype(np.float32))


def nsa_attend(q, g, q_pos, kcmp, vcmp, ksb, vsb, kw, vw, kw_pos):
    scale = HEAD_DIM ** -0.5
    n_cmp, n_blk = kcmp.shape[1], ksb.shape[1]
    B, Tq = q.shape[:2]
    cmp_end = jnp.asarray(np.arange(n_cmp) * C_CMP_STRIDE + C_CMP_LEN - 1)
    s_c = jnp.einsum('bqhd,bnd->bqhn', q, kcmp, preferred_element_type=jnp.float32) * scale
    p_c = masked_softmax(s_c, (cmp_end[None, :] <= q_pos[:, None])[None, :, None, :])
    o_c = jnp.einsum('bqhn,bnd->bqhd', p_c.astype(vcmp.dtype), vcmp)
    imp = jnp.einsum('bqn,nj->bqj', jnp.sum(p_c, axis=2), overlap_matrix(n_cmp, n_blk))
    blk = jnp.arange(n_blk)[None, :]
    cur = (q_pos // C_SLC_BLOCK)[:, None]
    forced = (blk == 0) | (blk == cur) | (blk == cur - 1)
    sc = jnp.where((blk <= cur)[None], jnp.where(forced[None], C_FORCE, imp), -jnp.inf)
    _, idx = lax.top_k(sc, min(C_TOPN, n_blk))
    n_keys = idx.shape[-1] * C_SLC_BLOCK
    k_sel = gather_rows(ksb, idx).reshape(B, Tq, n_keys, HEAD_DIM)
    v_sel = gather_rows(vsb, idx).reshape(B, Tq, n_keys, HEAD_DIM)
    pos_sel = (idx[..., None] * C_SLC_BLOCK + jnp.arange(C_SLC_BLOCK)).reshape(B, Tq, n_keys)
    s_s = jnp.einsum('bqhd,bqnd->bqhn', q, k_sel, preferred_element_type=jnp.float32) * scale
    p_s = masked_softmax(s_s, (pos_sel <= q_pos[None, :, None])[:, :, None, :])
    o_s = jnp.einsum('bqhn,bqnd->bqhd', p_s.astype(v_sel.dtype), v_sel)
    dist = q_pos[:, None] - kw_pos[None, :]
    mask_w = (kw_pos[None, :] >= 0) & (dist >= 0) & (dist <= C_WINDOW)
    s_w = jnp.einsum('bqhd,bkd->bqhk', q, kw, preferred_element_type=jnp.float32) * scale
    p_w = masked_softmax(s_w, mask_w[None, :, None, :])
    o_w = jnp.einsum('bqhk,bkd->bqhd', p_w.astype(vw.dtype), vw)
    return g[..., 0:1] * o_c + g[..., 1:2] * o_s + g[..., 2:3] * o_w


def s5_scan(u, h0_re, h0_im, p):
    B, T, _ = u.shape
    f32 = jnp.float32
    uf = u.astype(f32).reshape(B, T, B_GROUPS, B_GROUP)
    lr, li = p['b_lam_re'].astype(f32), p['b_lam_im'].astype(f32)
    dt = jnp.exp(p['b_log_dt'].astype(f32))[:, None]
    mag = jnp.exp(lr * dt)
    ar, ai = mag * jnp.cos(li * dt), mag * jnp.sin(li * dt)
    den = lr * lr + li * li
    fr = ((ar - 1.0) * lr + ai * li) / den
    fi = (ai * lr - (ar - 1.0) * li) / den
    br, bi = p['b_b_re'].astype(f32), p['b_b_im'].astype(f32)
    bbr = fr[..., None] * br - fi[..., None] * bi
    bbi = fr[..., None] * bi + fi[..., None] * br
    xr = jnp.einsum('gpc,btgc->btgp', bbr, uf)
    xi = jnp.einsum('gpc,btgc->btgp', bbi, uf)
    h0r, h0i = h0_re.astype(f32), h0_im.astype(f32)
    xr = xr.at[:, 0].add(ar * h0r - ai * h0i)
    xi = xi.at[:, 0].add(ar * h0i + ai * h0r)
    a_r = jnp.broadcast_to(ar, xr.shape)
    a_i = jnp.broadcast_to(ai, xi.shape)

    def combine(e1, e2):
        a1r, a1i, b1r, b1i = e1
        a2r, a2i, b2r, b2i = e2
        return (a1r * a2r - a1i * a2i, a1r * a2i + a1i * a2r,
                a2r * b1r - a2i * b1i + b2r, a2r * b1i + a2i * b1r + b2i)

    _, _, hr, hi = lax.associative_scan(combine, (a_r, a_i, xr, xi), axis=1)
    cr, ci = p['b_c_re'].astype(f32), p['b_c_im'].astype(f32)
    y = (jnp.einsum('gcp,btgp->btgc', cr, hr) - jnp.einsum('gcp,btgp->btgc', ci, hi)
         + p['b_d'].astype(f32).reshape(B_GROUPS, B_GROUP) * uf)
    return y.reshape(B, T, B_WIDTH).astype(u.dtype), hr[:, -1], hi[:, -1]


def s5_glu(y, p):
    h = jax.nn.gelu(y)
    return h * jax.nn.sigmoid(h @ p['b_glu_w'] + p['b_glu_b'])


def conformer_tail(a_ext, p):
    w = p['d_conv_w'][:, None, :].astype(a_ext.dtype)
    y = lax.conv_general_dilated(a_ext, w, (1,), 'VALID', dimension_numbers=('NWC', 'WIO', 'NWC'),
                                 feature_group_count=D_WIDTH) + p['d_conv_b']
    return jax.nn.silu(layer_norm(y, p['d_ln_g'], p['d_ln_b']))


def post_mix(x, branches, gt, p):
    merged = jnp.zeros_like(x)
    for i in range(N_BRANCH):
        merged = merged + gt[:, :, i] * (branches[i] @ p['w_br'][i])
    h = x + merged @ p['w_o']
    hn = rms_norm(h, p['norm_mlp'])
    return h + jnp.square(jax.nn.relu(hn @ p['w_up'])) @ p['w_down']


def layer_prompt(x, p):
    B, T, _ = x.shape
    pos = jnp.arange(T)
    aq, ak, av, aiq, aik, aiw, bu, cq, kv, cg, da, gt = project(x, pos, p)
    o_a = map_query_blocks(lambda q, iq, iw, qp: dsa_attend(q, iq, iw, ak, av, aik, qp),
                           (aq, aiq, aiw), pos, A_QBLOCK)
    kcmp = compress(kv[:, :, 0, 0], p['c_cmp_a'][0], p['c_cmp_w'][0])
    vcmp = compress(kv[:, :, 0, 1], p['c_cmp_a'][1], p['c_cmp_w'][1])
    ksb, vsb = to_blocks(kv[:, :, 1, 0]), to_blocks(kv[:, :, 1, 1])
    pad = ((0, 0), (C_WINDOW, 0), (0, 0))
    kw_pad, vw_pad = jnp.pad(kv[:, :, 2, 0], pad), jnp.pad(kv[:, :, 2, 1], pad)
    n_w = C_WINDOW + C_QBLOCK

    def nsa_block(q, g, qp):
        start = qp[0]
        kw = lax.dynamic_slice_in_dim(kw_pad, start, n_w, axis=1)
        vw = lax.dynamic_slice_in_dim(vw_pad, start, n_w, axis=1)
        kw_pos = start - C_WINDOW + jnp.arange(n_w)
        return nsa_attend(q, g, qp, kcmp, vcmp, ksb, vsb, kw, vw, kw_pos)

    o_c = map_query_blocks(nsa_block, (cq, cg), pos, C_QBLOCK)
    h0 = jnp.zeros((B, B_GROUPS, B_STATE), jnp.float32)
    y_b, hr, hi = s5_scan(bu, h0, h0, p)
    o_b = s5_glu(y_b, p)
    o_d = conformer_tail(jnp.pad(da, ((0, 0), (D_CONV - 1, 0), (0, 0))), p)
    y = post_mix(x, (o_a.reshape(B, T, BR_WIDTH), o_b, o_c.reshape(B, T, BR_WIDTH), o_d), gt, p)
    a_rows = jnp.concatenate([ak, av, aik], axis=-1)
    c_rows = kv[:, :, :2].reshape(B, T, C_CACHE_DIM)
    w_rows = kv[:, :, 2].reshape(B, T, C_WIN_DIM)[:, T - min(C_WINDOW, T):]
    return y, a_rows, c_rows, w_rows, hr, hi, da[:, T - (D_CONV - 1):]


def layer_sample(x, cache_a_l, cache_c_l, win_l, h_re, h_im, conv_l, page_table, p):
    B, T, _ = x.shape
    past = page_table.shape[1] * cache_a_l.shape[1]
    pos = past + jnp.arange(T)
    aq, ak, av, aiq, aik, aiw, bu, cq, kv, cg, da, gt = project(x, pos, p)
    a_rows = jnp.concatenate([ak, av, aik], axis=-1)
    a_all = jnp.concatenate([cache_a_l[page_table].reshape(B, past, A_CACHE_DIM),
                             a_rows.astype(cache_a_l.dtype)], axis=1)
    o_a = dsa_attend(aq, aiq, aiw, a_all[..., :HEAD_DIM], a_all[..., HEAD_DIM:2 * HEAD_DIM],
                     a_all[..., 2 * HEAD_DIM:], pos)
    c_rows = kv[:, :, :2].reshape(B, T, C_CACHE_DIM)
    c_all = jnp.concatenate([cache_c_l[page_table].reshape(B, past, C_CACHE_DIM),
                             c_rows.astype(cache_c_l.dtype)], axis=1).reshape(B, past + T, 2, 2, HEAD_DIM)
    kcmp = compress(c_all[:, :, 0, 0], p['c_cmp_a'][0], p['c_cmp_w'][0])
    vcmp = compress(c_all[:, :, 0, 1], p['c_cmp_a'][1], p['c_cmp_w'][1])
    ksb, vsb = to_blocks(c_all[:, :, 1, 0]), to_blocks(c_all[:, :, 1, 1])
    w_rows = kv[:, :, 2].reshape(B, T, C_WIN_DIM)
    wb = win_l.shape[1]
    w_all = jnp.concatenate([win_l, w_rows.astype(win_l.dtype)], axis=1)
    kw_pos = past - wb + jnp.arange(wb + T)
    o_c = nsa_attend(cq, cg, pos, kcmp, vcmp, ksb, vsb, w_all[..., :HEAD_DIM], w_all[..., HEAD_DIM:], kw_pos)
    y_b, hr, hi = s5_scan(bu, h_re, h_im, p)
    o_b = s5_glu(y_b, p)
    d_ext = jnp.concatenate([conv_l.astype(da.dtype), da], axis=1)
    o_d = conformer_tail(d_ext, p)
    y = post_mix(x, (o_a.reshape(B, T, BR_WIDTH), o_b, o_c.reshape(B, T, BR_WIDTH), o_d), gt, p)
    return y, a_rows, c_rows, w_all[:, T:], hr, hi, d_ext[:, T:]


def setup_inputs(seed: int = 0) -> dict:
    key = jax.random.key(seed)
    ks = iter(jax.random.split(key, 48))
    f32 = jnp.float32

    def nrm(shape, scale=1.0):
        return jax.random.normal(next(ks), shape, f32) * scale

    def gain(shape):
        return 1.0 + nrm(shape, 0.02)

    n_pages = PAST_LEN // PAGE_SIZE
    n_pool = (DEC_BATCH * n_pages * 5) // 4
    wb = min(C_WINDOW, PAST_LEN)
    Dn = DEPTH
    page_table = jax.random.permutation(next(ks), n_pool)[:DEC_BATCH * n_pages].reshape(DEC_BATCH, n_pages).astype(jnp.int32)
    lam_im = math.pi * jnp.arange(B_STATE, dtype=f32)
    return {
        'x_prompt': nrm((BATCH, SEQ, D_MODEL)),
        'x_sample': nrm((DEC_BATCH, DEC_SEQ, D_MODEL)),
        'cache_a': nrm((Dn, n_pool, PAGE_SIZE, A_CACHE_DIM)),
        'cache_c': nrm((Dn, n_pool, PAGE_SIZE, C_CACHE_DIM)),
        'cache_c_win': nrm((Dn, DEC_BATCH, wb, C_WIN_DIM)),
        'state_b_re': nrm((Dn, DEC_BATCH, B_GROUPS, B_STATE), 0.5),
        'state_b_im': nrm((Dn, DEC_BATCH, B_GROUPS, B_STATE), 0.5),
        'state_d_conv': nrm((Dn, DEC_BATCH, D_CONV - 1, D_WIDTH), 0.5),
        'page_table': page_table,
        'norm_mix': gain((Dn, D_MODEL)),
        'w_in': nrm((Dn, D_MODEL, IN_TOTAL), D_MODEL ** -0.5),
        'a_gq': gain((Dn, HEAD_DIM)),
        'a_gk': gain((Dn, HEAD_DIM)),
        'b_lam_re': -0.5 + nrm((Dn, B_GROUPS, B_STATE), 0.01),
        'b_lam_im': lam_im + nrm((Dn, B_GROUPS, B_STATE), 0.01),
        'b_log_dt': jax.random.uniform(next(ks), (Dn, B_GROUPS), f32, math.log(1e-3), math.log(1e-1)),
        'b_b_re': nrm((Dn, B_GROUPS, B_STATE, B_GROUP), (2 * B_GROUP) ** -0.5),
        'b_b_im': nrm((Dn, B_GROUPS, B_STATE, B_GROUP), (2 * B_GROUP) ** -0.5),
        'b_c_re': nrm((Dn, B_GROUPS, B_GROUP, B_STATE), B_STATE ** -0.5),
        'b_c_im': nrm((Dn, B_GROUPS, B_GROUP, B_STATE), B_STATE ** -0.5),
        'b_d': nrm((Dn, B_WIDTH)),
        'b_glu_w': nrm((Dn, B_WIDTH, B_WIDTH), B_WIDTH ** -0.5),
        'b_glu_b': nrm((Dn, B_WIDTH), 0.01),
        'c_gq': gain((Dn, HEAD_DIM)),
        'c_gk': gain((Dn, 3, HEAD_DIM)),
        'c_cmp_a': C_CMP_LEN ** -0.5 * (1.0 + nrm((Dn, 2, C_CMP_LEN, HEAD_DIM), 0.1)),
        'c_cmp_w': nrm((Dn, 2, HEAD_DIM, HEAD_DIM), HEAD_DIM ** -0.5),
        'd_conv_w': nrm((Dn, D_CONV, D_WIDTH), D_CONV ** -0.5),
        'd_conv_b': nrm((Dn, D_WIDTH), 0.01),
        'd_ln_g': gain((Dn, D_WIDTH)),
        'd_ln_b': nrm((Dn, D_WIDTH), 0.01),
        'w_br': nrm((Dn, N_BRANCH, BR_WIDTH, D_MODEL), BR_WIDTH ** -0.5),
        'w_o': nrm((Dn, D_MODEL, D_MODEL), D_MODEL ** -0.5),
        'norm_mlp': gain((Dn, D_MODEL)),
        'w_up': nrm((Dn, D_MODEL, D_FF), D_MODEL ** -0.5),
        'w_down': nrm((Dn, D_FF, D_MODEL), 0.5 * D_FF ** -0.5),
    }


def reference(x_prompt, x_sample, cache_a, cache_c, cache_c_win, state_b_re, state_b_im, state_d_conv,
              page_table, norm_mix, w_in, a_gq, a_gk, b_lam_re, b_lam_im, b_log_dt, b_b_re, b_b_im,
              b_c_re, b_c_im, b_d, b_glu_w, b_glu_b, c_gq, c_gk, c_cmp_a, c_cmp_w, d_conv_w, d_conv_b,
              d_ln_g, d_ln_b, w_br, w_o, norm_mlp, w_up, w_down):
    xp, xs = x_prompt, x_sample
    ap, a_s, cp, cs, wp, ws, brp, bip, brs, bis, dp, ds = ([] for _ in range(12))
    for l in range(DEPTH):
        p = dict(norm_mix=norm_mix[l], w_in=w_in[l], a_gq=a_gq[l], a_gk=a_gk[l],
                 b_lam_re=b_lam_re[l], b_lam_im=b_lam_im[l], b_log_dt=b_log_dt[l],
                 b_b_re=b_b_re[l], b_b_im=b_b_im[l], b_c_re=b_c_re[l], b_c_im=b_c_im[l], b_d=b_d[l],
                 b_glu_w=b_glu_w[l], b_glu_b=b_glu_b[l], c_gq=c_gq[l], c_gk=c_gk[l],
                 c_cmp_a=c_cmp_a[l], c_cmp_w=c_cmp_w[l], d_conv_w=d_conv_w[l], d_conv_b=d_conv_b[l],
                 d_ln_g=d_ln_g[l], d_ln_b=d_ln_b[l], w_br=w_br[l], w_o=w_o[l], norm_mlp=norm_mlp[l],
                 w_up=w_up[l], w_down=w_down[l])
        xp, r_a, r_c, r_w, h_r, h_i, r_d = layer_prompt(xp, p)
        ap.append(r_a); cp.append(r_c); wp.append(r_w); brp.append(h_r); bip.append(h_i); dp.append(r_d)
        xs, r_a, r_c, r_w, h_r, h_i, r_d = layer_sample(xs, cache_a[l], cache_c[l], cache_c_win[l],
                                                        state_b_re[l], state_b_im[l], state_d_conv[l],
                                                        page_table, p)
        a_s.append(r_a); cs.append(r_c); ws.append(r_w); brs.append(h_r); bis.append(h_i); ds.append(r_d)
    return (xp, xs, jnp.stack(ap), jnp.stack(a_s), jnp.stack(cp), jnp.stack(cs), jnp.stack(wp), jnp.stack(ws),
            jnp.stack(brp), jnp.stack(bip), jnp.stack(brs), jnp.stack(bis), jnp.stack(dp), jnp.stack(ds))
```

```python
import functools
import math

import numpy as np
import jax
import jax.numpy as jnp
from jax import lax
from jax.experimental import pallas as pl
from jax.experimental.pallas import tpu as pltpu

F32 = jnp.float32
BF16 = jnp.bfloat16
I32 = jnp.int32

D_MODEL = 2048
HEAD_DIM = 128
N_BRANCH = 4
BR_WIDTH = D_MODEL // N_BRANCH
ROPE_THETA = 10000.0
NORM_EPS = 1e-6
A_HEADS = BR_WIDTH // HEAD_DIM
A_IDX_HEADS = 8
A_IDX_DIM = 64
A_TOPK = 256
B_GROUP = 16
B_GROUPS = BR_WIDTH // B_GROUP
B_STATE = 64
B_LANES = B_GROUPS * B_STATE
C_HEADS = BR_WIDTH // HEAD_DIM
C_CMP_STRIDE = 16
C_CMP_LEN = 2 * C_CMP_STRIDE
C_SLC_BLOCK = 64
C_TOPN = 16
C_WINDOW = 512
C_FORCE = 1e4
D_CONV = 31
D_FF = 4 * D_MODEL
A_CACHE_DIM = 2 * HEAD_DIM + A_IDX_DIM
C_CACHE_DIM = 4 * HEAD_DIM
C_WIN_DIM = 2 * HEAD_DIM
PAGE = 128

LANES = 128
SUBLANES = 8
VMEM_LIMIT_MB = 56

_W = (A_HEADS * HEAD_DIM, HEAD_DIM, HEAD_DIM, A_IDX_HEADS * A_IDX_DIM, A_IDX_DIM, A_IDX_HEADS,
      BR_WIDTH, C_HEADS * HEAD_DIM, 6 * HEAD_DIM, 3 * C_HEADS, 2 * BR_WIDTH, N_BRANCH * D_MODEL)
_O = tuple(int(v) for v in np.cumsum((0,) + _W))
_P = {}
_cur = 0
for _name, _w in (("aq", 512), ("ak", 128), ("av", 128), ("aiq", 512), ("aik", 128), ("aiw", 128),
                  ("bu", 512), ("cq", 512), ("ckv", 768), ("cg", 128), ("dglu", 1024)):
    _P[_name] = (_cur, _w)
    _cur += _w
P_TOTAL = _cur

NEG = -1e30
ATT_SCALE = HEAD_DIM ** -0.5
SIGN = -2 ** 31


def _cparams(n_axes):
    return pltpu.CompilerParams(dimension_semantics=("arbitrary",) * n_axes,
                                vmem_limit_bytes=VMEM_LIMIT_MB * 1024 * 1024)


def _full(a, n_grid):
    nd = a.ndim
    return pl.BlockSpec(a.shape, lambda *_: (0,) * nd)


def _dot(a, b):
    return jnp.dot(a, b, preferred_element_type=F32)


def _dot_nt(a, b):
    return lax.dot_general(a, b, (((1,), (1,)), ((), ())), preferred_element_type=F32)


def _dot_split3(p, m):
    hi = p.astype(BF16)
    r = p - hi.astype(F32)
    mid = r.astype(BF16)
    lo = (r - mid.astype(F32)).astype(BF16)
    return _dot(hi, m) + _dot(mid, m) + _dot(lo, m)


def _rms(x, g):
    return x * lax.rsqrt(jnp.mean(x * x, axis=-1, keepdims=True) + NORM_EPS) * g


def _sigmoid(x):
    return 1.0 / (1.0 + jnp.exp(-x))


def _gelu(x):
    return x * (0.5 * (1.0 + jnp.tanh(math.sqrt(2.0 / math.pi) * (x + 0.044715 * (x * x * x)))))


def _sortable(x):
    b = pltpu.bitcast(x + 0.0, I32)
    return jnp.where(b < 0, b ^ jnp.int32(0x7FFFFFFF), b)


def _kth_key(key_ref, k, red_axes):
    shp = tuple(1 if a in red_axes else s for a, s in enumerate(key_ref.shape))

    def count(mask):
        c = jnp.where(mask, 1.0, 0.0)
        for a in sorted(red_axes):
            c = jnp.sum(c, axis=a, keepdims=True)
        return c

    def body(it, tu):
        cand_u = tu | jnp.left_shift(jnp.int32(1), 31 - it)
        cand_s = cand_u ^ jnp.int32(SIGN)
        return jnp.where(count(key_ref[...] >= cand_s) >= k, cand_u, tu)

    tu = lax.fori_loop(0, 32, body, jnp.zeros(shp, I32))
    return tu ^ jnp.int32(SIGN), count


def _select_rows(key_ref, k, tri):
    ts, count = _kth_key(key_ref, k, (1,))
    keys = key_ref[...]
    gt = keys > ts
    need = k - count(gt)
    eqf = jnp.where(keys == ts, 1.0, 0.0)
    base = jnp.zeros_like(need)
    pieces = []
    for c in range(keys.shape[1] // LANES):
        ch = eqf[:, c * LANES:(c + 1) * LANES]
        pref = _dot(ch.astype(BF16), tri) + base
        pieces.append(jnp.where(pref < need, ch, 0.0))
        base = base + jnp.sum(ch, axis=-1, keepdims=True)
    sel_eq = pieces[0] if len(pieces) == 1 else jnp.concatenate(pieces, axis=1)
    return jnp.where(gt, 1.0, sel_eq)


def _select_tiles(key_ref, base_ref, k, tri):
    nr = key_ref.shape[0]
    ts, count = _kth_key(key_ref, k, (0, 2))
    keys = key_ref[...]
    gt = keys > ts
    need = k - count(gt)
    eqf = jnp.where(keys == ts, 1.0, 0.0)
    within = _dot(eqf.reshape(nr * SUBLANES, LANES).astype(BF16), tri).reshape(nr, SUBLANES, LANES)
    tot = jnp.sum(eqf, axis=2, keepdims=True)
    run = jnp.zeros((SUBLANES, 1), F32)
    for c in range(nr):
        base_ref[c] = jnp.broadcast_to(run, (SUBLANES, LANES))
        run = run + tot[c]
    sel_eq = jnp.where(within + base_ref[...] < need, eqf, 0.0)
    return jnp.where(gt, 1.0, sel_eq)


def _masked_attn(qh, k, v, mask):
    s = _dot_nt(qh, k) * ATT_SCALE
    s = jnp.where(mask, s, NEG)
    m = jnp.max(s, axis=-1, keepdims=True)
    e = jnp.where(mask, jnp.exp(s - m), 0.0)
    den = jnp.maximum(jnp.sum(e, axis=-1, keepdims=True), 1e-30)
    return _dot(e.astype(BF16), v) / den


def _proj_kernel(x_ref, g_ref, w_ref, c128_ref, s128_ref, c64_ref, s64_ref, gqa_ref, gka_ref, gqc_ref, gkc_ref,
                 xn_ref, aq_ref, arow_ref, aiq_ref, aiw_ref, bu_ref, cq_ref, crow_ref, wrow_ref, cg_ref, da_ref):
    xn = _rms(x_ref[...], g_ref[...]).astype(BF16)
    xn_ref[...] = xn
    cos, sin = c128_ref[...], s128_ref[...]
    cos64, sin64 = c64_ref[...], s64_ref[...]
    lane = lax.broadcasted_iota(I32, cos.shape, 1)
    lo32 = (lane & 63) < 32
    lo64 = lane < 64

    def seg(name):
        a, w = _P[name]
        return _dot(xn, w_ref[:, a:a + w])

    def rope128(v):
        return v * cos + pltpu.roll(v, 64, 1) * sin

    def rope64(v):
        rot = jnp.where(lo32, pltpu.roll(v, 96, 1), pltpu.roll(v, 32, 1))
        return v * cos64 + rot * sin64

    z = seg("aq")
    for h in range(A_HEADS):
        sl = slice(h * HEAD_DIM, (h + 1) * HEAD_DIM)
        aq_ref[:, sl] = rope128(_rms(z[:, sl], gqa_ref[...])).astype(BF16)
    arow_ref[:, 0:128] = rope128(_rms(seg("ak"), gka_ref[...]))
    arow_ref[:, 128:256] = seg("av")
    arow_ref[:, 256:320] = rope64(seg("aik"))[:, 0:A_IDX_DIM]
    z = seg("aiq")
    for j in range(A_IDX_HEADS // 2):
        r = rope64(z[:, j * LANES:(j + 1) * LANES])
        aiq_ref[:, (2 * j) * LANES:(2 * j + 1) * LANES] = jnp.where(lo64, r, 0.0).astype(BF16)
        aiq_ref[:, (2 * j + 1) * LANES:(2 * j + 2) * LANES] = jnp.where(lo64, 0.0, r).astype(BF16)
    aiw_ref[...] = seg("aiw") * (A_IDX_HEADS ** -0.5) * (A_IDX_DIM ** -0.5)
    bu_ref[...] = seg("bu")
    z = seg("cq")
    for h in range(C_HEADS):
        sl = slice(h * HEAD_DIM, (h + 1) * HEAD_DIM)
        cq_ref[:, sl] = rope128(_rms(z[:, sl], gqc_ref[...])).astype(BF16)
    z = seg("ckv")
    for br in range(3):
        kk = rope128(_rms(z[:, (2 * br) * LANES:(2 * br + 1) * LANES], gkc_ref[br:br + 1, :]))
        vv = z[:, (2 * br + 1) * LANES:(2 * br + 2) * LANES]
        if br < 2:
            crow_ref[:, (2 * br) * LANES:(2 * br + 1) * LANES] = kk
            crow_ref[:, (2 * br + 1) * LANES:(2 * br + 2) * LANES] = vv
        else:
            wrow_ref[:, 0:LANES] = kk
            wrow_ref[:, LANES:2 * LANES] = vv
    cg_ref[...] = _sigmoid(seg("cg"))
    z = seg("dglu")
    da_ref[...] = z[:, 0:BR_WIDTH] * _sigmoid(z[:, BR_WIDTH:2 * BR_WIDTH])


def _project(x2d, tabs, lw, *, tm, n_pos_blocks, bu_shape, bu_spec):
    n = x2d.shape[0]
    row = lambda w: pl.BlockSpec((tm, w), lambda i: (i, 0))
    tab = pl.BlockSpec((tm, LANES), lambda i: (i % n_pos_blocks, 0))
    ins = [x2d, lw["g_mix"], lw["w1"], tabs[0], tabs[1], tabs[2], tabs[3], lw["a_gq"], lw["a_gk"], lw["c_gq"], lw["c_gk"]]
    in_specs = [row(D_MODEL), _full(ins[1], 1), _full(ins[2], 1), tab, tab, tab, tab] + [_full(a, 1) for a in ins[7:]]
    outs = [((n, D_MODEL), BF16, row(D_MODEL)),
            ((n, 512), BF16, row(512)),
            ((n, A_CACHE_DIM), F32, row(A_CACHE_DIM)),
            ((n, 1024), BF16, row(1024)),
            ((n, LANES), F32, row(LANES)),
            (bu_shape, F32, bu_spec),
            ((n, 512), BF16, row(512)),
            ((n, C_CACHE_DIM), F32, row(C_CACHE_DIM)),
            ((n, C_WIN_DIM), F32, row(C_WIN_DIM)),
            ((n, LANES), F32, row(LANES)),
            ((n, BR_WIDTH), F32, row(BR_WIDTH))]
    return pl.pallas_call(
        _proj_kernel,
        grid=(n // tm,),
        in_specs=in_specs,
        out_specs=[o[2] for o in outs],
        out_shape=[jax.ShapeDtypeStruct(o[0], o[1]) for o in outs],
        compiler_params=_cparams(1),
        name="project",
    )(*ins)


def _dsa_kernel(aq_ref, aiq_ref, aiw_ref, arow_ref, tri_ref, o_ref, k_sc, v_sc, ik_sc, key_sc, *, qb, L, n_sel):
    qi = pl.program_id(1)

    @pl.when(qi == 0)
    def _():
        k_sc[...] = arow_ref[:, 0:128].astype(BF16)
        v_sc[...] = arow_ref[:, 128:256].astype(BF16)
        ik = arow_ref[:, 256:320]
        ik_sc[...] = jnp.concatenate([ik, ik], axis=1).astype(BF16)

    w = aiw_ref[...]
    score = None
    for h in range(A_IDX_HEADS):
        lg = _dot_nt(aiq_ref[:, h * LANES:(h + 1) * LANES], ik_sc[...])
        t = jnp.maximum(lg, 0.0) * w[:, h:h + 1]
        score = t if score is None else score + t
    qpos = qi * qb + lax.broadcasted_iota(I32, (qb, 1), 0)
    valid = lax.broadcasted_iota(I32, (qb, L), 1) <= qpos
    key_sc[...] = _sortable(jnp.where(valid, score, -jnp.inf))
    sel = _select_rows(key_sc, n_sel, tri_ref[...])
    mask = jnp.where(valid, sel, 0.0) > 0.5
    q = aq_ref[...]
    for h in range(A_HEADS):
        sl = slice(h * HEAD_DIM, (h + 1) * HEAD_DIM)
        o_ref[:, sl] = _masked_attn(q[:, sl], k_sc[...], v_sc[...], mask).astype(BF16)


def _dsa(aq, aiq, aiw, arow3, tri, *, qb):
    bsz, L, _ = arow3.shape
    nq = L // qb
    n_sel = min(A_TOPK, L // 4)
    row = lambda w: pl.BlockSpec((qb, w), lambda b, i: (b * nq + i, 0))
    return pl.pallas_call(
        functools.partial(_dsa_kernel, qb=qb, L=L, n_sel=n_sel),
        grid=(bsz, nq),
        in_specs=[row(512), row(1024), row(LANES),
                  pl.BlockSpec((None, L, A_CACHE_DIM), lambda b, i: (b, 0, 0)), _full(tri, 2)],
        out_specs=row(512),
        out_shape=jax.ShapeDtypeStruct((bsz * L, 512), BF16),
        scratch_shapes=[pltpu.VMEM((L, 128), BF16), pltpu.VMEM((L, 128), BF16), pltpu.VMEM((L, 128), BF16),
                        pltpu.VMEM((qb, L), I32)],
        compiler_params=_cparams(2),
        name="dsa",
    )(aq, aiq, aiw, arow3, tri)


def _summaries(x, a, w, n_cmp):
    ns = x.shape[0] // C_CMP_STRIDE
    x3 = x.reshape(ns, C_CMP_STRIDE, HEAD_DIM)
    lo = jnp.sum(x3 * a[0:C_CMP_STRIDE][None], axis=1)
    hi = jnp.sum(x3 * a[C_CMP_STRIDE:C_CMP_LEN][None], axis=1)
    return lo, hi


def _finish_summaries(lo, hi, w, n_cmp):
    ns = lo.shape[0]
    comb = lo + pltpu.roll(hi, ns - 1, 0)
    comb = jnp.where(lax.broadcasted_iota(I32, comb.shape, 0) < n_cmp, comb, 0.0)
    return _dot(comb.astype(BF16), w).astype(BF16)


def _cmp_kernel(x_ref, a_ref, w_ref, kc_ref, vc_ref, *, n_cmp):
    for t, out in ((0, kc_ref), (1, vc_ref)):
        lo, hi = _summaries(x_ref[:, t * LANES:(t + 1) * LANES], a_ref[t], w_ref[t], n_cmp)
        out[...] = _finish_summaries(lo, hi, w_ref[t], n_cmp)


def _compress(crow3, cmp_a, cmp_w):
    bsz, L, _ = crow3.shape
    ns = L // C_CMP_STRIDE
    n_cmp = (L - C_CMP_LEN) // C_CMP_STRIDE + 1
    out = pl.BlockSpec((None, ns, HEAD_DIM), lambda b: (b, 0, 0))
    return pl.pallas_call(
        functools.partial(_cmp_kernel, n_cmp=n_cmp),
        grid=(bsz,),
        in_specs=[pl.BlockSpec((None, L, C_CACHE_DIM), lambda b: (b, 0, 0)), _full(cmp_a, 1), _full(cmp_w, 1)],
        out_specs=[out, out],
        out_shape=[jax.ShapeDtypeStruct((bsz, ns, HEAD_DIM), BF16)] * 2,
        compiler_params=_cparams(1),
        name="compress",
    )(crow3, cmp_a, cmp_w)


def _block_scores(imp, qpos):
    j = lax.broadcasted_iota(I32, imp.shape, 1)
    cur = lax.shift_right_logical(qpos, 6)
    forced = (j == 0) | (j == cur) | (j == cur - 1)
    return jnp.where(j <= cur, jnp.where(forced, C_FORCE, imp), -jnp.inf)


def _cmp_softmax(qh, kc, cvalid):
    s = _dot_nt(qh, kc) * ATT_SCALE
    s = jnp.where(cvalid, s, NEG)
    m = jnp.max(s, axis=-1, keepdims=True)
    e = jnp.where(cvalid, jnp.exp(s - m), 0.0)
    return e / jnp.maximum(jnp.sum(e, axis=-1, keepdims=True), 1e-30)


def _nsa_kernel(cq_ref, cg_ref, crow_ref, wrow_ref, kc_ref, vc_ref, ov_ref, ex_ref, tri_ref, o_ref,
                ks_sc, vs_sc, kw_sc, vw_sc, key_sc, *, qb, L, n_cmp, n_top, wsl):
    qi = pl.program_id(1)

    @pl.when(qi == 0)
    def _():
        ks_sc[...] = crow_ref[:, 256:384].astype(BF16)
        vs_sc[...] = crow_ref[:, 384:512].astype(BF16)
        kw_sc[...] = wrow_ref[:, 0:128].astype(BF16)
        vw_sc[...] = wrow_ref[:, 128:256].astype(BF16)

    q0 = qi * qb
    qpos = q0 + lax.broadcasted_iota(I32, (qb, 1), 0)
    q = cq_ref[...]
    g = cg_ref[...]
    kc, vc = kc_ref[...], vc_ref[...]
    ncp = kc.shape[0]
    n_io = lax.broadcasted_iota(I32, (qb, ncp), 1)
    cvalid = (n_io < n_cmp) & (n_io * C_CMP_STRIDE + (C_CMP_LEN - 1) <= qpos)
    o_c, psum = [], None
    for h in range(C_HEADS):
        p = _cmp_softmax(q[:, h * HEAD_DIM:(h + 1) * HEAD_DIM], kc, cvalid)
        o_c.append(_dot(p.astype(BF16), vc))
        psum = p if psum is None else psum + p
    imp = _dot_split3(psum, ov_ref[...])
    key_sc[...] = _sortable(_block_scores(imp, qpos))
    sel = _select_rows(key_sc, n_top, tri_ref[...])
    selk = _dot(sel.astype(BF16), ex_ref[...])
    smask = jnp.where(lax.broadcasted_iota(I32, (qb, L), 1) <= qpos, selk, 0.0) > 0.5
    start = pl.multiple_of(jnp.minimum(jnp.maximum(q0 - C_WINDOW, 0), L - wsl), qb)
    dist = qpos - (start + lax.broadcasted_iota(I32, (qb, wsl), 1))
    wmask = (dist >= 0) & (dist <= C_WINDOW)
    kw = kw_sc[pl.ds(start, wsl), :]
    vw = vw_sc[pl.ds(start, wsl), :]
    for h in range(C_HEADS):
        sl = slice(h * HEAD_DIM, (h + 1) * HEAD_DIM)
        o_s = _masked_attn(q[:, sl], ks_sc[...], vs_sc[...], smask)
        o_w = _masked_attn(q[:, sl], kw, vw, wmask)
        out = g[:, 3 * h:3 * h + 1] * o_c[h] + g[:, 3 * h + 1:3 * h + 2] * o_s + g[:, 3 * h + 2:3 * h + 3] * o_w
        o_ref[:, sl] = out.astype(BF16)


def _nsa(cq, cg, crow3, wrow3, kcmp, vcmp, ov, ex, tri, *, qb):
    bsz, L, _ = crow3.shape
    nq = L // qb
    ns = kcmp.shape[1]
    n_cmp = (L - C_CMP_LEN) // C_CMP_STRIDE + 1
    n_blk = -(-L // C_SLC_BLOCK)
    wsl = min(L, C_WINDOW + qb)
    row = lambda w: pl.BlockSpec((qb, w), lambda b, i: (b * nq + i, 0))
    per_b = lambda r, w: pl.BlockSpec((None, r, w), lambda b, i: (b, 0, 0))
    return pl.pallas_call(
        functools.partial(_nsa_kernel, qb=qb, L=L, n_cmp=n_cmp, n_top=min(C_TOPN, n_blk), wsl=wsl),
        grid=(bsz, nq),
        in_specs=[row(512), row(LANES), per_b(L, C_CACHE_DIM), per_b(L, C_WIN_DIM),
                  per_b(ns, HEAD_DIM), per_b(ns, HEAD_DIM), _full(ov, 2), _full(ex, 2), _full(tri, 2)],
        out_specs=row(512),
        out_shape=jax.ShapeDtypeStruct((bsz * L, 512), BF16),
        scratch_shapes=[pltpu.VMEM((L, 128), BF16)] * 4 + [pltpu.VMEM((qb, LANES), I32)],
        compiler_params=_cparams(2),
        name="nsa",
    )(cq, cg, crow3, wrow3, kcmp, vcmp, ov, ex, tri)


def _s5_kernel(u_ref, bdr_ref, bdi_ref, ar_ref, ai_ref, cdr_ref, cdi_ref, d_ref, gw_ref, gb_ref,
               o_ref, hr_ref, hi_ref, xr_sc, xi_sc, h_sc, *, tc, nb):
    i = pl.program_id(0)

    @pl.when(i == 0)
    def _():
        h_sc[...] = jnp.zeros_like(h_sc)

    u = u_ref[...]
    ub = u.astype(BF16)
    xr_sc[...] = _dot(ub, bdr_ref[...])
    xi_sc[...] = _dot(ub, bdi_ref[...])
    ar = jnp.broadcast_to(ar_ref[...], (nb, B_LANES))
    ai = jnp.broadcast_to(ai_ref[...], (nb, B_LANES))

    def step(t, carry):
        hr, hi = carry
        r0 = pl.multiple_of(t * nb, nb)
        nhr = ar * hr - ai * hi + xr_sc[pl.ds(r0, nb), :]
        nhi = ar * hi + ai * hr + xi_sc[pl.ds(r0, nb), :]
        xr_sc[pl.ds(r0, nb), :] = nhr
        xi_sc[pl.ds(r0, nb), :] = nhi
        return nhr, nhi

    hr, hi = lax.fori_loop(0, tc, step, (h_sc[0], h_sc[1]))
    h_sc[0] = hr
    h_sc[1] = hi
    hr_ref[...] = hr
    hi_ref[...] = hi
    y = _dot(xr_sc[...].astype(BF16), cdr_ref[...]) - _dot(xi_sc[...].astype(BF16), cdi_ref[...]) + d_ref[...] * u
    gl = _gelu(y)
    o_ref[...] = (gl * _sigmoid(_dot(gl.astype(BF16), gw_ref[...]) + gb_ref[...])).astype(BF16)


def _s5(u_tm, sp, *, nb, tc):
    rows = u_tm.shape[0]
    r = tc * nb
    consts = [sp["bdr"], sp["bdi"], sp["ar"], sp["ai"], sp["cdr"], sp["cdi"], sp["d"], sp["glu_w"], sp["glu_b"]]
    st = pl.BlockSpec((nb, B_LANES), lambda i: (0, 0))
    return pl.pallas_call(
        functools.partial(_s5_kernel, tc=tc, nb=nb),
        grid=(rows // r,),
        in_specs=[pl.BlockSpec((r, BR_WIDTH), lambda i: (i, 0))] + [_full(c, 1) for c in consts],
        out_specs=[pl.BlockSpec((r, BR_WIDTH), lambda i: (i, 0)), st, st],
        out_shape=[jax.ShapeDtypeStruct((rows, BR_WIDTH), BF16),
                   jax.ShapeDtypeStruct((nb, B_LANES), F32), jax.ShapeDtypeStruct((nb, B_LANES), F32)],
        scratch_shapes=[pltpu.VMEM((r, B_LANES), F32), pltpu.VMEM((r, B_LANES), F32), pltpu.VMEM((2, nb, B_LANES), F32)],
        compiler_params=_cparams(1),
        name="s5",
    )(u_tm, *consts)


HALO = 32


def _ln_swish(y, g, b):
    yc = y - jnp.mean(y, axis=-1, keepdims=True)
    yn = yc * lax.rsqrt(jnp.mean(yc * yc, axis=-1, keepdims=True) + NORM_EPS) * g + b
    return yn * _sigmoid(yn)


def _conv_kernel(cur_ref, halo_ref, w_ref, b_ref, lg_ref, lb_ref, o_ref, ext_sc, *, tm):
    i = pl.program_id(1)
    ext_sc[0:HALO, :] = jnp.where(i == 0, 0.0, halo_ref[...])
    ext_sc[HALO:HALO + tm, :] = cur_ref[...]
    acc = jnp.zeros((tm, BR_WIDTH), F32)
    for j in range(D_CONV):
        acc = acc + w_ref[j:j + 1, :] * ext_sc[pl.ds(HALO - (D_CONV - 1) + j, tm), :]
    o_ref[...] = _ln_swish(acc + b_ref[...], lg_ref[...], lb_ref[...]).astype(BF16)


def _conv(da3, cw, cb, lg, lb, *, tm):
    bsz, T, _ = da3.shape
    nt = T // tm
    hb = tm // HALO
    consts = [cw, cb, lg, lb]
    return pl.pallas_call(
        functools.partial(_conv_kernel, tm=tm),
        grid=(bsz, nt),
        in_specs=[pl.BlockSpec((None, tm, BR_WIDTH), lambda b, i: (b, i, 0)),
                  pl.BlockSpec((None, HALO, BR_WIDTH), lambda b, i: (b, jnp.maximum(i * hb - 1, 0), 0))]
        + [_full(c, 2) for c in consts],
        out_specs=pl.BlockSpec((tm, BR_WIDTH), lambda b, i: (b * nt + i, 0)),
        out_shape=jax.ShapeDtypeStruct((bsz * T, BR_WIDTH), BF16),
        scratch_shapes=[pltpu.VMEM((HALO + tm, BR_WIDTH), F32)],
        compiler_params=_cparams(2),
        name="conv",
    )(da3, da3, *consts)


MIX_CW = 512


def _mix_kernel(xn_ref, oa_ref, ob_ref, oc_ref, od_ref, wg_ref, wbr_ref, wo_ref, hm_ref, acc_sc):
    i = pl.program_id(1)
    xn = xn_ref[...]
    br = jnp.where(i == 0, oa_ref[...], jnp.where(i == 1, ob_ref[...], jnp.where(i == 2, oc_ref[...], od_ref[...])))
    for c in range(D_MODEL // MIX_CW):
        sl = slice(c * MIX_CW, (c + 1) * MIX_CW)
        contrib = _sigmoid(_dot(xn, wg_ref[:, sl])) * _dot(br, wbr_ref[:, sl])

        @pl.when(i == 0)
        def _():
            acc_sc[:, sl] = contrib

        @pl.when(i > 0)
        def _():
            acc_sc[:, sl] = acc_sc[:, sl] + contrib

    @pl.when(i == N_BRANCH - 1)
    def _():
        hm_ref[...] = _dot(acc_sc[...].astype(BF16), wo_ref[...])


def _mix(xn, oa, ob, ob_spec, oc, od, wg, wbr, wo, *, tm):
    n = xn.shape[0]
    row = lambda w: pl.BlockSpec((tm, w), lambda r, i: (r, 0))
    return pl.pallas_call(
        _mix_kernel,
        grid=(n // tm, N_BRANCH),
        in_specs=[row(D_MODEL), row(BR_WIDTH), ob_spec, row(BR_WIDTH), row(BR_WIDTH),
                  pl.BlockSpec((D_MODEL, D_MODEL), lambda r, i: (0, i)),
                  pl.BlockSpec((None, BR_WIDTH, D_MODEL), lambda r, i: (i, 0, 0)),
                  pl.BlockSpec((D_MODEL, D_MODEL), lambda r, i: (0, 0))],
        out_specs=row(D_MODEL),
        out_shape=jax.ShapeDtypeStruct((n, D_MODEL), F32),
        scratch_shapes=[pltpu.VMEM((tm, D_MODEL), F32)],
        compiler_params=_cparams(2),
        name="mix",
    )(xn, oa, ob, oc, od, wg, wbr, wo)


MLP_FC = 1024


def _mlp_kernel(x_ref, hm_ref, g_ref, wu_ref, wd_ref, y_ref, hn_sc, acc_sc):
    j = pl.program_id(1)

    @pl.when(j == 0)
    def _():
        hn_sc[...] = _rms(x_ref[...] + hm_ref[...], g_ref[...]).astype(BF16)

    up = _dot(hn_sc[...], wu_ref[...])
    act = jnp.square(jnp.maximum(up, 0.0)).astype(BF16)
    d = _dot(act, wd_ref[...])

    @pl.when(j == 0)
    def _():
        acc_sc[...] = d

    @pl.when(j > 0)
    def _():
        acc_sc[...] = acc_sc[...] + d

    @pl.when(j == pl.num_programs(1) - 1)
    def _():
        y_ref[...] = (x_ref[...] + hm_ref[...]) + acc_sc[...]


def _mlp(x2d, hm, g, wu, wd, *, tm):
    n = x2d.shape[0]
    row = pl.BlockSpec((tm, D_MODEL), lambda r, j: (r, 0))
    return pl.pallas_call(
        _mlp_kernel,
        grid=(n // tm, D_FF // MLP_FC),
        in_specs=[row, row, _full(g, 2),
                  pl.BlockSpec((D_MODEL, MLP_FC), lambda r, j: (0, j)),
                  pl.BlockSpec((MLP_FC, D_MODEL), lambda r, j: (j, 0))],
        out_specs=row,
        out_shape=jax.ShapeDtypeStruct((n, D_MODEL), F32),
        scratch_shapes=[pltpu.VMEM((tm, D_MODEL), BF16), pltpu.VMEM((tm, D_MODEL), F32)],
        compiler_params=_cparams(2),
        name="mlp",
    )(x2d, hm, g, wu, wd)


PAGES_PER_STEP = 8


def _row0_tile(row):
    r = lax.broadcasted_iota(I32, (LANES, LANES), 0)
    return jnp.where(r == 0, jnp.broadcast_to(row, (LANES, LANES)), 0.0).astype(BF16)


def _tile_attention(s_ref, bias_ref, v_ref, nr):
    s = s_ref[...] * ATT_SCALE + bias_ref[...]
    m = jnp.max(jnp.max(s, axis=0, keepdims=True), axis=2, keepdims=True)
    e = jnp.exp(s - m)
    den = jnp.sum(jnp.sum(e, axis=0, keepdims=True), axis=2, keepdims=True)
    acc = jnp.zeros((SUBLANES, HEAD_DIM), F32)
    for c in range(nr):
        acc = acc + _dot(e[c].astype(BF16), v_ref[c * LANES:(c + 1) * LANES, :])
    return acc / jnp.maximum(den[0], 1e-30)


def _dsa_s_kernel(pt_ref, q_ref, iq_ref, w_ref, new_ref, *rest, pg, n_pages, n_sel):
    pages = rest[:pg]
    tri_ref, o_ref, s_sc, sc_sc, v_sc, ik_sc, key_sc, base_sc = rest[pg:]
    g = pl.program_id(1)
    q = q_ref[...].astype(BF16)
    iq = iq_ref[...].astype(BF16)
    w = w_ref[...]

    @pl.when(g == 0)
    def _():
        ik_sc[...] = jnp.zeros_like(ik_sc)

    for i in range(pg):
        c = g * pg + i
        page = pages[i]
        s_sc[c] = _dot_nt(q, page[:, 0:128].astype(BF16))
        v_sc[pl.ds(pl.multiple_of(c * LANES, LANES), LANES), :] = page[:, 128:256].astype(BF16)
        ik_sc[:, 0:A_IDX_DIM] = page[:, 256:320].astype(BF16)
        lg = _dot_nt(iq, ik_sc[...])
        sc = jnp.sum(jnp.maximum(lg, 0.0) * w, axis=0, keepdims=True)
        sc_sc[c] = jnp.broadcast_to(sc, (SUBLANES, LANES))

    @pl.when(g == pl.num_programs(1) - 1)
    def _():
        new = new_ref[...]
        lane = lax.broadcasted_iota(I32, (SUBLANES, LANES), 1)
        k_new = new[0:1, 0:128].astype(BF16).astype(F32)
        s_new = jnp.sum(q.astype(F32) * k_new, axis=-1, keepdims=True)
        ik_new = new[0:1, 256:320].astype(BF16).astype(F32)
        lg_new = jnp.sum(iq[:, 0:A_IDX_DIM].astype(F32) * ik_new, axis=-1, keepdims=True)
        sc_new = jnp.sum(jnp.maximum(lg_new, 0.0) * w[:, 0:1], axis=0, keepdims=True)
        s_sc[n_pages] = jnp.where(lane == 0, s_new, 0.0)
        sc_sc[n_pages] = jnp.where(lane == 0, sc_new, -jnp.inf)
        v_sc[n_pages * LANES:(n_pages + 1) * LANES, :] = _row0_tile(new[0:1, 128:256])
        key_sc[...] = _sortable(sc_sc[...])
        sel = _select_tiles(key_sc, base_sc, n_sel, tri_ref[...])
        tile = lax.broadcasted_iota(I32, sel.shape, 0)
        lane3 = lax.broadcasted_iota(I32, sel.shape, 2)
        valid = (tile < n_pages) | (lane3 == 0)
        base_sc[...] = jnp.where(valid, jnp.where(sel > 0.5, 0.0, NEG), NEG)
        o_ref[...] = _tile_attention(s_sc, base_sc, v_sc, n_pages + 1)


def _dsa_sample(page_table, cache, layer, q8, iq8, w8, new8, tri):
    bsz, n_pages = page_table.shape
    pg = PAGES_PER_STEP
    n_sel = min(A_TOPK, (n_pages * PAGE + 1) // 4)
    nr = n_pages + 1
    per_b = lambda r, w: pl.BlockSpec((None, r, w), lambda b, g, pt: (b, 0, 0))
    page_spec = lambda i: pl.BlockSpec((None, None, PAGE, A_CACHE_DIM),
                                       lambda b, g, pt: (layer, pt[b, g * pg + i], 0, 0))
    grid_spec = pltpu.PrefetchScalarGridSpec(
        num_scalar_prefetch=1,
        grid=(bsz, n_pages // pg),
        in_specs=[per_b(SUBLANES, LANES), per_b(SUBLANES, LANES), per_b(SUBLANES, LANES), per_b(SUBLANES, A_CACHE_DIM)]
        + [page_spec(i) for i in range(pg)] + [pl.BlockSpec(tri.shape, lambda b, g, pt: (0, 0))],
        out_specs=per_b(SUBLANES, HEAD_DIM),
        scratch_shapes=[pltpu.VMEM((nr, SUBLANES, LANES), F32), pltpu.VMEM((nr, SUBLANES, LANES), F32),
                        pltpu.VMEM((nr * LANES, HEAD_DIM), BF16), pltpu.VMEM((PAGE, LANES), BF16),
                        pltpu.VMEM((nr, SUBLANES, LANES), I32), pltpu.VMEM((nr, SUBLANES, LANES), F32)],
    )
    return pl.pallas_call(
        functools.partial(_dsa_s_kernel, pg=pg, n_pages=n_pages, n_sel=n_sel),
        grid_spec=grid_spec,
        out_shape=jax.ShapeDtypeStruct((bsz, SUBLANES, HEAD_DIM), F32),
        compiler_params=_cparams(2),
        name="dsa_sample",
    )(page_table, q8, iq8, w8, new8, *([cache] * pg), tri)


def _nsa_s_kernel(pt_ref, q_ref, g3_ref, new_ref, win_ref, wnew_ref, a_ref, w_ref, ov_ref, *rest,
                  pg, n_pages, n_cmp, n_top, past):
    pages = rest[:pg]
    tri_ref, o_ref, ss_sc, bias_sc, vs_sc, lok_sc, hik_sc, lov_sc, hiv_sc, key_sc = rest[pg:]
    g = pl.program_id(1)
    q = q_ref[...].astype(BF16)
    sub = PAGE // C_CMP_STRIDE

    for i in range(pg):
        c = g * pg + i
        page = pages[i]
        r0 = pl.multiple_of(c * sub, sub)
        lo, hi = _summaries(page[:, 0:128], a_ref[0], None, n_cmp)
        lok_sc[pl.ds(r0, sub), :] = lo
        hik_sc[pl.ds(r0, sub), :] = hi
        lo, hi = _summaries(page[:, 128:256], a_ref[1], None, n_cmp)
        lov_sc[pl.ds(r0, sub), :] = lo
        hiv_sc[pl.ds(r0, sub), :] = hi
        ss_sc[c] = _dot_nt(q, page[:, 256:384].astype(BF16))
        vs_sc[pl.ds(pl.multiple_of(c * LANES, LANES), LANES), :] = page[:, 384:512].astype(BF16)

    @pl.when(g == pl.num_programs(1) - 1)
    def _():
        qf = q.astype(F32)
        new = new_ref[...]
        lane = lax.broadcasted_iota(I32, (SUBLANES, LANES), 1)
        row = lax.broadcasted_iota(I32, (SUBLANES, LANES), 0)
        qpos = jnp.full((SUBLANES, 1), past, I32)
        kc = _finish_summaries(lok_sc[...], hik_sc[...], w_ref[0], n_cmp)
        vc = _finish_summaries(lov_sc[...], hiv_sc[...], w_ref[1], n_cmp)
        ncp = kc.shape[0]
        n_io = lax.broadcasted_iota(I32, (SUBLANES, ncp), 1)
        cvalid = (n_io < n_cmp) & (n_io * C_CMP_STRIDE + (C_CMP_LEN - 1) <= qpos)
        p = _cmp_softmax(q, kc, cvalid)
        o_c = _dot(p.astype(BF16), vc)
        head = lax.broadcasted_iota(I32, p.shape, 0) < C_HEADS
        psum = jnp.broadcast_to(jnp.sum(jnp.where(head, p, 0.0), axis=0, keepdims=True), p.shape)
        imp = _dot_split3(psum, ov_ref[...])
        key_sc[...] = _sortable(_block_scores(imp, qpos))
        sel = _select_rows(key_sc, n_top, tri_ref[...])
        for c in range(n_pages):
            pick = jnp.where(lane < C_SLC_BLOCK, sel[:, 2 * c:2 * c + 1], sel[:, 2 * c + 1:2 * c + 2])
            bias_sc[c] = jnp.where(pick > 0.5, 0.0, NEG)
        k_new = new[0:1, 256:384].astype(BF16).astype(F32)
        ss_sc[n_pages] = jnp.where(lane == 0, jnp.sum(qf * k_new, axis=-1, keepdims=True), 0.0)
        pick_new = sel[:, 2 * n_pages:2 * n_pages + 1]
        bias_sc[n_pages] = jnp.where(lane == 0, jnp.where(pick_new > 0.5, 0.0, NEG), NEG)
        vs_sc[n_pages * LANES:(n_pages + 1) * LANES, :] = _row0_tile(new[0:1, 384:512])
        o_s = _tile_attention(ss_sc, bias_sc, vs_sc, n_pages + 1)
        wb = win_ref.shape[0]
        kw = win_ref[:, 0:128].astype(BF16)
        vw = win_ref[:, 128:256].astype(BF16)
        wnew = wnew_ref[...]
        s_w = _dot_nt(q, kw) * ATT_SCALE
        dist = wb - lax.broadcasted_iota(I32, (SUBLANES, wb), 1)
        wvalid = (dist <= C_WINDOW) & (past - dist >= 0)
        s_w = jnp.where(wvalid, s_w, NEG)
        s_n = jnp.sum(qf * wnew[0:1, 0:128].astype(BF16).astype(F32), axis=-1, keepdims=True) * ATT_SCALE
        m = jnp.maximum(jnp.max(s_w, axis=-1, keepdims=True), s_n)
        e_w = jnp.where(wvalid, jnp.exp(s_w - m), 0.0)
        e_n = jnp.exp(s_n - m)
        den = jnp.sum(e_w, axis=-1, keepdims=True) + e_n
        v_n = wnew[0:1, 128:256].astype(BF16).astype(F32)
        o_w = (_dot(e_w.astype(BF16), vw) + e_n.astype(BF16).astype(F32) * v_n) / den
        o_ref[...] = g3_ref[0] * o_c + g3_ref[1] * o_s + g3_ref[2] * o_w


def _nsa_sample(page_table, cache, layer, q8, g3, new8, win, wnew8, cmp_a, cmp_w, ov, tri):
    bsz, n_pages = page_table.shape
    pg = PAGES_PER_STEP
    past = n_pages * PAGE
    n_cmp = (past + 1 - C_CMP_LEN) // C_CMP_STRIDE + 1
    n_blk = -(-(past + 1) // C_SLC_BLOCK)
    nr = n_pages + 1
    ns = past // C_CMP_STRIDE
    wb = win.shape[2]
    cst = lambda a: pl.BlockSpec(a.shape, lambda b, g, pt: (0,) * a.ndim)
    per_b = lambda r, w: pl.BlockSpec((None, r, w), lambda b, g, pt: (b, 0, 0))
    page_spec = lambda i: pl.BlockSpec((None, None, PAGE, C_CACHE_DIM),
                                       lambda b, g, pt: (layer, pt[b, g * pg + i], 0, 0))
    grid_spec = pltpu.PrefetchScalarGridSpec(
        num_scalar_prefetch=1,
        grid=(bsz, n_pages // pg),
        in_specs=[per_b(SUBLANES, LANES),
                  pl.BlockSpec((None, 3, SUBLANES, LANES), lambda b, g, pt: (b, 0, 0, 0)),
                  per_b(SUBLANES, C_CACHE_DIM),
                  pl.BlockSpec((None, None, wb, C_WIN_DIM), lambda b, g, pt: (layer, b, 0, 0)),
                  per_b(SUBLANES, C_WIN_DIM), cst(cmp_a), cst(cmp_w), cst(ov)]
        + [page_spec(i) for i in range(pg)] + [cst(tri)],
        out_specs=per_b(SUBLANES, HEAD_DIM),
        scratch_shapes=[pltpu.VMEM((nr, SUBLANES, LANES), F32), pltpu.VMEM((nr, SUBLANES, LANES), F32),
                        pltpu.VMEM((nr * LANES, HEAD_DIM), BF16)]
        + [pltpu.VMEM((ns, HEAD_DIM), F32)] * 4 + [pltpu.VMEM((SUBLANES, ov.shape[1]), I32)],
    )
    return pl.pallas_call(
        functools.partial(_nsa_s_kernel, pg=pg, n_pages=n_pages, n_cmp=n_cmp, n_top=min(C_TOPN, n_blk), past=past),
        grid_spec=grid_spec,
        out_shape=jax.ShapeDtypeStruct((bsz, SUBLANES, HEAD_DIM), F32),
        compiler_params=_cparams(2),
        name="nsa_sample",
    )(page_table, q8, g3, new8, win, wnew8, cmp_a, cmp_w, ov, *([cache] * pg), tri)


def _step_kernel(u_ref, h0r_ref, h0i_ref, bdr_h_ref, bdr_l_ref, bdi_h_ref, bdi_l_ref, ar_ref, ai_ref,
                 cdr_ref, cdi_ref, d_ref, gw_ref, gb_ref, da_ref, cst_ref, cw_ref, cb_ref, lg_ref, lb_ref,
                 ob_ref, hr_ref, hi_ref, od_ref):
    u = u_ref[...]
    uh = u.astype(BF16)
    ul = (u - uh.astype(F32)).astype(BF16)

    def bmat(h_ref, l_ref):
        return _dot(uh, h_ref[...]) + (_dot(uh, l_ref[...]) + _dot(ul, h_ref[...]))

    ar, ai = ar_ref[...], ai_ref[...]
    h0r, h0i = h0r_ref[...], h0i_ref[...]
    hr = bmat(bdr_h_ref, bdr_l_ref) + (ar * h0r - ai * h0i)
    hi = bmat(bdi_h_ref, bdi_l_ref) + (ar * h0i + ai * h0r)
    hr_ref[...] = hr
    hi_ref[...] = hi
    y = _dot(hr.astype(BF16), cdr_ref[...]) - _dot(hi.astype(BF16), cdi_ref[...]) + d_ref[...] * u
    gl = _gelu(y)
    ob_ref[...] = (gl * _sigmoid(_dot(gl.astype(BF16), gw_ref[...]) + gb_ref[...])).astype(BF16)
    cw = cw_ref[...]
    y = jnp.sum(cst_ref[...] * cw[0:D_CONV - 1][None], axis=1) + cw[D_CONV - 1:D_CONV] * da_ref[...] + cb_ref[...]
    od_ref[...] = _ln_swish(y, lg_ref[...], lb_ref[...]).astype(BF16)


def _sample_step(u, h0r, h0i, sp, da, conv_state, cw, cb, lg, lb):
    bsz = u.shape[0]
    ins = [u, h0r, h0i, sp["bdr"], sp["bdr_lo"], sp["bdi"], sp["bdi_lo"], sp["ar"], sp["ai"], sp["cdr"], sp["cdi"],
           sp["d"], sp["glu_w"], sp["glu_b"], da, conv_state, cw, cb, lg, lb]
    outs = [((bsz, BR_WIDTH), BF16), ((bsz, B_LANES), F32), ((bsz, B_LANES), F32), ((bsz, BR_WIDTH), BF16)]
    return pl.pallas_call(
        _step_kernel,
        grid=(1,),
        in_specs=[_full(a, 1) for a in ins],
        out_specs=[pl.BlockSpec(o[0], lambda i: (0, 0)) for o in outs],
        out_shape=[jax.ShapeDtypeStruct(o[0], o[1]) for o in outs],
        compiler_params=_cparams(1),
        name="sample_step",
    )(*ins)


def _rope_tables(pos):
    pos = pos.astype(F32)[:, None]

    def tab(dim):
        half = dim // 2
        inv = ROPE_THETA ** (-jnp.arange(half, dtype=F32) / half)
        ang = pos * inv
        cos, sin = jnp.cos(ang), jnp.sin(ang)
        reps = LANES // dim
        return jnp.tile(jnp.concatenate([cos, cos], axis=1), (1, reps)), jnp.tile(jnp.concatenate([-sin, sin], axis=1), (1, reps))

    c128, s128 = tab(HEAD_DIM)
    c64, s64 = tab(A_IDX_DIM)
    return c128, s128, c64, s64


def _pack_w1(w):
    def padded(a, b, width):
        return jnp.pad(w[:, a:b], ((0, 0), (0, width - (b - a))))
    parts = [w[:, _O[0]:_O[4]], padded(_O[4], _O[5], 128), padded(_O[5], _O[6], 128), w[:, _O[6]:_O[9]],
             padded(_O[9], _O[10], 128), w[:, _O[10]:_O[11]]]
    return jnp.concatenate(parts, axis=1).astype(BF16)


def _s5_params(lam_re, lam_im, log_dt, b_re, b_im, c_re, c_im, d, glu_w, glu_b):
    lr, li = lam_re.astype(F32), lam_im.astype(F32)
    dt = jnp.exp(log_dt.astype(F32))[:, None]
    mag = jnp.exp(lr * dt)
    ar, ai = mag * jnp.cos(li * dt), mag * jnp.sin(li * dt)
    den = lr * lr + li * li
    fr = ((ar - 1.0) * lr + ai * li) / den
    fi = (ai * lr - (ar - 1.0) * li) / den
    br, bi = b_re.astype(F32), b_im.astype(F32)
    bbr = fr[..., None] * br - fi[..., None] * bi
    bbi = fr[..., None] * bi + fi[..., None] * br
    eye = jnp.eye(B_GROUPS, dtype=F32)
    bd = lambda m: jnp.einsum("gpc,gh->gchp", m, eye).reshape(BR_WIDTH, B_LANES)
    cd = lambda m: jnp.einsum("gcp,gh->gphc", m.astype(F32), eye).reshape(B_LANES, BR_WIDTH)
    bdr, bdi = bd(bbr), bd(bbi)
    hi_lo = lambda m: (m.astype(BF16), (m - m.astype(BF16).astype(F32)).astype(BF16))
    bdr_h, bdr_l = hi_lo(bdr)
    bdi_h, bdi_l = hi_lo(bdi)
    return dict(bdr=bdr_h, bdr_lo=bdr_l, bdi=bdi_h, bdi_lo=bdi_l,
                ar=ar.reshape(1, B_LANES), ai=ai.reshape(1, B_LANES),
                cdr=cd(c_re).astype(BF16), cdi=cd(c_im).astype(BF16), d=d.astype(F32).reshape(1, BR_WIDTH),
                glu_w=glu_w.astype(BF16), glu_b=glu_b.astype(F32).reshape(1, BR_WIDTH))


def _overlap(n_cmp, n_blk, rows, cols):
    start = np.arange(n_cmp)[:, None] * C_CMP_STRIDE
    blk = np.arange(n_blk)[None, :]
    m = (start <= (blk + 1) * C_SLC_BLOCK - 1) & (start + C_CMP_LEN - 1 >= blk * C_SLC_BLOCK)
    out = np.zeros((rows, cols), np.float32)
    out[:n_cmp, :n_blk] = m
    return jnp.asarray(out, BF16)


def _expand(n_keys):
    e = (np.arange(LANES)[:, None] == (np.arange(n_keys)[None, :] // C_SLC_BLOCK)).astype(np.float32)
    return jnp.asarray(e, BF16)


def _tri():
    i = np.arange(LANES)
    return jnp.asarray((i[:, None] < i[None, :]).astype(np.float32), BF16)


def _make_consts(T, sb, past):
    n_cmp_p = (T - C_CMP_LEN) // C_CMP_STRIDE + 1
    n_blk_p = -(-T // C_SLC_BLOCK)
    n_cmp_s = (past + 1 - C_CMP_LEN) // C_CMP_STRIDE + 1
    n_blk_s = -(-(past + 1) // C_SLC_BLOCK)
    return dict(
        tabs_p=_rope_tables(jnp.arange(T)),
        tabs_s=_rope_tables(jnp.full((sb,), past)),
        tri=_tri(),
        ov_p=_overlap(n_cmp_p, n_blk_p, T // C_CMP_STRIDE, LANES),
        ex_p=_expand(T),
        ov_s=_overlap(n_cmp_s, n_blk_s, past // C_CMP_STRIDE, -(-n_blk_s // LANES) * LANES),
    )


def _pick_tile(n, cands):
    for c in cands:
        if n % c == 0:
            return c
    return n


def _pad_rows(a, rows):
    return jnp.pad(a[:, None, :], ((0, 0), (0, rows - 1), (0, 0)))


def _layer_weights(l, norm_mix, w_in, a_gq, a_gk, c_gq, c_gk, c_cmp_a, c_cmp_w, d_conv_w, d_conv_b, d_ln_g, d_ln_b,
                   w_br, w_o, norm_mlp, w_up, w_down):
    row = lambda v: v.astype(F32).reshape(1, -1)
    return dict(
        g_mix=row(norm_mix[l]), w1=_pack_w1(w_in[l]), wg=w_in[l][:, _O[11]:_O[12]].astype(BF16),
        a_gq=row(a_gq[l]), a_gk=row(a_gk[l]), c_gq=row(c_gq[l]), c_gk=c_gk[l].astype(F32),
        cmp_a=c_cmp_a[l].astype(F32), cmp_w=c_cmp_w[l].astype(BF16),
        conv_w=jnp.pad(d_conv_w[l].astype(F32), ((0, 1), (0, 0))), conv_b=row(d_conv_b[l]),
        ln_g=row(d_ln_g[l]), ln_b=row(d_ln_b[l]),
        w_br=w_br[l].astype(BF16), w_o=w_o[l].astype(BF16), g_mlp=row(norm_mlp[l]),
        w_up=w_up[l].astype(BF16), w_down=w_down[l].astype(BF16))


def _prompt_layer(x, lw, sp, consts):
    bsz, T, _ = x.shape
    n = bsz * T
    x2d = x.reshape(n, D_MODEL)
    tm = _pick_tile(T, (256, 128))
    nt = T // tm
    qb = 128
    bu_spec = pl.BlockSpec((tm, BR_WIDTH), lambda i: (i % nt, i // nt))
    (xn, aq, arow, aiq, aiw, bu, cq, crow, wrow, cg, da) = _project(
        x2d, consts["tabs_p"], lw, tm=tm, n_pos_blocks=nt, bu_shape=(T, bsz * BR_WIDTH), bu_spec=bu_spec)
    arow3 = arow.reshape(bsz, T, A_CACHE_DIM)
    crow3 = crow.reshape(bsz, T, C_CACHE_DIM)
    wrow3 = wrow.reshape(bsz, T, C_WIN_DIM)
    da3 = da.reshape(bsz, T, BR_WIDTH)
    o_a = _dsa(aq, aiq, aiw, arow3, consts["tri"], qb=qb)
    kcmp, vcmp = _compress(crow3, lw["cmp_a"], lw["cmp_w"])
    o_c = _nsa(cq, cg, crow3, wrow3, kcmp, vcmp, consts["ov_p"], consts["ex_p"], consts["tri"], qb=qb)
    tc = _pick_tile(T, (128, 64))
    o_b, hr, hi = _s5(bu.reshape(T * bsz, BR_WIDTH), sp, nb=bsz, tc=tc)
    o_d = _conv(da3, lw["conv_w"], lw["conv_b"], lw["ln_g"], lw["ln_b"], tm=tm)
    tmx = _pick_tile(T, (512, 256, 128))
    ntx = T // tmx
    ob_spec = pl.BlockSpec((tmx, BR_WIDTH), lambda r, i: (r % ntx, r // ntx))
    hm = _mix(xn, o_a, o_b.reshape(T, bsz * BR_WIDTH), ob_spec, o_c, o_d, lw["wg"], lw["w_br"], lw["w_o"], tm=tmx)
    y = _mlp(x2d, hm, lw["g_mlp"], lw["w_up"], lw["w_down"], tm=tmx)
    wk = min(C_WINDOW, T)
    return (y.reshape(bsz, T, D_MODEL), arow3, crow3, wrow3[:, T - wk:],
            hr.reshape(bsz, B_GROUPS, B_STATE), hi.reshape(bsz, B_GROUPS, B_STATE), da3[:, T - (D_CONV - 1):])


def _sample_layer(x, l, cache_a, cache_c, cache_c_win, h_re, h_im, conv_l, page_table, lw, sp, consts):
    bsz = x.shape[0]
    x2d = x.reshape(bsz, D_MODEL)
    row = lambda w: pl.BlockSpec((bsz, w), lambda i: (0, 0))
    (xn, aq, arow, aiq, aiw, bu, cq, crow, wrow, cg, da) = _project(
        x2d, consts["tabs_s"], lw, tm=bsz, n_pos_blocks=1, bu_shape=(bsz, BR_WIDTH), bu_spec=row(BR_WIDTH))
    q8 = jnp.pad(aq.astype(F32).reshape(bsz, A_HEADS, HEAD_DIM), ((0, 0), (0, SUBLANES - A_HEADS), (0, 0)))
    iq8 = jnp.sum(aiq.astype(F32).reshape(bsz, A_IDX_HEADS, 2, A_IDX_DIM), axis=2)
    iq8 = jnp.pad(iq8, ((0, 0), (0, 0), (0, LANES - A_IDX_DIM)))
    w8 = jnp.broadcast_to(aiw[:, :A_IDX_HEADS, None], (bsz, A_IDX_HEADS, LANES))
    o_a = _dsa_sample(page_table, cache_a, l, q8, iq8, w8, _pad_rows(arow, SUBLANES), consts["tri"])
    o_a = o_a[:, :A_HEADS].reshape(bsz, BR_WIDTH).astype(BF16)
    cq8 = jnp.pad(cq.astype(F32).reshape(bsz, C_HEADS, HEAD_DIM), ((0, 0), (0, SUBLANES - C_HEADS), (0, 0)))
    g3 = jnp.transpose(cg[:, :3 * C_HEADS].reshape(bsz, C_HEADS, 3), (0, 2, 1))
    g3 = jnp.broadcast_to(jnp.pad(g3, ((0, 0), (0, 0), (0, SUBLANES - C_HEADS)))[..., None], (bsz, 3, SUBLANES, LANES))
    o_c = _nsa_sample(page_table, cache_c, l, cq8, g3, _pad_rows(crow, SUBLANES), cache_c_win,
                      _pad_rows(wrow, SUBLANES), lw["cmp_a"], lw["cmp_w"], consts["ov_s"], consts["tri"])
    o_c = o_c[:, :C_HEADS].reshape(bsz, BR_WIDTH).astype(BF16)
    o_b, hr, hi, o_d = _sample_step(bu, h_re.reshape(bsz, B_LANES), h_im.reshape(bsz, B_LANES), sp, da, conv_l,
                                    lw["conv_w"], lw["conv_b"], lw["ln_g"], lw["ln_b"])
    hm = _mix(xn, o_a, o_b, pl.BlockSpec((bsz, BR_WIDTH), lambda r, i: (r, 0)), o_c, o_d,
              lw["wg"], lw["w_br"], lw["w_o"], tm=bsz)
    y = _mlp(x2d, hm, lw["g_mlp"], lw["w_up"], lw["w_down"], tm=bsz)
    new_win = jnp.concatenate([cache_c_win[l][:, 1:], wrow[:, None, :]], axis=1)
    new_conv = jnp.concatenate([conv_l[:, 1:], da[:, None, :]], axis=1)
    return (y.reshape(bsz, 1, D_MODEL), arow[:, None, :], crow[:, None, :], new_win,
            hr.reshape(bsz, B_GROUPS, B_STATE), hi.reshape(bsz, B_GROUPS, B_STATE), new_conv)


def kernel(x_prompt, x_sample, cache_a, cache_c, cache_c_win, state_b_re, state_b_im, state_d_conv, page_table, norm_mix, w_in, a_gq, a_gk, b_lam_re, b_lam_im, b_log_dt, b_b_re, b_b_im, b_c_re, b_c_im, b_d, b_glu_w, b_glu_b, c_gq, c_gk, c_cmp_a, c_cmp_w, d_conv_w, d_conv_b, d_ln_g, d_ln_b, w_br, w_o, norm_mlp, w_up, w_down):
    depth = w_in.shape[0]
    bsz, T, _ = x_prompt.shape
    sb, st, _ = x_sample.shape
    assert st == 1 and bsz == SUBLANES
    assert cache_a.shape[2] == PAGE and cache_c.shape[2] == PAGE
    consts = _make_consts(T, sb, page_table.shape[1] * PAGE)
    xp, xs = x_prompt, x_sample
    order_p = (0, 2, 4, 6, 7, 10)
    order_s = (1, 3, 5, 8, 9, 11)
    outs = [[] for _ in range(12)]
    for l in range(depth):
        lw = _layer_weights(l, norm_mix, w_in, a_gq, a_gk, c_gq, c_gk, c_cmp_a, c_cmp_w, d_conv_w, d_conv_b,
                            d_ln_g, d_ln_b, w_br, w_o, norm_mlp, w_up, w_down)
        sp = _s5_params(b_lam_re[l], b_lam_im[l], b_log_dt[l], b_b_re[l], b_b_im[l], b_c_re[l], b_c_im[l],
                        b_d[l], b_glu_w[l], b_glu_b[l])
        xp, *rp = _prompt_layer(xp, lw, sp, consts)
        xs, *rs = _sample_layer(xs, l, cache_a, cache_c, cache_c_win, state_b_re[l], state_b_im[l], state_d_conv[l],
                                page_table, lw, sp, consts)
        for k in range(6):
            outs[order_p[k]].append(rp[k])
            outs[order_s[k]].append(rs[k])
    return (xp, xs) + tuple(jnp.stack(o) for o in outs)
```

```python
import functools
import math

import numpy as np
import jax
import jax.numpy as jnp
from jax import lax
from jax.experimental import pallas as pl
from jax.experimental.pallas import tpu as pltpu

F32 = jnp.float32
BF16 = jnp.bfloat16
I32 = jnp.int32

D_MODEL = 2048
HEAD_DIM = 128
N_BRANCH = 4
BR_WIDTH = D_MODEL // N_BRANCH
ROPE_THETA = 10000.0
NORM_EPS = 1e-6
A_HEADS = BR_WIDTH // HEAD_DIM
A_IDX_HEADS = 8
A_IDX_DIM = 64
A_TOPK = 256
B_GROUP = 16
B_GROUPS = BR_WIDTH // B_GROUP
B_STATE = 64
B_LANES = B_GROUPS * B_STATE
C_HEADS = BR_WIDTH // HEAD_DIM
C_CMP_STRIDE = 16
C_CMP_LEN = 2 * C_CMP_STRIDE
C_SLC_BLOCK = 64
C_TOPN = 16
C_WINDOW = 512
C_FORCE = 1e4
D_CONV = 31
D_FF = 4 * D_MODEL
A_CACHE_DIM = 2 * HEAD_DIM + A_IDX_DIM
C_CACHE_DIM = 4 * HEAD_DIM
C_WIN_DIM = 2 * HEAD_DIM
PAGE = 128

LANES = 128
SUBLANES = 8
VMEM_LIMIT_MB = 56

_W = (A_HEADS * HEAD_DIM, HEAD_DIM, HEAD_DIM, A_IDX_HEADS * A_IDX_DIM, A_IDX_DIM, A_IDX_HEADS,
      BR_WIDTH, C_HEADS * HEAD_DIM, 6 * HEAD_DIM, 3 * C_HEADS, 2 * BR_WIDTH, N_BRANCH * D_MODEL)
_O = tuple(int(v) for v in np.cumsum((0,) + _W))
_P = {}
_cur = 0
for _name, _w in (("aq", 512), ("ak", 128), ("av", 128), ("aiq", 512), ("aik", 128), ("aiw", 128),
                  ("bu", 512), ("cq", 512), ("ckv", 768), ("cg", 128), ("dglu", 1024)):
    _P[_name] = (_cur, _w)
    _cur += _w
P_TOTAL = _cur

NEG = -1e30
ATT_SCALE = HEAD_DIM ** -0.5
SIGN = -2 ** 31


def _cparams(n_axes):
    return pltpu.CompilerParams(dimension_semantics=("arbitrary",) * n_axes,
                                vmem_limit_bytes=VMEM_LIMIT_MB * 1024 * 1024)


def _full(a, n_grid):
    nd = a.ndim
    return pl.BlockSpec(a.shape, lambda *_: (0,) * nd)


def _dot(a, b):
    return jnp.dot(a, b, preferred_element_type=F32)


def _dot_nt(a, b):
    return lax.dot_general(a, b, (((1,), (1,)), ((), ())), preferred_element_type=F32)


def _dot_split3(p, m):
    hi = p.astype(BF16)
    r = p - hi.astype(F32)
    mid = r.astype(BF16)
    lo = (r - mid.astype(F32)).astype(BF16)
    return _dot(hi, m) + _dot(mid, m) + _dot(lo, m)


def _rms(x, g):
    return x * lax.rsqrt(jnp.mean(x * x, axis=-1, keepdims=True) + NORM_EPS) * g


def _sigmoid(x):
    return 1.0 / (1.0 + jnp.exp(-x))


def _gelu(x):
    return x * (0.5 * (1.0 + jnp.tanh(math.sqrt(2.0 / math.pi) * (x + 0.044715 * (x * x * x)))))


def _sortable(x):
    b = pltpu.bitcast(x + 0.0, I32)
    return jnp.where(b < 0, b ^ jnp.int32(0x7FFFFFFF), b)


def _kth_key(key_ref, k, red_axes):
    shp = tuple(1 if a in red_axes else s for a, s in enumerate(key_ref.shape))

    def count(mask):
        c = jnp.where(mask, 1.0, 0.0)
        for a in sorted(red_axes):
            c = jnp.sum(c, axis=a, keepdims=True)
        return c

    def body(it, tu):
        cand_u = tu | jnp.left_shift(jnp.int32(1), 31 - it)
        cand_s = cand_u ^ jnp.int32(SIGN)
        return jnp.where(count(key_ref[...] >= cand_s) >= k, cand_u, tu)

    tu = lax.fori_loop(0, 32, body, jnp.zeros(shp, I32))
    return tu ^ jnp.int32(SIGN), count


def _select_rows(key_ref, k, tri):
    ts, count = _kth_key(key_ref, k, (1,))
    keys = key_ref[...]
    gt = keys > ts
    need = k - count(gt)
    eqf = jnp.where(keys == ts, 1.0, 0.0)
    base = jnp.zeros_like(need)
    pieces = []
    for c in range(keys.shape[1] // LANES):
        ch = eqf[:, c * LANES:(c + 1) * LANES]
        pref = _dot(ch.astype(BF16), tri) + base
        pieces.append(jnp.where(pref < need, ch, 0.0))
        base = base + jnp.sum(ch, axis=-1, keepdims=True)
    sel_eq = pieces[0] if len(pieces) == 1 else jnp.concatenate(pieces, axis=1)
    return jnp.where(gt, 1.0, sel_eq)


def _select_packed(key_ref, k, tri, tril):
    shp = key_ref.shape
    ts, count = _kth_key(key_ref, k, (0, 1, 2))
    keys = key_ref[...]
    gt = keys > ts
    need = k - count(gt)
    eqf = jnp.where(keys == ts, 1.0, 0.0)
    eq2 = eqf.reshape(LANES, LANES)
    within = _dot(eq2.astype(BF16), tri)
    tot = jnp.broadcast_to(jnp.sum(eq2, axis=1, keepdims=True), (LANES, LANES))
    base = _dot(tril, tot.astype(BF16))
    sel_eq = jnp.where((within + base).reshape(shp) < need, eqf, 0.0)
    return jnp.where(gt, 1.0, sel_eq)


def _select_rank(sc, n, k):
    lane = lax.broadcasted_iota(I32, sc.shape, 1)
    rank = jnp.zeros(sc.shape, F32)
    for i in range(n):
        col = sc[:, i:i + 1]
        rank = rank + jnp.where(lane > i, jnp.where(col >= sc, 1.0, 0.0), jnp.where(col > sc, 1.0, 0.0))
    return jnp.where(rank < k, 1.0, 0.0)


LOG2E = 1.4426950408889634


def _masked_attn(qh, k, v, mask):
    s = jnp.where(mask, _dot_nt(qh, k) * (ATT_SCALE * LOG2E), NEG)
    e = jnp.where(mask, jnp.exp2(s - jnp.max(s, axis=-1, keepdims=True)), 0.0)
    den = jnp.maximum(jnp.sum(e, axis=-1, keepdims=True), 1e-30)
    return _dot(e.astype(BF16), v) / den


def _proj_kernel(x_ref, g_ref, w_ref, c128_ref, s128_ref, c64_ref, s64_ref, gqa_ref, gka_ref, gqc_ref, gkc_ref,
                 xn_ref, aq_ref, arow_ref, aiq_ref, aiw_ref, bu_ref, cq_ref, crow_ref, wrow_ref, cg_ref, da_ref,
                 arow_t_ref=None):
    xn = _rms(x_ref[...], g_ref[...]).astype(BF16)
    xn_ref[...] = xn
    cos, sin = c128_ref[...], s128_ref[...]
    cos64, sin64 = c64_ref[...], s64_ref[...]
    lane = lax.broadcasted_iota(I32, cos.shape, 1)
    lo32 = (lane & 63) < 32
    lo64 = lane < 64

    def seg(name):
        a, w = _P[name]
        return _dot_nt(xn, w_ref[a:a + w, :])

    def rope128(v):
        return v * cos + pltpu.roll(v, 64, 1) * sin

    def rope64(v):
        rot = jnp.where(lo32, pltpu.roll(v, 96, 1), pltpu.roll(v, 32, 1))
        return v * cos64 + rot * sin64

    z = seg("aq")
    for h in range(A_HEADS):
        sl = slice(h * HEAD_DIM, (h + 1) * HEAD_DIM)
        aq_ref[:, sl] = rope128(_rms(z[:, sl], gqa_ref[...])).astype(BF16)
    ak = rope128(_rms(seg("ak"), gka_ref[...]))
    av = seg("av")
    aik = rope64(seg("aik"))
    arow_ref[:, 0:128] = ak
    arow_ref[:, 128:256] = av
    arow_ref[:, 256:320] = aik[:, 0:A_IDX_DIM]
    if arow_t_ref is not None:
        arow_t_ref[0:128, :] = ak.T
        arow_t_ref[128:256, :] = av.T
        arow_t_ref[256:320, :] = aik.T[0:A_IDX_DIM, :]
    z = seg("aiq")
    for j in range(A_IDX_HEADS // 2):
        r = rope64(z[:, j * LANES:(j + 1) * LANES])
        aiq_ref[:, (2 * j) * LANES:(2 * j + 1) * LANES] = jnp.where(lo64, r, 0.0).astype(BF16)
        aiq_ref[:, (2 * j + 1) * LANES:(2 * j + 2) * LANES] = jnp.where(lo64, 0.0, r).astype(BF16)
    aiw_ref[...] = seg("aiw") * (A_IDX_HEADS ** -0.5) * (A_IDX_DIM ** -0.5)
    bu_ref[...] = seg("bu")
    z = seg("cq")
    for h in range(C_HEADS):
        sl = slice(h * HEAD_DIM, (h + 1) * HEAD_DIM)
        cq_ref[:, sl] = rope128(_rms(z[:, sl], gqc_ref[...])).astype(BF16)
    z = seg("ckv")
    for br in range(3):
        kk = rope128(_rms(z[:, (2 * br) * LANES:(2 * br + 1) * LANES], gkc_ref[br:br + 1, :]))
        vv = z[:, (2 * br + 1) * LANES:(2 * br + 2) * LANES]
        if br < 2:
            crow_ref[:, (2 * br) * LANES:(2 * br + 1) * LANES] = kk
            crow_ref[:, (2 * br + 1) * LANES:(2 * br + 2) * LANES] = vv
        else:
            wrow_ref[:, 0:LANES] = kk
            wrow_ref[:, LANES:2 * LANES] = vv
    cg_ref[...] = _sigmoid(seg("cg"))
    z = seg("dglu")
    da_ref[...] = z[:, 0:BR_WIDTH] * _sigmoid(z[:, BR_WIDTH:2 * BR_WIDTH])


def _project(x2d, tabs, lw, *, tm, n_pos_blocks, bu_shape, bu_spec, a_rows_t=None):
    n = x2d.shape[0]
    row = lambda w: pl.BlockSpec((tm, w), lambda i: (i, 0))
    tab = pl.BlockSpec((tm, LANES), lambda i: (i % n_pos_blocks, 0))
    ins = [x2d, lw["g_mix"], lw["w1"], tabs[0], tabs[1], tabs[2], tabs[3], lw["a_gq"], lw["a_gk"], lw["c_gq"], lw["c_gk"]]
    in_specs = [row(D_MODEL), _full(ins[1], 1), _full(ins[2], 1), tab, tab, tab, tab] + [_full(a, 1) for a in ins[7:]]
    outs = [((n, D_MODEL), BF16, row(D_MODEL)),
            ((n, 512), BF16, row(512)),
            ((n, A_CACHE_DIM), F32, row(A_CACHE_DIM)),
            ((n, 1024), BF16, row(1024)),
            ((n, LANES), F32, row(LANES)),
            (bu_shape, F32, bu_spec),
            ((n, 512), BF16, row(512)),
            ((n, C_CACHE_DIM), F32, row(C_CACHE_DIM)),
            ((n, C_WIN_DIM), F32, row(C_WIN_DIM)),
            ((n, LANES), F32, row(LANES)),
            ((n, BR_WIDTH), F32, row(BR_WIDTH))]
    if a_rows_t is not None:
        outs.append((a_rows_t[0], F32, a_rows_t[1]))
    return pl.pallas_call(
        _proj_kernel,
        grid=(n // tm,),
        in_specs=in_specs,
        out_specs=[o[2] for o in outs],
        out_shape=[jax.ShapeDtypeStruct(o[0], o[1]) for o in outs],
        compiler_params=_cparams(1),
        name="project",
    )(*ins)


def _dsa_kernel(aq_ref, aiq_ref, aiw_ref, arow_ref, tri_ref, o_ref, k_sc, v_sc, ik_sc, key_sc, *, qb, L, n_sel):
    qi = pl.program_id(1)

    @pl.when(qi == 0)
    def _():
        k_sc[...] = arow_ref[:, 0:128].astype(BF16)
        v_sc[...] = arow_ref[:, 128:256].astype(BF16)
        ik = arow_ref[:, 256:320]
        ik_sc[...] = jnp.concatenate([ik, ik], axis=1).astype(BF16)

    def body(le):
        w = aiw_ref[...]
        score = None
        for h in range(A_IDX_HEADS):
            lg = _dot_nt(aiq_ref[:, h * LANES:(h + 1) * LANES], ik_sc[0:le, :])
            t = jnp.maximum(lg, 0.0) * w[:, h:h + 1]
            score = t if score is None else score + t
        qpos = qi * qb + lax.broadcasted_iota(I32, (qb, 1), 0)
        valid = lax.broadcasted_iota(I32, (qb, le), 1) <= qpos
        keys = key_sc.at[:, 0:le]
        keys[...] = _sortable(jnp.where(valid, score, -jnp.inf))
        sel = _select_rows(keys, n_sel, tri_ref[...])
        mask = jnp.where(valid, sel, 0.0) > 0.5
        q = aq_ref[...]
        for h in range(A_HEADS):
            sl = slice(h * HEAD_DIM, (h + 1) * HEAD_DIM)
            o_ref[:, sl] = _masked_attn(q[:, sl], k_sc[0:le, :], v_sc[0:le, :], mask).astype(BF16)

    _causal_branches(qi, qb, L, body)


def _causal_branches(qi, qb, L, body):
    step = max(qb, L // 8)
    if L % step:
        step = qb
    per = step // qb
    for j in range(L // step):
        @pl.when((qi >= j * per) & (qi < (j + 1) * per))
        def _(j=j):
            body((j + 1) * step)


def _dsa(aq, aiq, aiw, arow3, tri, *, qb):
    bsz, L, _ = arow3.shape
    nq = L // qb
    n_sel = min(A_TOPK, L // 4)
    row = lambda w: pl.BlockSpec((qb, w), lambda b, i: (b * nq + i, 0))
    return pl.pallas_call(
        functools.partial(_dsa_kernel, qb=qb, L=L, n_sel=n_sel),
        grid=(bsz, nq),
        in_specs=[row(512), row(1024), row(LANES),
                  pl.BlockSpec((None, L, A_CACHE_DIM), lambda b, i: (b, 0, 0)), _full(tri, 2)],
        out_specs=row(512),
        out_shape=jax.ShapeDtypeStruct((bsz * L, 512), BF16),
        scratch_shapes=[pltpu.VMEM((L, 128), BF16), pltpu.VMEM((L, 128), BF16), pltpu.VMEM((L, 128), BF16),
                        pltpu.VMEM((qb, L), I32)],
        compiler_params=_cparams(2),
        name="dsa",
    )(aq, aiq, aiw, arow3, tri)


def _summaries(x, a, w, n_cmp):
    ns = x.shape[0] // C_CMP_STRIDE
    x3 = x.reshape(ns, C_CMP_STRIDE, HEAD_DIM)
    lo = jnp.sum(x3 * a[0:C_CMP_STRIDE][None], axis=1)
    hi = jnp.sum(x3 * a[C_CMP_STRIDE:C_CMP_LEN][None], axis=1)
    return lo, hi


def _finish_summaries(lo, hi, w, n_cmp):
    ns = lo.shape[0]
    comb = lo + pltpu.roll(hi, ns - 1, 0)
    comb = jnp.where(lax.broadcasted_iota(I32, comb.shape, 0) < n_cmp, comb, 0.0)
    return _dot(comb.astype(BF16), w).astype(BF16)


def _cmp_kernel(x_ref, a_ref, w_ref, kc_ref, vc_ref, *, n_cmp):
    for t, out in ((0, kc_ref), (1, vc_ref)):
        lo, hi = _summaries(x_ref[:, t * LANES:(t + 1) * LANES], a_ref[t], w_ref[t], n_cmp)
        out[...] = _finish_summaries(lo, hi, w_ref[t], n_cmp)


def _compress(crow3, cmp_a, cmp_w):
    bsz, L, _ = crow3.shape
    ns = L // C_CMP_STRIDE
    n_cmp = (L - C_CMP_LEN) // C_CMP_STRIDE + 1
    out = pl.BlockSpec((None, ns, HEAD_DIM), lambda b: (b, 0, 0))
    return pl.pallas_call(
        functools.partial(_cmp_kernel, n_cmp=n_cmp),
        grid=(bsz,),
        in_specs=[pl.BlockSpec((None, L, C_CACHE_DIM), lambda b: (b, 0, 0)), _full(cmp_a, 1), _full(cmp_w, 1)],
        out_specs=[out, out],
        out_shape=[jax.ShapeDtypeStruct((bsz, ns, HEAD_DIM), BF16)] * 2,
        compiler_params=_cparams(1),
        name="compress",
    )(crow3, cmp_a, cmp_w)


def _block_scores(imp, qpos):
    j = lax.broadcasted_iota(I32, imp.shape, 1)
    cur = lax.shift_right_logical(qpos, 6)
    forced = (j == 0) | (j == cur) | (j == cur - 1)
    return jnp.where(j <= cur, jnp.where(forced, C_FORCE, imp), -jnp.inf)


def _cmp_softmax(qh, kc, cvalid):
    s = _dot_nt(qh, kc) * ATT_SCALE
    s = jnp.where(cvalid, s, NEG)
    m = jnp.max(s, axis=-1, keepdims=True)
    e = jnp.where(cvalid, jnp.exp(s - m), 0.0)
    return e / jnp.maximum(jnp.sum(e, axis=-1, keepdims=True), 1e-30)


def _nsa_kernel(cq_ref, cg_ref, crow_ref, wrow_ref, kc_ref, vc_ref, ov_ref, ex_ref, o_ref,
                ks_sc, vs_sc, kw_sc, vw_sc, os_sc, *, qb, L, n_cmp, n_blk, n_top, wsl):
    qi = pl.program_id(1)

    @pl.when(qi == 0)
    def _():
        ks_sc[...] = crow_ref[:, 256:384].astype(BF16)
        vs_sc[...] = crow_ref[:, 384:512].astype(BF16)
        kw_sc[...] = wrow_ref[:, 0:128].astype(BF16)
        vw_sc[...] = wrow_ref[:, 128:256].astype(BF16)

    q0 = qi * qb
    qpos = q0 + lax.broadcasted_iota(I32, (qb, 1), 0)
    q = cq_ref[...]
    g = cg_ref[...]
    kc, vc = kc_ref[...], vc_ref[...]
    ncp = kc.shape[0]
    n_io = lax.broadcasted_iota(I32, (qb, ncp), 1)
    cvalid = (n_io < n_cmp) & (n_io * C_CMP_STRIDE + (C_CMP_LEN - 1) <= qpos)
    o_c, psum = [], None
    for h in range(C_HEADS):
        p = _cmp_softmax(q[:, h * HEAD_DIM:(h + 1) * HEAD_DIM], kc, cvalid)
        o_c.append(_dot(p.astype(BF16), vc))
        psum = p if psum is None else psum + p
    imp = _dot_split3(psum, ov_ref[...])
    sel = _select_rank(_block_scores(imp, qpos), n_blk, n_top).astype(BF16)

    def selected(le):
        selk = _dot(sel, ex_ref[:, 0:le])
        smask = jnp.where(lax.broadcasted_iota(I32, (qb, le), 1) <= qpos, selk, 0.0) > 0.5
        for h in range(C_HEADS):
            sl = slice(h * HEAD_DIM, (h + 1) * HEAD_DIM)
            os_sc[:, sl] = _masked_attn(q[:, sl], ks_sc[0:le, :], vs_sc[0:le, :], smask)

    _causal_branches(qi, qb, L, selected)
    start = pl.multiple_of(jnp.minimum(jnp.maximum(q0 - C_WINDOW, 0), L - wsl), qb)
    dist = qpos - (start + lax.broadcasted_iota(I32, (qb, wsl), 1))
    wmask = (dist >= 0) & (dist <= C_WINDOW)
    kw = kw_sc[pl.ds(start, wsl), :]
    vw = vw_sc[pl.ds(start, wsl), :]
    for h in range(C_HEADS):
        sl = slice(h * HEAD_DIM, (h + 1) * HEAD_DIM)
        o_w = _masked_attn(q[:, sl], kw, vw, wmask)
        out = g[:, 3 * h:3 * h + 1] * o_c[h] + g[:, 3 * h + 1:3 * h + 2] * os_sc[:, sl] + g[:, 3 * h + 2:3 * h + 3] * o_w
        o_ref[:, sl] = out.astype(BF16)


def _nsa(cq, cg, crow3, wrow3, kcmp, vcmp, ov, ex, *, qb):
    bsz, L, _ = crow3.shape
    nq = L // qb
    ns = kcmp.shape[1]
    n_cmp = (L - C_CMP_LEN) // C_CMP_STRIDE + 1
    n_blk = -(-L // C_SLC_BLOCK)
    wsl = min(L, C_WINDOW + qb)
    row = lambda w: pl.BlockSpec((qb, w), lambda b, i: (b * nq + i, 0))
    per_b = lambda r, w: pl.BlockSpec((None, r, w), lambda b, i: (b, 0, 0))
    return pl.pallas_call(
        functools.partial(_nsa_kernel, qb=qb, L=L, n_cmp=n_cmp, n_blk=n_blk, n_top=min(C_TOPN, n_blk), wsl=wsl),
        grid=(bsz, nq),
        in_specs=[row(512), row(LANES), per_b(L, C_CACHE_DIM), per_b(L, C_WIN_DIM),
                  per_b(ns, HEAD_DIM), per_b(ns, HEAD_DIM), _full(ov, 2), _full(ex, 2)],
        out_specs=row(512),
        out_shape=jax.ShapeDtypeStruct((bsz * L, 512), BF16),
        scratch_shapes=[pltpu.VMEM((L, 128), BF16)] * 4 + [pltpu.VMEM((qb, 512), F32)],
        compiler_params=_cparams(2),
        name="nsa",
    )(cq, cg, crow3, wrow3, kcmp, vcmp, ov, ex)


def _s5_kernel(u_ref, bdr_ref, bdi_ref, ar_ref, ai_ref, cdr_ref, cdi_ref, d_ref, gw_ref, gb_ref,
               o_ref, hr_ref, hi_ref, xr_sc, xi_sc, h_sc, *, tc, nb):
    i = pl.program_id(0)

    @pl.when(i == 0)
    def _():
        h_sc[...] = jnp.zeros_like(h_sc)

    u = u_ref[...]
    ub = u.astype(BF16)
    xr_sc[...] = _dot(ub, bdr_ref[...])
    xi_sc[...] = _dot(ub, bdi_ref[...])
    ar = jnp.broadcast_to(ar_ref[...], (nb, B_LANES))
    ai = jnp.broadcast_to(ai_ref[...], (nb, B_LANES))

    def step(t, carry):
        hr, hi = carry
        r0 = pl.multiple_of(t * nb, nb)
        nhr = ar * hr - ai * hi + xr_sc[pl.ds(r0, nb), :]
        nhi = ar * hi + ai * hr + xi_sc[pl.ds(r0, nb), :]
        xr_sc[pl.ds(r0, nb), :] = nhr
        xi_sc[pl.ds(r0, nb), :] = nhi
        return nhr, nhi

    hr, hi = lax.fori_loop(0, tc, step, (h_sc[0], h_sc[1]))
    h_sc[0] = hr
    h_sc[1] = hi
    hr_ref[...] = hr
    hi_ref[...] = hi
    y = _dot(xr_sc[...].astype(BF16), cdr_ref[...]) - _dot(xi_sc[...].astype(BF16), cdi_ref[...]) + d_ref[...] * u
    gl = _gelu(y)
    o_ref[...] = (gl * _sigmoid(_dot(gl.astype(BF16), gw_ref[...]) + gb_ref[...])).astype(BF16)


def _s5(u_tm, sp, *, nb, tc):
    rows = u_tm.shape[0]
    r = tc * nb
    consts = [sp["bdr"], sp["bdi"], sp["ar"], sp["ai"], sp["cdr"], sp["cdi"], sp["d"], sp["glu_w"], sp["glu_b"]]
    st = pl.BlockSpec((nb, B_LANES), lambda i: (0, 0))
    return pl.pallas_call(
        functools.partial(_s5_kernel, tc=tc, nb=nb),
        grid=(rows // r,),
        in_specs=[pl.BlockSpec((r, BR_WIDTH), lambda i: (i, 0))] + [_full(c, 1) for c in consts],
        out_specs=[pl.BlockSpec((r, BR_WIDTH), lambda i: (i, 0)), st, st],
        out_shape=[jax.ShapeDtypeStruct((rows, BR_WIDTH), BF16),
                   jax.ShapeDtypeStruct((nb, B_LANES), F32), jax.ShapeDtypeStruct((nb, B_LANES), F32)],
        scratch_shapes=[pltpu.VMEM((r, B_LANES), F32), pltpu.VMEM((r, B_LANES), F32), pltpu.VMEM((2, nb, B_LANES), F32)],
        compiler_params=_cparams(1),
        name="s5",
    )(u_tm, *consts)


HALO = 32


def _ln_swish(y, g, b):
    yc = y - jnp.mean(y, axis=-1, keepdims=True)
    yn = yc * lax.rsqrt(jnp.mean(yc * yc, axis=-1, keepdims=True) + NORM_EPS) * g + b
    return yn * _sigmoid(yn)


def _conv_kernel(cur_ref, halo_ref, w_ref, b_ref, lg_ref, lb_ref, o_ref, ext_sc, *, tm):
    i = pl.program_id(1)
    ext_sc[0:HALO, :] = jnp.where(i == 0, 0.0, halo_ref[...])
    ext_sc[HALO:HALO + tm, :] = cur_ref[...]
    acc = jnp.zeros((tm, BR_WIDTH), F32)
    for j in range(D_CONV):
        acc = acc + w_ref[j:j + 1, :] * ext_sc[pl.ds(HALO - (D_CONV - 1) + j, tm), :]
    o_ref[...] = _ln_swish(acc + b_ref[...], lg_ref[...], lb_ref[...]).astype(BF16)


def _conv(da3, cw, cb, lg, lb, *, tm):
    bsz, T, _ = da3.shape
    nt = T // tm
    hb = tm // HALO
    consts = [cw, cb, lg, lb]
    return pl.pallas_call(
        functools.partial(_conv_kernel, tm=tm),
        grid=(bsz, nt),
        in_specs=[pl.BlockSpec((None, tm, BR_WIDTH), lambda b, i: (b, i, 0)),
                  pl.BlockSpec((None, HALO, BR_WIDTH), lambda b, i: (b, jnp.maximum(i * hb - 1, 0), 0))]
        + [_full(c, 2) for c in consts],
        out_specs=pl.BlockSpec((tm, BR_WIDTH), lambda b, i: (b * nt + i, 0)),
        out_shape=jax.ShapeDtypeStruct((bsz * T, BR_WIDTH), BF16),
        scratch_shapes=[pltpu.VMEM((HALO + tm, BR_WIDTH), F32)],
        compiler_params=_cparams(2),
        name="conv",
    )(da3, da3, *consts)


MIX_CW = 512


def _mix_kernel(xn_ref, oa_ref, ob_ref, oc_ref, od_ref, wg_ref, wbr_ref, wo_ref, hm_ref, acc_sc):
    i = pl.program_id(1)
    xn = xn_ref[...]
    br = jnp.where(i == 0, oa_ref[...], jnp.where(i == 1, ob_ref[...], jnp.where(i == 2, oc_ref[...], od_ref[...])))
    for c in range(D_MODEL // MIX_CW):
        sl = slice(c * MIX_CW, (c + 1) * MIX_CW)
        contrib = _sigmoid(_dot_nt(xn, wg_ref[sl, :])) * _dot(br, wbr_ref[:, sl])

        @pl.when(i == 0)
        def _():
            acc_sc[:, sl] = contrib

        @pl.when(i > 0)
        def _():
            acc_sc[:, sl] = acc_sc[:, sl] + contrib

    @pl.when(i == N_BRANCH - 1)
    def _():
        hm_ref[...] = _dot(acc_sc[...].astype(BF16), wo_ref[...])


def _mix(xn, oa, ob, ob_spec, oc, od, wg, wbr, wo, *, tm):
    n = xn.shape[0]
    row = lambda w: pl.BlockSpec((tm, w), lambda r, i: (r, 0))
    return pl.pallas_call(
        _mix_kernel,
        grid=(n // tm, N_BRANCH),
        in_specs=[row(D_MODEL), row(BR_WIDTH), ob_spec, row(BR_WIDTH), row(BR_WIDTH),
                  pl.BlockSpec((D_MODEL, D_MODEL), lambda r, i: (i, 0)),
                  pl.BlockSpec((None, BR_WIDTH, D_MODEL), lambda r, i: (i, 0, 0)),
                  pl.BlockSpec((D_MODEL, D_MODEL), lambda r, i: (0, 0))],
        out_specs=row(D_MODEL),
        out_shape=jax.ShapeDtypeStruct((n, D_MODEL), F32),
        scratch_shapes=[pltpu.VMEM((tm, D_MODEL), F32)],
        compiler_params=_cparams(2),
        name="mix",
    )(xn, oa, ob, oc, od, wg, wbr, wo)


MLP_FC = 1024


def _mlp_kernel(x_ref, hm_ref, g_ref, wu_ref, wd_ref, y_ref, hn_sc, acc_sc):
    j = pl.program_id(1)

    @pl.when(j == 0)
    def _():
        hn_sc[...] = _rms(x_ref[...] + hm_ref[...], g_ref[...]).astype(BF16)

    up = _dot(hn_sc[...], wu_ref[...])
    act = jnp.square(jnp.maximum(up, 0.0)).astype(BF16)
    d = _dot(act, wd_ref[...])

    @pl.when(j == 0)
    def _():
        acc_sc[...] = d

    @pl.when(j > 0)
    def _():
        acc_sc[...] = acc_sc[...] + d

    @pl.when(j == pl.num_programs(1) - 1)
    def _():
        y_ref[...] = (x_ref[...] + hm_ref[...]) + acc_sc[...]


def _mlp(x2d, hm, g, wu, wd, *, tm):
    n = x2d.shape[0]
    row = pl.BlockSpec((tm, D_MODEL), lambda r, j: (r, 0))
    return pl.pallas_call(
        _mlp_kernel,
        grid=(n // tm, D_FF // MLP_FC),
        in_specs=[row, row, _full(g, 2),
                  pl.BlockSpec((D_MODEL, MLP_FC), lambda r, j: (0, j)),
                  pl.BlockSpec((MLP_FC, D_MODEL), lambda r, j: (j, 0))],
        out_specs=row,
        out_shape=jax.ShapeDtypeStruct((n, D_MODEL), F32),
        scratch_shapes=[pltpu.VMEM((tm, D_MODEL), BF16), pltpu.VMEM((tm, D_MODEL), F32)],
        compiler_params=_cparams(2),
        name="mlp",
    )(x2d, hm, g, wu, wd)


PAGES_PER_STEP = 8


def _tile_attention(s_ref, bias_ref, v_ref, n, s_new, bias_new, v_new, v_transposed):
    c2 = ATT_SCALE * LOG2E
    s = s_ref[...] * c2 + bias_ref[...]
    sn = s_new * c2 + bias_new
    m = jnp.maximum(jnp.max(jnp.max(s, axis=0), axis=1, keepdims=True), sn)
    e = jnp.exp2(s - m)
    en = jnp.exp2(sn - m)
    den = jnp.sum(jnp.sum(e, axis=0), axis=1, keepdims=True) + en
    acc = en.astype(BF16).astype(F32) * v_new.astype(BF16).astype(F32)
    mm = _dot_nt if v_transposed else _dot
    for c in range(n):
        acc = acc + mm(e[c].astype(BF16), v_ref[c * LANES:(c + 1) * LANES, :])
    return acc / den


KEY_TILES = 16


def _dsa_s_kernel(pt_ref, q_ref, iq_ref, w_ref, new_ref, *rest, pg, n_pages, n_sel):
    pages = rest[:pg]
    tri_ref, tril_ref, o_ref, s_sc, bias_sc, vt_sc, sc_sc, key_sc = rest[pg:]
    g = pl.program_id(1)
    q = q_ref[...].astype(BF16)
    iq = iq_ref[...].astype(BF16)
    w = w_ref[...]

    @pl.when(g == 0)
    def _():
        sc_sc[...] = jnp.full(sc_sc.shape, -jnp.inf, F32)

    for i in range(pg):
        c = g * pg + i
        page = pages[i]
        s_sc[c] = _dot(q, page[0:128, :].astype(BF16))
        vt_sc[pl.ds(pl.multiple_of(c * LANES, LANES), LANES), :] = page[128:256, :].astype(BF16)
        lg = _dot(iq, page[192:320, :].astype(BF16))
        sc_sc[g, i:i + 1, :] = jnp.sum(jnp.maximum(lg, 0.0) * w, axis=0, keepdims=True)

    @pl.when(g == pl.num_programs(1) - 1)
    def _():
        new = new_ref[...]
        k_new = new[0:1, 0:128].astype(BF16).astype(F32)
        s_new = jnp.sum(q.astype(F32) * k_new, axis=-1, keepdims=True)
        ik_new = new[0:1, 256:320].astype(BF16).astype(F32)
        lg_new = jnp.sum(iq[:, A_IDX_DIM:2 * A_IDX_DIM].astype(F32) * ik_new, axis=-1, keepdims=True)
        sc_new = jnp.sum(jnp.maximum(lg_new, 0.0) * w[:, 0:1], axis=0, keepdims=True)
        t_new = n_pages // pg
        first = (lax.broadcasted_iota(I32, (SUBLANES, LANES), 0) == 0) & (lax.broadcasted_iota(I32, (SUBLANES, LANES), 1) == 0)
        sc_sc[t_new] = jnp.where(first, sc_new, -jnp.inf)
        key_sc[...] = _sortable(sc_sc[...])
        sel = _select_packed(key_sc, n_sel, tri_ref[...], tril_ref[...])
        for c in range(n_pages):
            row = sel[c // pg][c % pg:c % pg + 1, :]
            bias_sc[c] = jnp.where(jnp.broadcast_to(row, (SUBLANES, LANES)) > 0.5, 0.0, NEG)
        bias_new = jnp.where(sel[t_new][0:1, 0:1] > 0.5, 0.0, NEG)
        o_ref[...] = _tile_attention(s_sc, bias_sc, vt_sc, n_pages, s_new, bias_new, new[0:1, 128:256], True)


def _dsa_sample(page_table, cache_t, layer, q8, iq8, w8, new8, tri, tril):
    bsz, n_pages = page_table.shape
    pg = PAGES_PER_STEP
    assert pg == SUBLANES and n_pages % pg == 0 and n_pages * PAGE + 1 <= KEY_TILES * SUBLANES * LANES
    n_sel = min(A_TOPK, (n_pages * PAGE + 1) // 4)
    per_b = lambda r, w: pl.BlockSpec((None, r, w), lambda b, g, pt: (b, 0, 0))
    cst = lambda a: pl.BlockSpec(a.shape, lambda b, g, pt: (0,) * a.ndim)
    page_spec = lambda i: pl.BlockSpec((None, None, A_CACHE_DIM, PAGE),
                                       lambda b, g, pt: (layer, pt[b, g * pg + i], 0, 0))
    grid_spec = pltpu.PrefetchScalarGridSpec(
        num_scalar_prefetch=1,
        grid=(bsz, n_pages // pg),
        in_specs=[per_b(SUBLANES, LANES), per_b(SUBLANES, LANES), per_b(SUBLANES, LANES), per_b(SUBLANES, A_CACHE_DIM)]
        + [page_spec(i) for i in range(pg)] + [cst(tri), cst(tril)],
        out_specs=per_b(SUBLANES, HEAD_DIM),
        scratch_shapes=[pltpu.VMEM((n_pages, SUBLANES, LANES), F32), pltpu.VMEM((n_pages, SUBLANES, LANES), F32),
                        pltpu.VMEM((n_pages * LANES, PAGE), BF16),
                        pltpu.VMEM((KEY_TILES, SUBLANES, LANES), F32), pltpu.VMEM((KEY_TILES, SUBLANES, LANES), I32)],
    )
    return pl.pallas_call(
        functools.partial(_dsa_s_kernel, pg=pg, n_pages=n_pages, n_sel=n_sel),
        grid_spec=grid_spec,
        out_shape=jax.ShapeDtypeStruct((bsz, SUBLANES, HEAD_DIM), F32),
        compiler_params=_cparams(2),
        name="dsa_sample",
    )(page_table, q8, iq8, w8, new8, *([cache_t] * pg), tri, tril)


def _nsa_s_kernel(pt_ref, q_ref, g3_ref, new_ref, win_ref, wnew_ref, a_ref, w_ref, ov_ref, *rest,
                  pg, n_pages, n_cmp, n_blk, n_top, past):
    pages = rest[:pg]
    o_ref, ss_sc, bias_sc, vs_sc, lok_sc, hik_sc, lov_sc, hiv_sc = rest[pg:]
    g = pl.program_id(1)
    q = q_ref[...].astype(BF16)
    sub = PAGE // C_CMP_STRIDE

    for i in range(pg):
        c = g * pg + i
        page = pages[i]
        r0 = pl.multiple_of(c * sub, sub)
        lo, hi = _summaries(page[:, 0:128], a_ref[0], None, n_cmp)
        lok_sc[pl.ds(r0, sub), :] = lo
        hik_sc[pl.ds(r0, sub), :] = hi
        lo, hi = _summaries(page[:, 128:256], a_ref[1], None, n_cmp)
        lov_sc[pl.ds(r0, sub), :] = lo
        hiv_sc[pl.ds(r0, sub), :] = hi
        ss_sc[c] = _dot_nt(q, page[:, 256:384].astype(BF16))
        vs_sc[pl.ds(pl.multiple_of(c * LANES, LANES), LANES), :] = page[:, 384:512].astype(BF16)

    @pl.when(g == pl.num_programs(1) - 1)
    def _():
        qf = q.astype(F32)
        new = new_ref[...]
        lane = lax.broadcasted_iota(I32, (SUBLANES, LANES), 1)
        qpos = jnp.full((SUBLANES, 1), past, I32)
        kc = _finish_summaries(lok_sc[...], hik_sc[...], w_ref[0], n_cmp)
        vc = _finish_summaries(lov_sc[...], hiv_sc[...], w_ref[1], n_cmp)
        ncp = kc.shape[0]
        n_io = lax.broadcasted_iota(I32, (SUBLANES, ncp), 1)
        cvalid = (n_io < n_cmp) & (n_io * C_CMP_STRIDE + (C_CMP_LEN - 1) <= qpos)
        p = _cmp_softmax(q, kc, cvalid)
        o_c = _dot(p.astype(BF16), vc)
        head = lax.broadcasted_iota(I32, p.shape, 0) < C_HEADS
        psum = jnp.broadcast_to(jnp.sum(jnp.where(head, p, 0.0), axis=0, keepdims=True), p.shape)
        imp = _dot_split3(psum, ov_ref[...])
        sel = _select_rank(_block_scores(imp, qpos), n_blk, n_top)
        for c in range(n_pages):
            pick = jnp.where(lane < C_SLC_BLOCK, sel[:, 2 * c:2 * c + 1], sel[:, 2 * c + 1:2 * c + 2])
            bias_sc[c] = jnp.where(pick > 0.5, 0.0, NEG)
        k_new = new[0:1, 256:384].astype(BF16).astype(F32)
        s_new = jnp.sum(qf * k_new, axis=-1, keepdims=True)
        bias_new = jnp.where(sel[:, 2 * n_pages:2 * n_pages + 1] > 0.5, 0.0, NEG)
        o_s = _tile_attention(ss_sc, bias_sc, vs_sc, n_pages, s_new, bias_new, new[0:1, 384:512], False)
        wb = win_ref.shape[0]
        kw = win_ref[:, 0:128].astype(BF16)
        vw = win_ref[:, 128:256].astype(BF16)
        wnew = wnew_ref[...]
        s_w = _dot_nt(q, kw) * ATT_SCALE
        dist = wb - lax.broadcasted_iota(I32, (SUBLANES, wb), 1)
        wvalid = (dist <= C_WINDOW) & (past - dist >= 0)
        s_w = jnp.where(wvalid, s_w, NEG)
        s_n = jnp.sum(qf * wnew[0:1, 0:128].astype(BF16).astype(F32), axis=-1, keepdims=True) * ATT_SCALE
        m = jnp.maximum(jnp.max(s_w, axis=-1, keepdims=True), s_n)
        e_w = jnp.where(wvalid, jnp.exp(s_w - m), 0.0)
        e_n = jnp.exp(s_n - m)
        den = jnp.sum(e_w, axis=-1, keepdims=True) + e_n
        v_n = wnew[0:1, 128:256].astype(BF16).astype(F32)
        o_w = (_dot(e_w.astype(BF16), vw) + e_n.astype(BF16).astype(F32) * v_n) / den
        o_ref[...] = g3_ref[0] * o_c + g3_ref[1] * o_s + g3_ref[2] * o_w


def _nsa_sample(page_table, cache, layer, q8, g3, new8, win, wnew8, cmp_a, cmp_w, ov):
    bsz, n_pages = page_table.shape
    pg = PAGES_PER_STEP
    past = n_pages * PAGE
    n_cmp = (past + 1 - C_CMP_LEN) // C_CMP_STRIDE + 1
    n_blk = -(-(past + 1) // C_SLC_BLOCK)
    ns = past // C_CMP_STRIDE
    wb = win.shape[2]
    cst = lambda a: pl.BlockSpec(a.shape, lambda b, g, pt: (0,) * a.ndim)
    per_b = lambda r, w: pl.BlockSpec((None, r, w), lambda b, g, pt: (b, 0, 0))
    page_spec = lambda i: pl.BlockSpec((None, None, PAGE, C_CACHE_DIM),
                                       lambda b, g, pt: (layer, pt[b, g * pg + i], 0, 0))
    grid_spec = pltpu.PrefetchScalarGridSpec(
        num_scalar_prefetch=1,
        grid=(bsz, n_pages // pg),
        in_specs=[per_b(SUBLANES, LANES),
                  pl.BlockSpec((None, 3, SUBLANES, LANES), lambda b, g, pt: (b, 0, 0, 0)),
                  per_b(SUBLANES, C_CACHE_DIM),
                  pl.BlockSpec((None, None, wb, C_WIN_DIM), lambda b, g, pt: (layer, b, 0, 0)),
                  per_b(SUBLANES, C_WIN_DIM), cst(cmp_a), cst(cmp_w), cst(ov)]
        + [page_spec(i) for i in range(pg)],
        out_specs=per_b(SUBLANES, HEAD_DIM),
        scratch_shapes=[pltpu.VMEM((n_pages, SUBLANES, LANES), F32), pltpu.VMEM((n_pages, SUBLANES, LANES), F32),
                        pltpu.VMEM((n_pages * LANES, HEAD_DIM), BF16)]
        + [pltpu.VMEM((ns, HEAD_DIM), F32)] * 4,
    )
    return pl.pallas_call(
        functools.partial(_nsa_s_kernel, pg=pg, n_pages=n_pages, n_cmp=n_cmp, n_blk=n_blk, n_top=min(C_TOPN, n_blk),
                          past=past),
        grid_spec=grid_spec,
        out_shape=jax.ShapeDtypeStruct((bsz, SUBLANES, HEAD_DIM), F32),
        compiler_params=_cparams(2),
        name="nsa_sample",
    )(page_table, q8, g3, new8, win, wnew8, cmp_a, cmp_w, ov, *([cache] * pg))


def _step_kernel(u_ref, h0r_ref, h0i_ref, bdr_h_ref, bdr_l_ref, bdi_h_ref, bdi_l_ref, ar_ref, ai_ref,
                 cdr_ref, cdi_ref, d_ref, gw_ref, gb_ref, da_ref, cst_ref, cw_ref, cb_ref, lg_ref, lb_ref,
                 ob_ref, hr_ref, hi_ref, od_ref):
    u = u_ref[...]
    uh = u.astype(BF16)
    ul = (u - uh.astype(F32)).astype(BF16)

    def bmat(h_ref, l_ref):
        return _dot(uh, h_ref[...]) + (_dot(uh, l_ref[...]) + _dot(ul, h_ref[...]))

    ar, ai = ar_ref[...], ai_ref[...]
    h0r, h0i = h0r_ref[...], h0i_ref[...]
    hr = bmat(bdr_h_ref, bdr_l_ref) + (ar * h0r - ai * h0i)
    hi = bmat(bdi_h_ref, bdi_l_ref) + (ar * h0i + ai * h0r)
    hr_ref[...] = hr
    hi_ref[...] = hi
    y = _dot(hr.astype(BF16), cdr_ref[...]) - _dot(hi.astype(BF16), cdi_ref[...]) + d_ref[...] * u
    gl = _gelu(y)
    ob_ref[...] = (gl * _sigmoid(_dot(gl.astype(BF16), gw_ref[...]) + gb_ref[...])).astype(BF16)
    cw = cw_ref[...]
    y = jnp.sum(cst_ref[...] * cw[0:D_CONV - 1][None], axis=1) + cw[D_CONV - 1:D_CONV] * da_ref[...] + cb_ref[...]
    od_ref[...] = _ln_swish(y, lg_ref[...], lb_ref[...]).astype(BF16)


def _sample_step(u, h0r, h0i, sp, da, conv_state, cw, cb, lg, lb):
    bsz = u.shape[0]
    ins = [u, h0r, h0i, sp["bdr"], sp["bdr_lo"], sp["bdi"], sp["bdi_lo"], sp["ar"], sp["ai"], sp["cdr"], sp["cdi"],
           sp["d"], sp["glu_w"], sp["glu_b"], da, conv_state, cw, cb, lg, lb]
    outs = [((bsz, BR_WIDTH), BF16), ((bsz, B_LANES), F32), ((bsz, B_LANES), F32), ((bsz, BR_WIDTH), BF16)]
    return pl.pallas_call(
        _step_kernel,
        grid=(1,),
        in_specs=[_full(a, 1) for a in ins],
        out_specs=[pl.BlockSpec(o[0], lambda i: (0, 0)) for o in outs],
        out_shape=[jax.ShapeDtypeStruct(o[0], o[1]) for o in outs],
        compiler_params=_cparams(1),
        name="sample_step",
    )(*ins)


def _rope_tables(pos):
    pos = pos.astype(F32)[:, None]

    def tab(dim):
        half = dim // 2
        inv = ROPE_THETA ** (-jnp.arange(half, dtype=F32) / half)
        ang = pos * inv
        cos, sin = jnp.cos(ang), jnp.sin(ang)
        reps = LANES // dim
        return jnp.tile(jnp.concatenate([cos, cos], axis=1), (1, reps)), jnp.tile(jnp.concatenate([-sin, sin], axis=1), (1, reps))

    c128, s128 = tab(HEAD_DIM)
    c64, s64 = tab(A_IDX_DIM)
    return c128, s128, c64, s64


def _pack_w1(wt):
    def padded(a, b, rows):
        return jnp.pad(wt[a:b], ((0, rows - (b - a)), (0, 0)))
    parts = [wt[_O[0]:_O[4]], padded(_O[4], _O[5], 128), padded(_O[5], _O[6], 128), wt[_O[6]:_O[9]],
             padded(_O[9], _O[10], 128), wt[_O[10]:_O[11]]]
    return jnp.concatenate(parts, axis=0).astype(BF16)


def _s5_params(lam_re, lam_im, log_dt, b_re, b_im, c_re, c_im, d, glu_w, glu_b):
    lr, li = lam_re.astype(F32), lam_im.astype(F32)
    dt = jnp.exp(log_dt.astype(F32))[:, None]
    mag = jnp.exp(lr * dt)
    ar, ai = mag * jnp.cos(li * dt), mag * jnp.sin(li * dt)
    den = lr * lr + li * li
    fr = ((ar - 1.0) * lr + ai * li) / den
    fi = (ai * lr - (ar - 1.0) * li) / den
    br, bi = b_re.astype(F32), b_im.astype(F32)
    bbr = fr[..., None] * br - fi[..., None] * bi
    bbi = fr[..., None] * bi + fi[..., None] * br
    eye = jnp.eye(B_GROUPS, dtype=F32)
    bd = lambda m: jnp.einsum("gpc,gh->gchp", m, eye).reshape(BR_WIDTH, B_LANES)
    cd = lambda m: jnp.einsum("gcp,gh->gphc", m.astype(F32), eye).reshape(B_LANES, BR_WIDTH)
    bdr, bdi = bd(bbr), bd(bbi)
    hi_lo = lambda m: (m.astype(BF16), (m - m.astype(BF16).astype(F32)).astype(BF16))
    bdr_h, bdr_l = hi_lo(bdr)
    bdi_h, bdi_l = hi_lo(bdi)
    return dict(bdr=bdr_h, bdr_lo=bdr_l, bdi=bdi_h, bdi_lo=bdi_l,
                ar=ar.reshape(1, B_LANES), ai=ai.reshape(1, B_LANES),
                cdr=cd(c_re).astype(BF16), cdi=cd(c_im).astype(BF16), d=d.astype(F32).reshape(1, BR_WIDTH),
                glu_w=glu_w.astype(BF16), glu_b=glu_b.astype(F32).reshape(1, BR_WIDTH))


def _overlap(n_cmp, n_blk, rows, cols):
    start = np.arange(n_cmp)[:, None] * C_CMP_STRIDE
    blk = np.arange(n_blk)[None, :]
    m = (start <= (blk + 1) * C_SLC_BLOCK - 1) & (start + C_CMP_LEN - 1 >= blk * C_SLC_BLOCK)
    out = np.zeros((rows, cols), np.float32)
    out[:n_cmp, :n_blk] = m
    return jnp.asarray(out, BF16)


def _expand(n_keys):
    e = (np.arange(LANES)[:, None] == (np.arange(n_keys)[None, :] // C_SLC_BLOCK)).astype(np.float32)
    return jnp.asarray(e, BF16)


def _tri(lower=False):
    i = np.arange(LANES)
    m = (i[:, None] > i[None, :]) if lower else (i[:, None] < i[None, :])
    return jnp.asarray(m.astype(np.float32), BF16)


def _make_consts(T, sb, past):
    n_cmp_p = (T - C_CMP_LEN) // C_CMP_STRIDE + 1
    n_blk_p = -(-T // C_SLC_BLOCK)
    n_cmp_s = (past + 1 - C_CMP_LEN) // C_CMP_STRIDE + 1
    n_blk_s = -(-(past + 1) // C_SLC_BLOCK)
    return dict(
        tabs_p=_rope_tables(jnp.arange(T)),
        tabs_s=_rope_tables(jnp.full((sb,), past)),
        tri=_tri(),
        tril=_tri(lower=True),
        ov_p=_overlap(n_cmp_p, n_blk_p, T // C_CMP_STRIDE, LANES),
        ex_p=_expand(T),
        ov_s=_overlap(n_cmp_s, n_blk_s, past // C_CMP_STRIDE, -(-n_blk_s // LANES) * LANES),
    )


def _pick_tile(n, cands):
    for c in cands:
        if n % c == 0:
            return c
    return n


def _pad_rows(a, rows):
    return jnp.pad(a[:, None, :], ((0, 0), (0, rows - 1), (0, 0)))


def _layer_weights(l, norm_mix, w_in, a_gq, a_gk, c_gq, c_gk, c_cmp_a, c_cmp_w, d_conv_w, d_conv_b, d_ln_g, d_ln_b,
                   w_br, w_o, norm_mlp, w_up, w_down):
    row = lambda v: v.astype(F32).reshape(1, -1)
    return dict(
        g_mix=row(norm_mix[l]), w1=_pack_w1(jnp.transpose(w_in[l])),
        wg=jnp.transpose(w_in[l])[_O[11]:_O[12]].astype(BF16),
        a_gq=row(a_gq[l]), a_gk=row(a_gk[l]), c_gq=row(c_gq[l]), c_gk=c_gk[l].astype(F32),
        cmp_a=c_cmp_a[l].astype(F32), cmp_w=c_cmp_w[l].astype(BF16),
        conv_w=jnp.pad(d_conv_w[l].astype(F32), ((0, 1), (0, 0))), conv_b=row(d_conv_b[l]),
        ln_g=row(d_ln_g[l]), ln_b=row(d_ln_b[l]),
        w_br=w_br[l].astype(BF16), w_o=w_o[l].astype(BF16), g_mlp=row(norm_mlp[l]),
        w_up=w_up[l].astype(BF16), w_down=w_down[l].astype(BF16))


def _prompt_layer(x, lw, sp, consts):
    bsz, T, _ = x.shape
    n = bsz * T
    x2d = x.reshape(n, D_MODEL)
    tm = _pick_tile(T, (256, 128))
    nt = T // tm
    qb = 128
    bu_spec = pl.BlockSpec((tm, BR_WIDTH), lambda i: (i % nt, i // nt))
    a_t = ((bsz, A_CACHE_DIM, T), pl.BlockSpec((None, A_CACHE_DIM, tm), lambda i: (i // nt, 0, i % nt)))
    (xn, aq, arow, aiq, aiw, bu, cq, crow, wrow, cg, da, arow_t) = _project(
        x2d, consts["tabs_p"], lw, tm=tm, n_pos_blocks=nt, bu_shape=(T, bsz * BR_WIDTH), bu_spec=bu_spec, a_rows_t=a_t)
    arow3 = arow.reshape(bsz, T, A_CACHE_DIM)
    crow3 = crow.reshape(bsz, T, C_CACHE_DIM)
    wrow3 = wrow.reshape(bsz, T, C_WIN_DIM)
    da3 = da.reshape(bsz, T, BR_WIDTH)
    o_a = _dsa(aq, aiq, aiw, arow3, consts["tri"], qb=qb)
    kcmp, vcmp = _compress(crow3, lw["cmp_a"], lw["cmp_w"])
    o_c = _nsa(cq, cg, crow3, wrow3, kcmp, vcmp, consts["ov_p"], consts["ex_p"], qb=qb)
    tc = _pick_tile(T, (128, 64))
    o_b, hr, hi = _s5(bu.reshape(T * bsz, BR_WIDTH), sp, nb=bsz, tc=tc)
    o_d = _conv(da3, lw["conv_w"], lw["conv_b"], lw["ln_g"], lw["ln_b"], tm=tm)
    tmx = _pick_tile(T, (512, 256, 128))
    ntx = T // tmx
    ob_spec = pl.BlockSpec((tmx, BR_WIDTH), lambda r, i: (r % ntx, r // ntx))
    hm = _mix(xn, o_a, o_b.reshape(T, bsz * BR_WIDTH), ob_spec, o_c, o_d, lw["wg"], lw["w_br"], lw["w_o"], tm=tmx)
    y = _mlp(x2d, hm, lw["g_mlp"], lw["w_up"], lw["w_down"], tm=tmx)
    wk = min(C_WINDOW, T)
    return (y.reshape(bsz, T, D_MODEL), jnp.swapaxes(arow_t, 1, 2), crow3, wrow3[:, T - wk:],
            hr.reshape(bsz, B_GROUPS, B_STATE), hi.reshape(bsz, B_GROUPS, B_STATE), da3[:, T - (D_CONV - 1):])


def _sample_layer(x, l, cache_a, cache_c, cache_c_win, h_re, h_im, conv_l, page_table, lw, sp, consts):
    bsz = x.shape[0]
    cache_a_t = jnp.swapaxes(cache_a, 2, 3)
    x2d = x.reshape(bsz, D_MODEL)
    row = lambda w: pl.BlockSpec((bsz, w), lambda i: (0, 0))
    (xn, aq, arow, aiq, aiw, bu, cq, crow, wrow, cg, da) = _project(
        x2d, consts["tabs_s"], lw, tm=bsz, n_pos_blocks=1, bu_shape=(bsz, BR_WIDTH), bu_spec=row(BR_WIDTH))
    q8 = jnp.pad(aq.astype(F32).reshape(bsz, A_HEADS, HEAD_DIM), ((0, 0), (0, SUBLANES - A_HEADS), (0, 0)))
    iq8 = jnp.sum(aiq.astype(F32).reshape(bsz, A_IDX_HEADS, 2, A_IDX_DIM), axis=2)
    iq8 = jnp.pad(iq8, ((0, 0), (0, 0), (LANES - A_IDX_DIM, 0)))
    w8 = jnp.broadcast_to(aiw[:, :A_IDX_HEADS, None], (bsz, A_IDX_HEADS, LANES))
    o_a = _dsa_sample(page_table, cache_a_t, l, q8, iq8, w8, _pad_rows(arow, SUBLANES), consts["tri"], consts["tril"])
    o_a = o_a[:, :A_HEADS].reshape(bsz, BR_WIDTH).astype(BF16)
    cq8 = jnp.pad(cq.astype(F32).reshape(bsz, C_HEADS, HEAD_DIM), ((0, 0), (0, SUBLANES - C_HEADS), (0, 0)))
    g3 = jnp.transpose(cg[:, :3 * C_HEADS].reshape(bsz, C_HEADS, 3), (0, 2, 1))
    g3 = jnp.broadcast_to(jnp.pad(g3, ((0, 0), (0, 0), (0, SUBLANES - C_HEADS)))[..., None], (bsz, 3, SUBLANES, LANES))
    o_c = _nsa_sample(page_table, cache_c, l, cq8, g3, _pad_rows(crow, SUBLANES), cache_c_win,
                      _pad_rows(wrow, SUBLANES), lw["cmp_a"], lw["cmp_w"], consts["ov_s"])
    o_c = o_c[:, :C_HEADS].reshape(bsz, BR_WIDTH).astype(BF16)
    o_b, hr, hi, o_d = _sample_step(bu, h_re.reshape(bsz, B_LANES), h_im.reshape(bsz, B_LANES), sp, da, conv_l,
                                    lw["conv_w"], lw["conv_b"], lw["ln_g"], lw["ln_b"])
    hm = _mix(xn, o_a, o_b, pl.BlockSpec((bsz, BR_WIDTH), lambda r, i: (r, 0)), o_c, o_d,
              lw["wg"], lw["w_br"], lw["w_o"], tm=bsz)
    y = _mlp(x2d, hm, lw["g_mlp"], lw["w_up"], lw["w_down"], tm=bsz)
    new_win = jnp.concatenate([cache_c_win[l][:, 1:], wrow[:, None, :]], axis=1)
    new_conv = jnp.concatenate([conv_l[:, 1:], da[:, None, :]], axis=1)
    return (y.reshape(bsz, 1, D_MODEL), arow[:, None, :], crow[:, None, :], new_win,
            hr.reshape(bsz, B_GROUPS, B_STATE), hi.reshape(bsz, B_GROUPS, B_STATE), new_conv)


def kernel(x_prompt, x_sample, cache_a, cache_c, cache_c_win, state_b_re, state_b_im, state_d_conv, page_table, norm_mix, w_in, a_gq, a_gk, b_lam_re, b_lam_im, b_log_dt, b_b_re, b_b_im, b_c_re, b_c_im, b_d, b_glu_w, b_glu_b, c_gq, c_gk, c_cmp_a, c_cmp_w, d_conv_w, d_conv_b, d_ln_g, d_ln_b, w_br, w_o, norm_mlp, w_up, w_down):
    depth = w_in.shape[0]
    bsz, T, _ = x_prompt.shape
    sb, st, _ = x_sample.shape
    assert st == 1 and bsz == SUBLANES
    assert cache_a.shape[2] == PAGE and cache_c.shape[2] == PAGE
    consts = _make_consts(T, sb, page_table.shape[1] * PAGE)
    xp, xs = x_prompt, x_sample
    order_p = (0, 2, 4, 6, 7, 10)
    order_s = (1, 3, 5, 8, 9, 11)
    outs = [[] for _ in range(12)]
    for l in range(depth):
        lw = _layer_weights(l, norm_mix, w_in, a_gq, a_gk, c_gq, c_gk, c_cmp_a, c_cmp_w, d_conv_w, d_conv_b,
                            d_ln_g, d_ln_b, w_br, w_o, norm_mlp, w_up, w_down)
        sp = _s5_params(b_lam_re[l], b_lam_im[l], b_log_dt[l], b_b_re[l], b_b_im[l], b_c_re[l], b_c_im[l],
                        b_d[l], b_glu_w[l], b_glu_b[l])
        xp, *rp = _prompt_layer(xp, lw, sp, consts)
        xs, *rs = _sample_layer(xs, l, cache_a, cache_c, cache_c_win, state_b_re[l], state_b_im[l], state_d_conv[l],
                                page_table, lw, sp, consts)
        for k in range(6):
            outs[order_p[k]].append(rp[k])
            outs[order_s[k]].append(rs[k])
    return (xp, xs) + tuple(jnp.stack(o) for o in outs)
```

```python
import functools
import math

import numpy as np
import jax
import jax.numpy as jnp
from jax import lax
from jax.experimental import pallas as pl
from jax.experimental.pallas import tpu as pltpu

F32 = jnp.float32
BF16 = jnp.bfloat16
I32 = jnp.int32

D_MODEL = 2048
HEAD_DIM = 128
N_BRANCH = 4
BR_WIDTH = D_MODEL // N_BRANCH
ROPE_THETA = 10000.0
NORM_EPS = 1e-6
A_HEADS = BR_WIDTH // HEAD_DIM
A_IDX_HEADS = 8
A_IDX_DIM = 64
A_TOPK = 256
B_GROUP = 16
B_GROUPS = BR_WIDTH // B_GROUP
B_STATE = 64
B_LANES = B_GROUPS * B_STATE
C_HEADS = BR_WIDTH // HEAD_DIM
C_CMP_STRIDE = 16
C_CMP_LEN = 2 * C_CMP_STRIDE
C_SLC_BLOCK = 64
C_TOPN = 16
C_WINDOW = 512
C_FORCE = 1e4
D_CONV = 31
D_FF = 4 * D_MODEL
A_CACHE_DIM = 2 * HEAD_DIM + A_IDX_DIM
C_CACHE_DIM = 4 * HEAD_DIM
C_WIN_DIM = 2 * HEAD_DIM
PAGE = 128

LANES = 128
SUBLANES = 8
VMEM_LIMIT_MB = 56

_W = (A_HEADS * HEAD_DIM, HEAD_DIM, HEAD_DIM, A_IDX_HEADS * A_IDX_DIM, A_IDX_DIM, A_IDX_HEADS,
      BR_WIDTH, C_HEADS * HEAD_DIM, 6 * HEAD_DIM, 3 * C_HEADS, 2 * BR_WIDTH, N_BRANCH * D_MODEL)
_O = tuple(int(v) for v in np.cumsum((0,) + _W))
_P = {}
_cur = 0
for _name, _w in (("aq", 512), ("ak", 128), ("av", 128), ("aiq", 512), ("aik", 128), ("aiw", 128),
                  ("bu", 512), ("cq", 512), ("ckv", 768), ("cg", 128), ("dglu", 1024)):
    _P[_name] = (_cur, _w)
    _cur += _w
P_TOTAL = _cur

NEG = -1e30
ATT_SCALE = HEAD_DIM ** -0.5
SIGN = -2 ** 31


def _cparams(n_axes):
    return pltpu.CompilerParams(dimension_semantics=("arbitrary",) * n_axes,
                                vmem_limit_bytes=VMEM_LIMIT_MB * 1024 * 1024)


def _full(a, n_grid):
    nd = a.ndim
    return pl.BlockSpec(a.shape, lambda *_: (0,) * nd)


def _dot(a, b):
    return jnp.dot(a, b, preferred_element_type=F32)


def _dot_nt(a, b):
    return lax.dot_general(a, b, (((1,), (1,)), ((), ())), preferred_element_type=F32)


def _dot_split3(p, m):
    hi = p.astype(BF16)
    r = p - hi.astype(F32)
    mid = r.astype(BF16)
    lo = (r - mid.astype(F32)).astype(BF16)
    return _dot(hi, m) + _dot(mid, m) + _dot(lo, m)


def _rms(x, g):
    return x * lax.rsqrt(jnp.mean(x * x, axis=-1, keepdims=True) + NORM_EPS) * g


def _sigmoid(x):
    return 1.0 / (1.0 + jnp.exp(-x))


def _gelu(x):
    return x * (0.5 * (1.0 + jnp.tanh(math.sqrt(2.0 / math.pi) * (x + 0.044715 * (x * x * x)))))


def _sortable(x):
    b = pltpu.bitcast(x + 0.0, I32)
    return jnp.where(b < 0, b ^ jnp.int32(0x7FFFFFFF), b)


def _kth_key(key_ref, k, red_axes):
    shp = tuple(1 if a in red_axes else s for a, s in enumerate(key_ref.shape))

    def count(mask):
        c = jnp.where(mask, 1.0, 0.0)
        for a in sorted(red_axes):
            c = jnp.sum(c, axis=a, keepdims=True)
        return c

    def body(it, tu):
        cand_u = tu | jnp.left_shift(jnp.int32(1), 31 - it)
        cand_s = cand_u ^ jnp.int32(SIGN)
        return jnp.where(count(key_ref[...] >= cand_s) >= k, cand_u, tu)

    tu = lax.fori_loop(0, 32, body, jnp.zeros(shp, I32), unroll=4)
    return tu ^ jnp.int32(SIGN), count


def _select_rows(key_ref, k, tri):
    ts, count = _kth_key(key_ref, k, (1,))
    keys = key_ref[...]
    gt = keys > ts
    need = k - count(gt)
    eqf = jnp.where(keys == ts, 1.0, 0.0)
    base = jnp.zeros_like(need)
    pieces = []
    for c in range(keys.shape[1] // LANES):
        ch = eqf[:, c * LANES:(c + 1) * LANES]
        pref = _dot(ch.astype(BF16), tri) + base
        pieces.append(jnp.where(pref < need, ch, 0.0))
        base = base + jnp.sum(ch, axis=-1, keepdims=True)
    sel_eq = pieces[0] if len(pieces) == 1 else jnp.concatenate(pieces, axis=1)
    return jnp.where(gt, 1.0, sel_eq)


def _select_packed(key_ref, k, tri, tril):
    shp = key_ref.shape
    ts, count = _kth_key(key_ref, k, (0, 1, 2))
    keys = key_ref[...]
    gt = keys > ts
    need = k - count(gt)
    eqf = jnp.where(keys == ts, 1.0, 0.0)
    eq2 = eqf.reshape(LANES, LANES)
    within = _dot(eq2.astype(BF16), tri)
    tot = jnp.broadcast_to(jnp.sum(eq2, axis=1, keepdims=True), (LANES, LANES))
    base = _dot(tril, tot.astype(BF16))
    sel_eq = jnp.where((within + base).reshape(shp) < need, eqf, 0.0)
    return jnp.where(gt, 1.0, sel_eq)


def _select_rank(sc, n, k):
    lane = lax.broadcasted_iota(I32, sc.shape, 1)
    rank = jnp.zeros(sc.shape, F32)
    for i in range(n):
        col = sc[:, i:i + 1]
        rank = rank + jnp.where(lane > i, jnp.where(col >= sc, 1.0, 0.0), jnp.where(col > sc, 1.0, 0.0))
    return jnp.where(rank < k, 1.0, 0.0)


LOG2E = 1.4426950408889634


def _masked_attn(qh, k, v, maskf):
    s = jnp.where(maskf > 0.5, _dot_nt(qh, k) * (ATT_SCALE * LOG2E), NEG)
    e = jnp.exp2(s - jnp.max(s, axis=-1, keepdims=True)) * maskf
    den = jnp.maximum(jnp.sum(e, axis=-1, keepdims=True), 1e-30)
    return _dot(e.astype(BF16), v) / den


def _proj_kernel(x_ref, g_ref, w_ref, c128_ref, s128_ref, c64_ref, s64_ref, gqa_ref, gka_ref, gqc_ref, gkc_ref,
                 xn_ref, aq_ref, arow_ref, aiq_ref, aiw_ref, bu_ref, cq_ref, crow_ref, wrow_ref, cg_ref, da_ref,
                 arow_t_ref=None):
    xn = _rms(x_ref[...], g_ref[...]).astype(BF16)
    xn_ref[...] = xn
    cos, sin = c128_ref[...], s128_ref[...]
    cos64, sin64 = c64_ref[...], s64_ref[...]
    lane = lax.broadcasted_iota(I32, cos.shape, 1)
    lo32 = (lane & 63) < 32
    lo64 = lane < 64

    def seg(name):
        a, w = _P[name]
        return _dot_nt(xn, w_ref[a:a + w, :])

    def rope128(v):
        return v * cos + pltpu.roll(v, 64, 1) * sin

    def rope64(v):
        rot = jnp.where(lo32, pltpu.roll(v, 96, 1), pltpu.roll(v, 32, 1))
        return v * cos64 + rot * sin64

    z = seg("aq")
    for h in range(A_HEADS):
        sl = slice(h * HEAD_DIM, (h + 1) * HEAD_DIM)
        aq_ref[:, sl] = rope128(_rms(z[:, sl], gqa_ref[...])).astype(BF16)
    ak = rope128(_rms(seg("ak"), gka_ref[...]))
    av = seg("av")
    aik = rope64(seg("aik"))
    arow_ref[:, 0:128] = ak
    arow_ref[:, 128:256] = av
    arow_ref[:, 256:320] = aik[:, 0:A_IDX_DIM]
    if arow_t_ref is not None:
        arow_t_ref[0:128, :] = ak.T
        arow_t_ref[128:256, :] = av.T
        arow_t_ref[256:320, :] = aik.T[0:A_IDX_DIM, :]
    z = seg("aiq")
    for j in range(A_IDX_HEADS // 2):
        r = rope64(z[:, j * LANES:(j + 1) * LANES])
        aiq_ref[:, (2 * j) * LANES:(2 * j + 1) * LANES] = jnp.where(lo64, r, 0.0).astype(BF16)
        aiq_ref[:, (2 * j + 1) * LANES:(2 * j + 2) * LANES] = jnp.where(lo64, 0.0, r).astype(BF16)
    aiw_ref[...] = seg("aiw") * (A_IDX_HEADS ** -0.5) * (A_IDX_DIM ** -0.5)
    bu_ref[...] = seg("bu")
    z = seg("cq")
    for h in range(C_HEADS):
        sl = slice(h * HEAD_DIM, (h + 1) * HEAD_DIM)
        cq_ref[:, sl] = rope128(_rms(z[:, sl], gqc_ref[...])).astype(BF16)
    z = seg("ckv")
    for br in range(3):
        kk = rope128(_rms(z[:, (2 * br) * LANES:(2 * br + 1) * LANES], gkc_ref[br:br + 1, :]))
        vv = z[:, (2 * br + 1) * LANES:(2 * br + 2) * LANES]
        if br < 2:
            crow_ref[:, (2 * br) * LANES:(2 * br + 1) * LANES] = kk
            crow_ref[:, (2 * br + 1) * LANES:(2 * br + 2) * LANES] = vv
        else:
            wrow_ref[:, 0:LANES] = kk
            wrow_ref[:, LANES:2 * LANES] = vv
    cg_ref[...] = _sigmoid(seg("cg"))
    z = seg("dglu")
    da_ref[...] = z[:, 0:BR_WIDTH] * _sigmoid(z[:, BR_WIDTH:2 * BR_WIDTH])


def _project(x2d, tabs, lw, *, tm, n_pos_blocks, bu_shape, bu_spec, a_rows_t=None):
    n = x2d.shape[0]
    row = lambda w: pl.BlockSpec((tm, w), lambda i: (i, 0))
    tab = pl.BlockSpec((tm, LANES), lambda i: (i % n_pos_blocks, 0))
    ins = [x2d, lw["g_mix"], lw["w1"], tabs[0], tabs[1], tabs[2], tabs[3], lw["a_gq"], lw["a_gk"], lw["c_gq"], lw["c_gk"]]
    in_specs = [row(D_MODEL), _full(ins[1], 1), _full(ins[2], 1), tab, tab, tab, tab] + [_full(a, 1) for a in ins[7:]]
    outs = [((n, D_MODEL), BF16, row(D_MODEL)),
            ((n, 512), BF16, row(512)),
            ((n, A_CACHE_DIM), F32, row(A_CACHE_DIM)),
            ((n, 1024), BF16, row(1024)),
            ((n, LANES), F32, row(LANES)),
            (bu_shape, F32, bu_spec),
            ((n, 512), BF16, row(512)),
            ((n, C_CACHE_DIM), F32, row(C_CACHE_DIM)),
            ((n, C_WIN_DIM), F32, row(C_WIN_DIM)),
            ((n, LANES), F32, row(LANES)),
            ((n, BR_WIDTH), F32, row(BR_WIDTH))]
    if a_rows_t is not None:
        outs.append((a_rows_t[0], F32, a_rows_t[1]))
    return pl.pallas_call(
        _proj_kernel,
        grid=(n // tm,),
        in_specs=in_specs,
        out_specs=[o[2] for o in outs],
        out_shape=[jax.ShapeDtypeStruct(o[0], o[1]) for o in outs],
        compiler_params=_cparams(1),
        name="project",
    )(*ins)


def _dsa_kernel(aq_ref, aiq_ref, aiw_ref, arow_ref, tri_ref, o_ref, k_sc, v_sc, ik_sc, key_sc, *, qb, L, n_sel):
    qi = pl.program_id(1)

    @pl.when(qi == 0)
    def _():
        k_sc[...] = arow_ref[:, 0:128].astype(BF16)
        v_sc[...] = arow_ref[:, 128:256].astype(BF16)
        ik = arow_ref[:, 256:320]
        ik_sc[...] = jnp.concatenate([ik, ik], axis=1).astype(BF16)

    def body(le):
        qpos = qi * qb + lax.broadcasted_iota(I32, (qb, 1), 0)
        valid = lax.broadcasted_iota(I32, (qb, le), 1) <= qpos
        if le <= n_sel:
            mask = jnp.where(valid, 1.0, 0.0)
        else:
            w = aiw_ref[...]
            score = None
            for h in range(A_IDX_HEADS):
                lg = _dot_nt(aiq_ref[:, h * LANES:(h + 1) * LANES], ik_sc[0:le, :])
                t = jnp.maximum(lg, 0.0) * w[:, h:h + 1]
                score = t if score is None else score + t
            keys = key_sc.at[:, 0:le]
            keys[...] = _sortable(jnp.where(valid, score, -jnp.inf))
            mask = jnp.where(valid, _select_rows(keys, n_sel, tri_ref[...]), 0.0)
        q = aq_ref[...]
        for h in range(A_HEADS):
            sl = slice(h * HEAD_DIM, (h + 1) * HEAD_DIM)
            o_ref[:, sl] = _masked_attn(q[:, sl], k_sc[0:le, :], v_sc[0:le, :], mask).astype(BF16)

    _causal_branches(qi, qb, L, body)


def _causal_branches(qi, qb, L, body):
    step = max(qb, L // 8)
    if L % step:
        step = qb
    per = step // qb
    for j in range(L // step):
        @pl.when((qi >= j * per) & (qi < (j + 1) * per))
        def _(j=j):
            body((j + 1) * step)


def _dsa(aq, aiq, aiw, arow3, tri, *, qb):
    bsz, L, _ = arow3.shape
    nq = L // qb
    n_sel = min(A_TOPK, L // 4)
    row = lambda w: pl.BlockSpec((qb, w), lambda b, i: (b * nq + i, 0))
    return pl.pallas_call(
        functools.partial(_dsa_kernel, qb=qb, L=L, n_sel=n_sel),
        grid=(bsz, nq),
        in_specs=[row(512), row(1024), row(LANES),
                  pl.BlockSpec((None, L, A_CACHE_DIM), lambda b, i: (b, 0, 0)), _full(tri, 2)],
        out_specs=row(512),
        out_shape=jax.ShapeDtypeStruct((bsz * L, 512), BF16),
        scratch_shapes=[pltpu.VMEM((L, 128), BF16), pltpu.VMEM((L, 128), BF16), pltpu.VMEM((L, 128), BF16),
                        pltpu.VMEM((qb, L), I32)],
        compiler_params=_cparams(2),
        name="dsa",
    )(aq, aiq, aiw, arow3, tri)


def _summaries(x, a, w, n_cmp):
    ns = x.shape[0] // C_CMP_STRIDE
    x3 = x.reshape(ns, C_CMP_STRIDE, HEAD_DIM)
    lo = jnp.sum(x3 * a[0:C_CMP_STRIDE][None], axis=1)
    hi = jnp.sum(x3 * a[C_CMP_STRIDE:C_CMP_LEN][None], axis=1)
    return lo, hi


def _finish_summaries(lo, hi, w, n_cmp):
    ns = lo.shape[0]
    comb = lo + pltpu.roll(hi, ns - 1, 0)
    comb = jnp.where(lax.broadcasted_iota(I32, comb.shape, 0) < n_cmp, comb, 0.0)
    return _dot(comb.astype(BF16), w).astype(BF16)


def _cmp_kernel(x_ref, a_ref, w_ref, kc_ref, vc_ref, *, n_cmp):
    for t, out in ((0, kc_ref), (1, vc_ref)):
        lo, hi = _summaries(x_ref[:, t * LANES:(t + 1) * LANES], a_ref[t], w_ref[t], n_cmp)
        out[...] = _finish_summaries(lo, hi, w_ref[t], n_cmp)


def _compress(crow3, cmp_a, cmp_w):
    bsz, L, _ = crow3.shape
    ns = L // C_CMP_STRIDE
    n_cmp = (L - C_CMP_LEN) // C_CMP_STRIDE + 1
    out = pl.BlockSpec((None, ns, HEAD_DIM), lambda b: (b, 0, 0))
    return pl.pallas_call(
        functools.partial(_cmp_kernel, n_cmp=n_cmp),
        grid=(bsz,),
        in_specs=[pl.BlockSpec((None, L, C_CACHE_DIM), lambda b: (b, 0, 0)), _full(cmp_a, 1), _full(cmp_w, 1)],
        out_specs=[out, out],
        out_shape=[jax.ShapeDtypeStruct((bsz, ns, HEAD_DIM), BF16)] * 2,
        compiler_params=_cparams(1),
        name="compress",
    )(crow3, cmp_a, cmp_w)


def _block_scores(imp, qpos):
    j = lax.broadcasted_iota(I32, imp.shape, 1)
    cur = lax.shift_right_logical(qpos, 6)
    forced = (j == 0) | (j == cur) | (j == cur - 1)
    return jnp.where(j <= cur, jnp.where(forced, C_FORCE, imp), -jnp.inf)


def _cmp_softmax(qh, kc, cvalid):
    s = _dot_nt(qh, kc) * ATT_SCALE
    s = jnp.where(cvalid, s, NEG)
    m = jnp.max(s, axis=-1, keepdims=True)
    e = jnp.where(cvalid, jnp.exp(s - m), 0.0)
    return e / jnp.maximum(jnp.sum(e, axis=-1, keepdims=True), 1e-30)


def _nsa_kernel(cq_ref, cg_ref, crow_ref, wrow_ref, kc_ref, vc_ref, ov_ref, ex_ref, o_ref,
                ks_sc, vs_sc, kw_sc, vw_sc, os_sc, *, qb, L, n_cmp, n_blk, n_top, wsl):
    qi = pl.program_id(1)

    @pl.when(qi == 0)
    def _():
        ks_sc[...] = crow_ref[:, 256:384].astype(BF16)
        vs_sc[...] = crow_ref[:, 384:512].astype(BF16)
        kw_sc[...] = wrow_ref[:, 0:128].astype(BF16)
        vw_sc[...] = wrow_ref[:, 128:256].astype(BF16)

    q0 = qi * qb
    qpos = q0 + lax.broadcasted_iota(I32, (qb, 1), 0)
    q = cq_ref[...]
    g = cg_ref[...]
    kc, vc = kc_ref[...], vc_ref[...]
    ncp = kc.shape[0]
    n_io = lax.broadcasted_iota(I32, (qb, ncp), 1)
    cvalid = (n_io < n_cmp) & (n_io * C_CMP_STRIDE + (C_CMP_LEN - 1) <= qpos)
    o_c, psum = [], None
    for h in range(C_HEADS):
        p = _cmp_softmax(q[:, h * HEAD_DIM:(h + 1) * HEAD_DIM], kc, cvalid)
        o_c.append(_dot(p.astype(BF16), vc))
        psum = p if psum is None else psum + p
    imp = _dot_split3(psum, ov_ref[...])
    sel = _select_rank(_block_scores(imp, qpos), n_blk, n_top).astype(BF16)

    def selected(le):
        selk = _dot(sel, ex_ref[:, 0:le])
        smask = jnp.where(lax.broadcasted_iota(I32, (qb, le), 1) <= qpos, selk, 0.0)
        for h in range(C_HEADS):
            sl = slice(h * HEAD_DIM, (h + 1) * HEAD_DIM)
            os_sc[:, sl] = _masked_attn(q[:, sl], ks_sc[0:le, :], vs_sc[0:le, :], smask)

    _causal_branches(qi, qb, L, selected)
    start = pl.multiple_of(jnp.minimum(jnp.maximum(q0 - C_WINDOW, 0), L - wsl), qb)
    dist = qpos - (start + lax.broadcasted_iota(I32, (qb, wsl), 1))
    wmask = jnp.where((dist >= 0) & (dist <= C_WINDOW), 1.0, 0.0)
    kw = kw_sc[pl.ds(start, wsl), :]
    vw = vw_sc[pl.ds(start, wsl), :]
    for h in range(C_HEADS):
        sl = slice(h * HEAD_DIM, (h + 1) * HEAD_DIM)
        o_w = _masked_attn(q[:, sl], kw, vw, wmask)
        out = g[:, 3 * h:3 * h + 1] * o_c[h] + g[:, 3 * h + 1:3 * h + 2] * os_sc[:, sl] + g[:, 3 * h + 2:3 * h + 3] * o_w
        o_ref[:, sl] = out.astype(BF16)


def _nsa(cq, cg, crow3, wrow3, kcmp, vcmp, ov, ex, *, qb):
    bsz, L, _ = crow3.shape
    nq = L // qb
    ns = kcmp.shape[1]
    n_cmp = (L - C_CMP_LEN) // C_CMP_STRIDE + 1
    n_blk = -(-L // C_SLC_BLOCK)
    wsl = min(L, C_WINDOW + qb)
    row = lambda w: pl.BlockSpec((qb, w), lambda b, i: (b * nq + i, 0))
    per_b = lambda r, w: pl.BlockSpec((None, r, w), lambda b, i: (b, 0, 0))
    return pl.pallas_call(
        functools.partial(_nsa_kernel, qb=qb, L=L, n_cmp=n_cmp, n_blk=n_blk, n_top=min(C_TOPN, n_blk), wsl=wsl),
        grid=(bsz, nq),
        in_specs=[row(512), row(LANES), per_b(L, C_CACHE_DIM), per_b(L, C_WIN_DIM),
                  per_b(ns, HEAD_DIM), per_b(ns, HEAD_DIM), _full(ov, 2), _full(ex, 2)],
        out_specs=row(512),
        out_shape=jax.ShapeDtypeStruct((bsz * L, 512), BF16),
        scratch_shapes=[pltpu.VMEM((L, 128), BF16)] * 4 + [pltpu.VMEM((qb, 512), F32)],
        compiler_params=_cparams(2),
        name="nsa",
    )(cq, cg, crow3, wrow3, kcmp, vcmp, ov, ex)


S5_BLOCKS = 4


def _s5_kernel(u_ref, bdr_ref, bdi_ref, ar_ref, ai_ref, cdr_ref, cdi_ref, d_ref, gw_ref, gb_ref,
               o_ref, hr_ref, hi_ref, xr_sc, xi_sc, h_sc, *, tc, nb):
    i = pl.program_id(0)

    @pl.when(i == 0)
    def _():
        h_sc[...] = jnp.zeros_like(h_sc)

    u = u_ref[...]
    ub = u.astype(BF16)
    nsb = bdr_ref.shape[0]
    wu, wx = BR_WIDTH // nsb, B_LANES // nsb
    for sb in range(nsb):
        us = ub[:, sb * wu:(sb + 1) * wu]
        xr_sc[:, sb * wx:(sb + 1) * wx] = _dot(us, bdr_ref[sb])
        xi_sc[:, sb * wx:(sb + 1) * wx] = _dot(us, bdi_ref[sb])
    ar = jnp.broadcast_to(ar_ref[...], (nb, B_LANES))
    ai = jnp.broadcast_to(ai_ref[...], (nb, B_LANES))

    def step(t, carry):
        hr, hi = carry
        r0 = pl.multiple_of(t * nb, nb)
        nhr = ar * hr - ai * hi + xr_sc[pl.ds(r0, nb), :]
        nhi = ar * hi + ai * hr + xi_sc[pl.ds(r0, nb), :]
        xr_sc[pl.ds(r0, nb), :] = nhr
        xi_sc[pl.ds(r0, nb), :] = nhi
        return nhr, nhi

    hr, hi = lax.fori_loop(0, tc, step, (h_sc[0], h_sc[1]))
    h_sc[0] = hr
    h_sc[1] = hi
    hr_ref[...] = hr
    hi_ref[...] = hi
    ch = [_dot(xr_sc[:, sb * wx:(sb + 1) * wx].astype(BF16), cdr_ref[sb])
          - _dot(xi_sc[:, sb * wx:(sb + 1) * wx].astype(BF16), cdi_ref[sb]) for sb in range(nsb)]
    y = jnp.concatenate(ch, axis=1) + d_ref[...] * u
    gl = _gelu(y)
    o_ref[...] = (gl * _sigmoid(_dot(gl.astype(BF16), gw_ref[...]) + gb_ref[...])).astype(BF16)


def _s5(u_tm, sp, *, nb, tc):
    rows = u_tm.shape[0]
    r = tc * nb
    consts = [sp["bdr4"], sp["bdi4"], sp["ar"], sp["ai"], sp["cdr4"], sp["cdi4"], sp["d"], sp["glu_w"], sp["glu_b"]]
    st = pl.BlockSpec((nb, B_LANES), lambda i: (0, 0))
    return pl.pallas_call(
        functools.partial(_s5_kernel, tc=tc, nb=nb),
        grid=(rows // r,),
        in_specs=[pl.BlockSpec((r, BR_WIDTH), lambda i: (i, 0))] + [_full(c, 1) for c in consts],
        out_specs=[pl.BlockSpec((r, BR_WIDTH), lambda i: (i, 0)), st, st],
        out_shape=[jax.ShapeDtypeStruct((rows, BR_WIDTH), BF16),
                   jax.ShapeDtypeStruct((nb, B_LANES), F32), jax.ShapeDtypeStruct((nb, B_LANES), F32)],
        scratch_shapes=[pltpu.VMEM((r, B_LANES), F32), pltpu.VMEM((r, B_LANES), F32), pltpu.VMEM((2, nb, B_LANES), F32)],
        compiler_params=_cparams(1),
        name="s5",
    )(u_tm, *consts)


HALO = 32


def _ln_swish(y, g, b):
    yc = y - jnp.mean(y, axis=-1, keepdims=True)
    yn = yc * lax.rsqrt(jnp.mean(yc * yc, axis=-1, keepdims=True) + NORM_EPS) * g + b
    return yn * _sigmoid(yn)


def _conv_kernel(cur_ref, halo_ref, w_ref, b_ref, lg_ref, lb_ref, o_ref, ext_sc, *, tm):
    i = pl.program_id(1)
    ext_sc[0, 0:HALO, :] = jnp.where(i == 0, 0.0, halo_ref[...])
    ext_sc[0, HALO:HALO + tm, :] = cur_ref[...]
    n = HALO + tm - SUBLANES
    for k in range(1, SUBLANES):
        ext_sc[k, 0:n, :] = ext_sc[0, pl.ds(k, n), :]
    acc = jnp.zeros((tm, BR_WIDTH), F32)
    for j in range(D_CONV):
        off = HALO - (D_CONV - 1) + j
        k = off % SUBLANES
        acc = acc + w_ref[j:j + 1, :] * ext_sc[k, off - k:off - k + tm, :]
    o_ref[...] = _ln_swish(acc + b_ref[...], lg_ref[...], lb_ref[...]).astype(BF16)


def _conv(da3, cw, cb, lg, lb, *, tm):
    bsz, T, _ = da3.shape
    nt = T // tm
    hb = tm // HALO
    consts = [cw, cb, lg, lb]
    return pl.pallas_call(
        functools.partial(_conv_kernel, tm=tm),
        grid=(bsz, nt),
        in_specs=[pl.BlockSpec((None, tm, BR_WIDTH), lambda b, i: (b, i, 0)),
                  pl.BlockSpec((None, HALO, BR_WIDTH), lambda b, i: (b, jnp.maximum(i * hb - 1, 0), 0))]
        + [_full(c, 2) for c in consts],
        out_specs=pl.BlockSpec((tm, BR_WIDTH), lambda b, i: (b * nt + i, 0)),
        out_shape=jax.ShapeDtypeStruct((bsz * T, BR_WIDTH), BF16),
        scratch_shapes=[pltpu.VMEM((SUBLANES, HALO + tm, BR_WIDTH), F32)],
        compiler_params=_cparams(2),
        name="conv",
    )(da3, da3, *consts)


MIX_CW = 512


def _mix_kernel(xn_ref, oa_ref, ob_ref, oc_ref, od_ref, wg_ref, wbr_ref, wo_ref, hm_ref, acc_sc):
    i = pl.program_id(1)
    xn = xn_ref[...]
    br = jnp.where(i == 0, oa_ref[...], jnp.where(i == 1, ob_ref[...], jnp.where(i == 2, oc_ref[...], od_ref[...])))
    for c in range(D_MODEL // MIX_CW):
        sl = slice(c * MIX_CW, (c + 1) * MIX_CW)
        contrib = _sigmoid(_dot_nt(xn, wg_ref[sl, :])) * _dot(br, wbr_ref[:, sl])

        @pl.when(i == 0)
        def _():
            acc_sc[:, sl] = contrib

        @pl.when(i > 0)
        def _():
            acc_sc[:, sl] = acc_sc[:, sl] + contrib

    @pl.when(i == N_BRANCH - 1)
    def _():
        hm_ref[...] = _dot(acc_sc[...].astype(BF16), wo_ref[...])


def _mix(xn, oa, ob, ob_spec, oc, od, wg, wbr, wo, *, tm):
    n = xn.shape[0]
    row = lambda w: pl.BlockSpec((tm, w), lambda r, i: (r, 0))
    return pl.pallas_call(
        _mix_kernel,
        grid=(n // tm, N_BRANCH),
        in_specs=[row(D_MODEL), row(BR_WIDTH), ob_spec, row(BR_WIDTH), row(BR_WIDTH),
                  pl.BlockSpec((D_MODEL, D_MODEL), lambda r, i: (i, 0)),
                  pl.BlockSpec((None, BR_WIDTH, D_MODEL), lambda r, i: (i, 0, 0)),
                  pl.BlockSpec((D_MODEL, D_MODEL), lambda r, i: (0, 0))],
        out_specs=row(D_MODEL),
        out_shape=jax.ShapeDtypeStruct((n, D_MODEL), F32),
        scratch_shapes=[pltpu.VMEM((tm, D_MODEL), F32)],
        compiler_params=_cparams(2),
        name="mix",
    )(xn, oa, ob, oc, od, wg, wbr, wo)


MLP_FC = 1024


def _mlp_kernel(x_ref, hm_ref, g_ref, wu_ref, wd_ref, y_ref, hn_sc, acc_sc):
    j = pl.program_id(1)

    @pl.when(j == 0)
    def _():
        hn_sc[...] = _rms(x_ref[...] + hm_ref[...], g_ref[...]).astype(BF16)

    up = _dot(hn_sc[...], wu_ref[...])
    act = jnp.square(jnp.maximum(up, 0.0)).astype(BF16)
    d = _dot(act, wd_ref[...])

    @pl.when(j == 0)
    def _():
        acc_sc[...] = d

    @pl.when(j > 0)
    def _():
        acc_sc[...] = acc_sc[...] + d

    @pl.when(j == pl.num_programs(1) - 1)
    def _():
        y_ref[...] = (x_ref[...] + hm_ref[...]) + acc_sc[...]


def _mlp(x2d, hm, g, wu, wd, *, tm):
    n = x2d.shape[0]
    row = pl.BlockSpec((tm, D_MODEL), lambda r, j: (r, 0))
    return pl.pallas_call(
        _mlp_kernel,
        grid=(n // tm, D_FF // MLP_FC),
        in_specs=[row, row, _full(g, 2),
                  pl.BlockSpec((D_MODEL, MLP_FC), lambda r, j: (0, j)),
                  pl.BlockSpec((MLP_FC, D_MODEL), lambda r, j: (j, 0))],
        out_specs=row,
        out_shape=jax.ShapeDtypeStruct((n, D_MODEL), F32),
        scratch_shapes=[pltpu.VMEM((tm, D_MODEL), BF16), pltpu.VMEM((tm, D_MODEL), F32)],
        compiler_params=_cparams(2),
        name="mlp",
    )(x2d, hm, g, wu, wd)


PAGES_PER_STEP = 16


def _tile_attention(s_ref, bias_ref, v_ref, n, s_new, bias_new, v_new, v_transposed):
    c2 = ATT_SCALE * LOG2E
    s = s_ref[...] * c2 + bias_ref[...]
    sn = s_new * c2 + bias_new
    m = jnp.maximum(jnp.max(jnp.max(s, axis=0), axis=1, keepdims=True), sn)
    e = jnp.exp2(s - m)
    en = jnp.exp2(sn - m)
    den = jnp.sum(jnp.sum(e, axis=0), axis=1, keepdims=True) + en
    acc = en.astype(BF16).astype(F32) * v_new.astype(BF16).astype(F32)
    mm = _dot_nt if v_transposed else _dot
    for c in range(n):
        acc = acc + mm(e[c].astype(BF16), v_ref[c * LANES:(c + 1) * LANES, :])
    return acc / den


KEY_TILES = 16


def _dsa_s_kernel(pt_ref, q_ref, iq_ref, w_ref, new_ref, *rest, pg, n_pages, n_sel):
    pages = rest[:pg]
    tri_ref, tril_ref, o_ref, s_sc, bias_sc, vt_sc, sc_sc, key_sc = rest[pg:]
    g = pl.program_id(1)
    q = q_ref[...].astype(BF16)
    iq = iq_ref[...].astype(BF16)
    w = w_ref[...]

    @pl.when(g == 0)
    def _():
        sc_sc[...] = jnp.full(sc_sc.shape, -jnp.inf, F32)

    for i in range(pg):
        c = g * pg + i
        page = pages[i]
        s_sc[c] = _dot(q, page[0:128, :].astype(BF16))
        vt_sc[pl.ds(pl.multiple_of(c * LANES, LANES), LANES), :] = page[128:256, :].astype(BF16)
        lg = _dot(iq, page[192:320, :].astype(BF16))
        sc_sc[g * (pg // SUBLANES) + i // SUBLANES, i % SUBLANES:i % SUBLANES + 1, :] = jnp.sum(
            jnp.maximum(lg, 0.0) * w, axis=0, keepdims=True)

    @pl.when(g == pl.num_programs(1) - 1)
    def _():
        new = new_ref[...]
        k_new = new[0:1, 0:128].astype(BF16).astype(F32)
        s_new = jnp.sum(q.astype(F32) * k_new, axis=-1, keepdims=True)
        ik_new = new[0:1, 256:320].astype(BF16).astype(F32)
        lg_new = jnp.sum(iq[:, A_IDX_DIM:2 * A_IDX_DIM].astype(F32) * ik_new, axis=-1, keepdims=True)
        sc_new = jnp.sum(jnp.maximum(lg_new, 0.0) * w[:, 0:1], axis=0, keepdims=True)
        t_new = n_pages // SUBLANES
        first = (lax.broadcasted_iota(I32, (SUBLANES, LANES), 0) == 0) & (lax.broadcasted_iota(I32, (SUBLANES, LANES), 1) == 0)
        sc_sc[t_new] = jnp.where(first, sc_new, -jnp.inf)
        key_sc[...] = _sortable(sc_sc[...])
        sel = _select_packed(key_sc, n_sel, tri_ref[...], tril_ref[...])
        for c in range(n_pages):
            row = sel[c // SUBLANES][c % SUBLANES:c % SUBLANES + 1, :]
            bias_sc[c] = jnp.where(jnp.broadcast_to(row, (SUBLANES, LANES)) > 0.5, 0.0, NEG)
        bias_new = jnp.where(sel[t_new][0:1, 0:1] > 0.5, 0.0, NEG)
        o_ref[...] = _tile_attention(s_sc, bias_sc, vt_sc, n_pages, s_new, bias_new, new[0:1, 128:256], True)


def _dsa_sample(page_table, cache_t, layer, q8, iq8, w8, new8, tri, tril):
    bsz, n_pages = page_table.shape
    pg = PAGES_PER_STEP
    assert pg % SUBLANES == 0 and n_pages % pg == 0 and n_pages * PAGE + 1 <= KEY_TILES * SUBLANES * LANES
    n_sel = min(A_TOPK, (n_pages * PAGE + 1) // 4)
    per_b = lambda r, w: pl.BlockSpec((None, r, w), lambda b, g, pt: (b, 0, 0))
    cst = lambda a: pl.BlockSpec(a.shape, lambda b, g, pt: (0,) * a.ndim)
    page_spec = lambda i: pl.BlockSpec((None, None, A_CACHE_DIM, PAGE),
                                       lambda b, g, pt: (layer, pt[b, g * pg + i], 0, 0))
    grid_spec = pltpu.PrefetchScalarGridSpec(
        num_scalar_prefetch=1,
        grid=(bsz, n_pages // pg),
        in_specs=[per_b(SUBLANES, LANES), per_b(SUBLANES, LANES), per_b(SUBLANES, LANES), per_b(SUBLANES, A_CACHE_DIM)]
        + [page_spec(i) for i in range(pg)] + [cst(tri), cst(tril)],
        out_specs=per_b(SUBLANES, HEAD_DIM),
        scratch_shapes=[pltpu.VMEM((n_pages, SUBLANES, LANES), F32), pltpu.VMEM((n_pages, SUBLANES, LANES), F32),
                        pltpu.VMEM((n_pages * LANES, PAGE), BF16),
                        pltpu.VMEM((KEY_TILES, SUBLANES, LANES), F32), pltpu.VMEM((KEY_TILES, SUBLANES, LANES), I32)],
    )
    return pl.pallas_call(
        functools.partial(_dsa_s_kernel, pg=pg, n_pages=n_pages, n_sel=n_sel),
        grid_spec=grid_spec,
        out_shape=jax.ShapeDtypeStruct((bsz, SUBLANES, HEAD_DIM), F32),
        compiler_params=_cparams(2),
        name="dsa_sample",
    )(page_table, q8, iq8, w8, new8, *([cache_t] * pg), tri, tril)


def _nsa_s_kernel(pt_ref, q_ref, g3_ref, new_ref, win_ref, wnew_ref, a_ref, w_ref, ov_ref, *rest,
                  pg, n_pages, n_cmp, n_blk, n_top, past):
    pages = rest[:pg]
    o_ref, ss_sc, bias_sc, vs_sc, lok_sc, hik_sc, lov_sc, hiv_sc = rest[pg:]
    g = pl.program_id(1)
    q = q_ref[...].astype(BF16)
    sub = PAGE // C_CMP_STRIDE

    for i in range(pg):
        c = g * pg + i
        page = pages[i]
        r0 = pl.multiple_of(c * sub, sub)
        lo, hi = _summaries(page[:, 0:128], a_ref[0], None, n_cmp)
        lok_sc[pl.ds(r0, sub), :] = lo
        hik_sc[pl.ds(r0, sub), :] = hi
        lo, hi = _summaries(page[:, 128:256], a_ref[1], None, n_cmp)
        lov_sc[pl.ds(r0, sub), :] = lo
        hiv_sc[pl.ds(r0, sub), :] = hi
        ss_sc[c] = _dot_nt(q, page[:, 256:384].astype(BF16))
        vs_sc[pl.ds(pl.multiple_of(c * LANES, LANES), LANES), :] = page[:, 384:512].astype(BF16)

    @pl.when(g == pl.num_programs(1) - 1)
    def _():
        qf = q.astype(F32)
        new = new_ref[...]
        lane = lax.broadcasted_iota(I32, (SUBLANES, LANES), 1)
        qpos = jnp.full((SUBLANES, 1), past, I32)
        kc = _finish_summaries(lok_sc[...], hik_sc[...], w_ref[0], n_cmp)
        vc = _finish_summaries(lov_sc[...], hiv_sc[...], w_ref[1], n_cmp)
        ncp = kc.shape[0]
        n_io = lax.broadcasted_iota(I32, (SUBLANES, ncp), 1)
        cvalid = (n_io < n_cmp) & (n_io * C_CMP_STRIDE + (C_CMP_LEN - 1) <= qpos)
        p = _cmp_softmax(q, kc, cvalid)
        o_c = _dot(p.astype(BF16), vc)
        head = lax.broadcasted_iota(I32, p.shape, 0) < C_HEADS
        psum = jnp.broadcast_to(jnp.sum(jnp.where(head, p, 0.0), axis=0, keepdims=True), p.shape)
        imp = _dot_split3(psum, ov_ref[...])
        sel = _select_rank(_block_scores(imp, qpos), n_blk, n_top)
        for c in range(n_pages):
            pick = jnp.where(lane < C_SLC_BLOCK, sel[:, 2 * c:2 * c + 1], sel[:, 2 * c + 1:2 * c + 2])
            bias_sc[c] = jnp.where(pick > 0.5, 0.0, NEG)
        k_new = new[0:1, 256:384].astype(BF16).astype(F32)
        s_new = jnp.sum(qf * k_new, axis=-1, keepdims=True)
        bias_new = jnp.where(sel[:, 2 * n_pages:2 * n_pages + 1] > 0.5, 0.0, NEG)
        o_s = _tile_attention(ss_sc, bias_sc, vs_sc, n_pages, s_new, bias_new, new[0:1, 384:512], False)
        wb = win_ref.shape[0]
        kw = win_ref[:, 0:128].astype(BF16)
        vw = win_ref[:, 128:256].astype(BF16)
        wnew = wnew_ref[...]
        s_w = _dot_nt(q, kw) * ATT_SCALE
        dist = wb - lax.broadcasted_iota(I32, (SUBLANES, wb), 1)
        wvalid = (dist <= C_WINDOW) & (past - dist >= 0)
        s_w = jnp.where(wvalid, s_w, NEG)
        s_n = jnp.sum(qf * wnew[0:1, 0:128].astype(BF16).astype(F32), axis=-1, keepdims=True) * ATT_SCALE
        m = jnp.maximum(jnp.max(s_w, axis=-1, keepdims=True), s_n)
        e_w = jnp.where(wvalid, jnp.exp(s_w - m), 0.0)
        e_n = jnp.exp(s_n - m)
        den = jnp.sum(e_w, axis=-1, keepdims=True) + e_n
        v_n = wnew[0:1, 128:256].astype(BF16).astype(F32)
        o_w = (_dot(e_w.astype(BF16), vw) + e_n.astype(BF16).astype(F32) * v_n) / den
        o_ref[...] = g3_ref[0] * o_c + g3_ref[1] * o_s + g3_ref[2] * o_w


def _nsa_sample(page_table, cache, layer, q8, g3, new8, win, wnew8, cmp_a, cmp_w, ov):
    bsz, n_pages = page_table.shape
    pg = PAGES_PER_STEP
    past = n_pages * PAGE
    n_cmp = (past + 1 - C_CMP_LEN) // C_CMP_STRIDE + 1
    n_blk = -(-(past + 1) // C_SLC_BLOCK)
    ns = past // C_CMP_STRIDE
    wb = win.shape[2]
    cst = lambda a: pl.BlockSpec(a.shape, lambda b, g, pt: (0,) * a.ndim)
    per_b = lambda r, w: pl.BlockSpec((None, r, w), lambda b, g, pt: (b, 0, 0))
    page_spec = lambda i: pl.BlockSpec((None, None, PAGE, C_CACHE_DIM),
                                       lambda b, g, pt: (layer, pt[b, g * pg + i], 0, 0))
    grid_spec = pltpu.PrefetchScalarGridSpec(
        num_scalar_prefetch=1,
        grid=(bsz, n_pages // pg),
        in_specs=[per_b(SUBLANES, LANES),
                  pl.BlockSpec((None, 3, SUBLANES, LANES), lambda b, g, pt: (b, 0, 0, 0)),
                  per_b(SUBLANES, C_CACHE_DIM),
                  pl.BlockSpec((None, None, wb, C_WIN_DIM), lambda b, g, pt: (layer, b, 0, 0)),
                  per_b(SUBLANES, C_WIN_DIM), cst(cmp_a), cst(cmp_w), cst(ov)]
        + [page_spec(i) for i in range(pg)],
        out_specs=per_b(SUBLANES, HEAD_DIM),
        scratch_shapes=[pltpu.VMEM((n_pages, SUBLANES, LANES), F32), pltpu.VMEM((n_pages, SUBLANES, LANES), F32),
                        pltpu.VMEM((n_pages * LANES, HEAD_DIM), BF16)]
        + [pltpu.VMEM((ns, HEAD_DIM), F32)] * 4,
    )
    return pl.pallas_call(
        functools.partial(_nsa_s_kernel, pg=pg, n_pages=n_pages, n_cmp=n_cmp, n_blk=n_blk, n_top=min(C_TOPN, n_blk),
                          past=past),
        grid_spec=grid_spec,
        out_shape=jax.ShapeDtypeStruct((bsz, SUBLANES, HEAD_DIM), F32),
        compiler_params=_cparams(2),
        name="nsa_sample",
    )(page_table, q8, g3, new8, win, wnew8, cmp_a, cmp_w, ov, *([cache] * pg))


def _step_kernel(u_ref, h0r_ref, h0i_ref, bdr_h_ref, bdr_l_ref, bdi_h_ref, bdi_l_ref, ar_ref, ai_ref,
                 cdr_ref, cdi_ref, d_ref, gw_ref, gb_ref, da_ref, cst_ref, cw_ref, cb_ref, lg_ref, lb_ref,
                 ob_ref, hr_ref, hi_ref, od_ref):
    u = u_ref[...]
    uh = u.astype(BF16)
    ul = (u - uh.astype(F32)).astype(BF16)

    def bmat(h_ref, l_ref):
        return _dot(uh, h_ref[...]) + (_dot(uh, l_ref[...]) + _dot(ul, h_ref[...]))

    ar, ai = ar_ref[...], ai_ref[...]
    h0r, h0i = h0r_ref[...], h0i_ref[...]
    hr = bmat(bdr_h_ref, bdr_l_ref) + (ar * h0r - ai * h0i)
    hi = bmat(bdi_h_ref, bdi_l_ref) + (ar * h0i + ai * h0r)
    hr_ref[...] = hr
    hi_ref[...] = hi
    y = _dot(hr.astype(BF16), cdr_ref[...]) - _dot(hi.astype(BF16), cdi_ref[...]) + d_ref[...] * u
    gl = _gelu(y)
    ob_ref[...] = (gl * _sigmoid(_dot(gl.astype(BF16), gw_ref[...]) + gb_ref[...])).astype(BF16)
    cw = cw_ref[...]
    y = jnp.sum(cst_ref[...] * cw[0:D_CONV - 1][None], axis=1) + cw[D_CONV - 1:D_CONV] * da_ref[...] + cb_ref[...]
    od_ref[...] = _ln_swish(y, lg_ref[...], lb_ref[...]).astype(BF16)


def _sample_step(u, h0r, h0i, sp, da, conv_state, cw, cb, lg, lb):
    bsz = u.shape[0]
    ins = [u, h0r, h0i, sp["bdr"], sp["bdr_lo"], sp["bdi"], sp["bdi_lo"], sp["ar"], sp["ai"], sp["cdr"], sp["cdi"],
           sp["d"], sp["glu_w"], sp["glu_b"], da, conv_state, cw, cb, lg, lb]
    outs = [((bsz, BR_WIDTH), BF16), ((bsz, B_LANES), F32), ((bsz, B_LANES), F32), ((bsz, BR_WIDTH), BF16)]
    return pl.pallas_call(
        _step_kernel,
        grid=(1,),
        in_specs=[_full(a, 1) for a in ins],
        out_specs=[pl.BlockSpec(o[0], lambda i: (0, 0)) for o in outs],
        out_shape=[jax.ShapeDtypeStruct(o[0], o[1]) for o in outs],
        compiler_params=_cparams(1),
        name="sample_step",
    )(*ins)


def _rope_tables(pos):
    pos = pos.astype(F32)[:, None]

    def tab(dim):
        half = dim // 2
        inv = ROPE_THETA ** (-jnp.arange(half, dtype=F32) / half)
        ang = pos * inv
        cos, sin = jnp.cos(ang), jnp.sin(ang)
        reps = LANES // dim
        return jnp.tile(jnp.concatenate([cos, cos], axis=1), (1, reps)), jnp.tile(jnp.concatenate([-sin, sin], axis=1), (1, reps))

    c128, s128 = tab(HEAD_DIM)
    c64, s64 = tab(A_IDX_DIM)
    return c128, s128, c64, s64


def _pack_w1(wt):
    def padded(a, b, rows):
        return jnp.pad(wt[a:b], ((0, rows - (b - a)), (0, 0)))
    parts = [wt[_O[0]:_O[4]], padded(_O[4], _O[5], 128), padded(_O[5], _O[6], 128), wt[_O[6]:_O[9]],
             padded(_O[9], _O[10], 128), wt[_O[10]:_O[11]]]
    return jnp.concatenate(parts, axis=0).astype(BF16)


def _s5_params(lam_re, lam_im, log_dt, b_re, b_im, c_re, c_im, d, glu_w, glu_b):
    lr, li = lam_re.astype(F32), lam_im.astype(F32)
    dt = jnp.exp(log_dt.astype(F32))[:, None]
    mag = jnp.exp(lr * dt)
    ar, ai = mag * jnp.cos(li * dt), mag * jnp.sin(li * dt)
    den = lr * lr + li * li
    fr = ((ar - 1.0) * lr + ai * li) / den
    fi = (ai * lr - (ar - 1.0) * li) / den
    br, bi = b_re.astype(F32), b_im.astype(F32)
    bbr = fr[..., None] * br - fi[..., None] * bi
    bbi = fr[..., None] * bi + fi[..., None] * br
    eye = jnp.eye(B_GROUPS, dtype=F32)
    bd = lambda m: jnp.einsum("gpc,gh->gchp", m, eye).reshape(BR_WIDTH, B_LANES)
    cd = lambda m: jnp.einsum("gcp,gh->gphc", m.astype(F32), eye).reshape(B_LANES, BR_WIDTH)
    bdr, bdi = bd(bbr), bd(bbi)
    hi_lo = lambda m: (m.astype(BF16), (m - m.astype(BF16).astype(F32)).astype(BF16))
    bdr_h, bdr_l = hi_lo(bdr)
    bdi_h, bdi_l = hi_lo(bdi)
    nsb = S5_BLOCKS
    diag_b = lambda m: jnp.stack([m[s * (BR_WIDTH // nsb):(s + 1) * (BR_WIDTH // nsb),
                                    s * (B_LANES // nsb):(s + 1) * (B_LANES // nsb)] for s in range(nsb)])
    diag_c = lambda m: jnp.stack([m[s * (B_LANES // nsb):(s + 1) * (B_LANES // nsb),
                                    s * (BR_WIDTH // nsb):(s + 1) * (BR_WIDTH // nsb)] for s in range(nsb)])
    return dict(bdr=bdr_h, bdr_lo=bdr_l, bdi=bdi_h, bdi_lo=bdi_l,
                bdr4=diag_b(bdr_h), bdi4=diag_b(bdi_h), cdr4=diag_c(cd(c_re).astype(BF16)), cdi4=diag_c(cd(c_im).astype(BF16)),
                ar=ar.reshape(1, B_LANES), ai=ai.reshape(1, B_LANES),
                cdr=cd(c_re).astype(BF16), cdi=cd(c_im).astype(BF16), d=d.astype(F32).reshape(1, BR_WIDTH),
                glu_w=glu_w.astype(BF16), glu_b=glu_b.astype(F32).reshape(1, BR_WIDTH))


def _overlap(n_cmp, n_blk, rows, cols):
    start = np.arange(n_cmp)[:, None] * C_CMP_STRIDE
    blk = np.arange(n_blk)[None, :]
    m = (start <= (blk + 1) * C_SLC_BLOCK - 1) & (start + C_CMP_LEN - 1 >= blk * C_SLC_BLOCK)
    out = np.zeros((rows, cols), np.float32)
    out[:n_cmp, :n_blk] = m
    return jnp.asarray(out, BF16)


def _expand(n_keys):
    e = (np.arange(LANES)[:, None] == (np.arange(n_keys)[None, :] // C_SLC_BLOCK)).astype(np.float32)
    return jnp.asarray(e, BF16)


def _tri(lower=False):
    i = np.arange(LANES)
    m = (i[:, None] > i[None, :]) if lower else (i[:, None] < i[None, :])
    return jnp.asarray(m.astype(np.float32), BF16)


def _make_consts(T, sb, past):
    n_cmp_p = (T - C_CMP_LEN) // C_CMP_STRIDE + 1
    n_blk_p = -(-T // C_SLC_BLOCK)
    n_cmp_s = (past + 1 - C_CMP_LEN) // C_CMP_STRIDE + 1
    n_blk_s = -(-(past + 1) // C_SLC_BLOCK)
    return dict(
        tabs_p=_rope_tables(jnp.arange(T)),
        tabs_s=_rope_tables(jnp.full((sb,), past)),
        tri=_tri(),
        tril=_tri(lower=True),
        ov_p=_overlap(n_cmp_p, n_blk_p, T // C_CMP_STRIDE, LANES),
        ex_p=_expand(T),
        ov_s=_overlap(n_cmp_s, n_blk_s, past // C_CMP_STRIDE, -(-n_blk_s // LANES) * LANES),
    )


def _pick_tile(n, cands):
    for c in cands:
        if n % c == 0:
            return c
    return n


def _pad_rows(a, rows):
    return jnp.pad(a[:, None, :], ((0, 0), (0, rows - 1), (0, 0)))


def _layer_weights(l, norm_mix, w_in, a_gq, a_gk, c_gq, c_gk, c_cmp_a, c_cmp_w, d_conv_w, d_conv_b, d_ln_g, d_ln_b,
                   w_br, w_o, norm_mlp, w_up, w_down):
    row = lambda v: v.astype(F32).reshape(1, -1)
    return dict(
        g_mix=row(norm_mix[l]), w1=_pack_w1(jnp.transpose(w_in[l])),
        wg=jnp.transpose(w_in[l])[_O[11]:_O[12]].astype(BF16),
        a_gq=row(a_gq[l]), a_gk=row(a_gk[l]), c_gq=row(c_gq[l]), c_gk=c_gk[l].astype(F32),
        cmp_a=c_cmp_a[l].astype(F32), cmp_w=c_cmp_w[l].astype(BF16),
        conv_w=jnp.pad(d_conv_w[l].astype(F32), ((0, 1), (0, 0))), conv_b=row(d_conv_b[l]),
        ln_g=row(d_ln_g[l]), ln_b=row(d_ln_b[l]),
        w_br=w_br[l].astype(BF16), w_o=w_o[l].astype(BF16), g_mlp=row(norm_mlp[l]),
        w_up=w_up[l].astype(BF16), w_down=w_down[l].astype(BF16))


def _prompt_layer(x, lw, sp, consts):
    bsz, T, _ = x.shape
    n = bsz * T
    x2d = x.reshape(n, D_MODEL)
    tm = _pick_tile(T, (256, 128))
    nt = T // tm
    qb = 128
    bu_spec = pl.BlockSpec((tm, BR_WIDTH), lambda i: (i % nt, i // nt))
    a_t = ((bsz, A_CACHE_DIM, T), pl.BlockSpec((None, A_CACHE_DIM, tm), lambda i: (i // nt, 0, i % nt)))
    (xn, aq, arow, aiq, aiw, bu, cq, crow, wrow, cg, da, arow_t) = _project(
        x2d, consts["tabs_p"], lw, tm=tm, n_pos_blocks=nt, bu_shape=(T, bsz * BR_WIDTH), bu_spec=bu_spec, a_rows_t=a_t)
    arow3 = arow.reshape(bsz, T, A_CACHE_DIM)
    crow3 = crow.reshape(bsz, T, C_CACHE_DIM)
    wrow3 = wrow.reshape(bsz, T, C_WIN_DIM)
    da3 = da.reshape(bsz, T, BR_WIDTH)
    o_a = _dsa(aq, aiq, aiw, arow3, consts["tri"], qb=qb)
    kcmp, vcmp = _compress(crow3, lw["cmp_a"], lw["cmp_w"])
    o_c = _nsa(cq, cg, crow3, wrow3, kcmp, vcmp, consts["ov_p"], consts["ex_p"], qb=qb)
    tc = _pick_tile(T, (128, 64))
    o_b, hr, hi = _s5(bu.reshape(T * bsz, BR_WIDTH), sp, nb=bsz, tc=tc)
    o_d = _conv(da3, lw["conv_w"], lw["conv_b"], lw["ln_g"], lw["ln_b"], tm=tm)
    tmx = _pick_tile(T, (512, 256, 128))
    ntx = T // tmx
    ob_spec = pl.BlockSpec((tmx, BR_WIDTH), lambda r, i: (r % ntx, r // ntx))
    hm = _mix(xn, o_a, o_b.reshape(T, bsz * BR_WIDTH), ob_spec, o_c, o_d, lw["wg"], lw["w_br"], lw["w_o"], tm=tmx)
    y = _mlp(x2d, hm, lw["g_mlp"], lw["w_up"], lw["w_down"], tm=tmx)
    wk = min(C_WINDOW, T)
    return (y.reshape(bsz, T, D_MODEL), jnp.swapaxes(arow_t, 1, 2), crow3, wrow3[:, T - wk:],
            hr.reshape(bsz, B_GROUPS, B_STATE), hi.reshape(bsz, B_GROUPS, B_STATE), da3[:, T - (D_CONV - 1):])


def _sample_layer(x, l, cache_a, cache_c, cache_c_win, h_re, h_im, conv_l, page_table, lw, sp, consts):
    bsz = x.shape[0]
    cache_a_t = jnp.swapaxes(cache_a, 2, 3)
    x2d = x.reshape(bsz, D_MODEL)
    row = lambda w: pl.BlockSpec((bsz, w), lambda i: (0, 0))
    (xn, aq, arow, aiq, aiw, bu, cq, crow, wrow, cg, da) = _project(
        x2d, consts["tabs_s"], lw, tm=bsz, n_pos_blocks=1, bu_shape=(bsz, BR_WIDTH), bu_spec=row(BR_WIDTH))
    q8 = jnp.pad(aq.astype(F32).reshape(bsz, A_HEADS, HEAD_DIM), ((0, 0), (0, SUBLANES - A_HEADS), (0, 0)))
    iq8 = jnp.sum(aiq.astype(F32).reshape(bsz, A_IDX_HEADS, 2, A_IDX_DIM), axis=2)
    iq8 = jnp.pad(iq8, ((0, 0), (0, 0), (LANES - A_IDX_DIM, 0)))
    w8 = jnp.broadcast_to(aiw[:, :A_IDX_HEADS, None], (bsz, A_IDX_HEADS, LANES))
    o_a = _dsa_sample(page_table, cache_a_t, l, q8, iq8, w8, _pad_rows(arow, SUBLANES), consts["tri"], consts["tril"])
    o_a = o_a[:, :A_HEADS].reshape(bsz, BR_WIDTH).astype(BF16)
    cq8 = jnp.pad(cq.astype(F32).reshape(bsz, C_HEADS, HEAD_DIM), ((0, 0), (0, SUBLANES - C_HEADS), (0, 0)))
    g3 = jnp.transpose(cg[:, :3 * C_HEADS].reshape(bsz, C_HEADS, 3), (0, 2, 1))
    g3 = jnp.broadcast_to(jnp.pad(g3, ((0, 0), (0, 0), (0, SUBLANES - C_HEADS)))[..., None], (bsz, 3, SUBLANES, LANES))
    o_c = _nsa_sample(page_table, cache_c, l, cq8, g3, _pad_rows(crow, SUBLANES), cache_c_win,
                      _pad_rows(wrow, SUBLANES), lw["cmp_a"], lw["cmp_w"], consts["ov_s"])
    o_c = o_c[:, :C_HEADS].reshape(bsz, BR_WIDTH).astype(BF16)
    o_b, hr, hi, o_d = _sample_step(bu, h_re.reshape(bsz, B_LANES), h_im.reshape(bsz, B_LANES), sp, da, conv_l,
                                    lw["conv_w"], lw["conv_b"], lw["ln_g"], lw["ln_b"])
    hm = _mix(xn, o_a, o_b, pl.BlockSpec((bsz, BR_WIDTH), lambda r, i: (r, 0)), o_c, o_d,
              lw["wg"], lw["w_br"], lw["w_o"], tm=bsz)
    y = _mlp(x2d, hm, lw["g_mlp"], lw["w_up"], lw["w_down"], tm=bsz)
    new_win = jnp.concatenate([cache_c_win[l][:, 1:], wrow[:, None, :]], axis=1)
    new_conv = jnp.concatenate([conv_l[:, 1:], da[:, None, :]], axis=1)
    return (y.reshape(bsz, 1, D_MODEL), arow[:, None, :], crow[:, None, :], new_win,
            hr.reshape(bsz, B_GROUPS, B_STATE), hi.reshape(bsz, B_GROUPS, B_STATE), new_conv)


def kernel(x_prompt, x_sample, cache_a, cache_c, cache_c_win, state_b_re, state_b_im, state_d_conv, page_table, norm_mix, w_in, a_gq, a_gk, b_lam_re, b_lam_im, b_log_dt, b_b_re, b_b_im, b_c_re, b_c_im, b_d, b_glu_w, b_glu_b, c_gq, c_gk, c_cmp_a, c_cmp_w, d_conv_w, d_conv_b, d_ln_g, d_ln_b, w_br, w_o, norm_mlp, w_up, w_down):
    depth = w_in.shape[0]
    bsz, T, _ = x_prompt.shape
    sb, st, _ = x_sample.shape
    assert st == 1 and bsz == SUBLANES
    assert cache_a.shape[2] == PAGE and cache_c.shape[2] == PAGE
    consts = _make_consts(T, sb, page_table.shape[1] * PAGE)
    xp, xs = x_prompt, x_sample
    order_p = (0, 2, 4, 6, 7, 10)
    order_s = (1, 3, 5, 8, 9, 11)
    outs = [[] for _ in range(12)]
    for l in range(depth):
        lw = _layer_weights(l, norm_mix, w_in, a_gq, a_gk, c_gq, c_gk, c_cmp_a, c_cmp_w, d_conv_w, d_conv_b,
                            d_ln_g, d_ln_b, w_br, w_o, norm_mlp, w_up, w_down)
        sp = _s5_params(b_lam_re[l], b_lam_im[l], b_log_dt[l], b_b_re[l], b_b_im[l], b_c_re[l], b_c_im[l],
                        b_d[l], b_glu_w[l], b_glu_b[l])
        xp, *rp = _prompt_layer(xp, lw, sp, consts)
        xs, *rs = _sample_layer(xs, l, cache_a, cache_c, cache_c_win, state_b_re[l], state_b_im[l], state_d_conv[l],
                                page_table, lw, sp, consts)
        for k in range(6):
            outs[order_p[k]].append(rp[k])
            outs[order_s[k]].append(rs[k])
    return (xp, xs) + tuple(jnp.stack(o) for o in outs)
```

```python
import functools
import math

import numpy as np
import jax
import jax.numpy as jnp
from jax import lax
from jax.experimental import pallas as pl
from jax.experimental.pallas import tpu as pltpu

F32 = jnp.float32
BF16 = jnp.bfloat16
I32 = jnp.int32

D_MODEL = 2048
HEAD_DIM = 128
N_BRANCH = 4
BR_WIDTH = D_MODEL // N_BRANCH
ROPE_THETA = 10000.0
NORM_EPS = 1e-6
A_HEADS = BR_WIDTH // HEAD_DIM
A_IDX_HEADS = 8
A_IDX_DIM = 64
A_TOPK = 256
B_GROUP = 16
B_GROUPS = BR_WIDTH // B_GROUP
B_STATE = 64
B_LANES = B_GROUPS * B_STATE
C_HEADS = BR_WIDTH // HEAD_DIM
C_CMP_STRIDE = 16
C_CMP_LEN = 2 * C_CMP_STRIDE
C_SLC_BLOCK = 64
C_TOPN = 16
C_WINDOW = 512
C_FORCE = 1e4
D_CONV = 31
D_FF = 4 * D_MODEL
A_CACHE_DIM = 2 * HEAD_DIM + A_IDX_DIM
C_CACHE_DIM = 4 * HEAD_DIM
C_WIN_DIM = 2 * HEAD_DIM
PAGE = 128

LANES = 128
SUBLANES = 8
VMEM_LIMIT_MB = 56

_W = (A_HEADS * HEAD_DIM, HEAD_DIM, HEAD_DIM, A_IDX_HEADS * A_IDX_DIM, A_IDX_DIM, A_IDX_HEADS,
      BR_WIDTH, C_HEADS * HEAD_DIM, 6 * HEAD_DIM, 3 * C_HEADS, 2 * BR_WIDTH, N_BRANCH * D_MODEL)
_O = tuple(int(v) for v in np.cumsum((0,) + _W))
_P = {}
_cur = 0
for _name, _w in (("aq", 512), ("ak", 128), ("av", 128), ("aiq", 512), ("aik", 128), ("aiw", 128),
                  ("bu", 512), ("cq", 512), ("ckv", 768), ("cg", 128), ("dglu", 1024)):
    _P[_name] = (_cur, _w)
    _cur += _w
P_TOTAL = _cur

NEG = -1e30
ATT_SCALE = HEAD_DIM ** -0.5
SIGN = -2 ** 31


def _cparams(n_axes):
    return pltpu.CompilerParams(dimension_semantics=("arbitrary",) * n_axes,
                                vmem_limit_bytes=VMEM_LIMIT_MB * 1024 * 1024)


def _full(a, n_grid):
    nd = a.ndim
    return pl.BlockSpec(a.shape, lambda *_: (0,) * nd)


def _dot(a, b):
    return jnp.dot(a, b, preferred_element_type=F32)


def _dot_nt(a, b):
    return lax.dot_general(a, b, (((1,), (1,)), ((), ())), preferred_element_type=F32)


def _dot_split3(p, m):
    hi = p.astype(BF16)
    r = p - hi.astype(F32)
    mid = r.astype(BF16)
    lo = (r - mid.astype(F32)).astype(BF16)
    return _dot(hi, m) + _dot(mid, m) + _dot(lo, m)


def _rms(x, g):
    return x * lax.rsqrt(jnp.mean(x * x, axis=-1, keepdims=True) + NORM_EPS) * g


def _sigmoid(x):
    return 1.0 / (1.0 + jnp.exp(-x))


def _gelu(x):
    return x * (0.5 * (1.0 + jnp.tanh(math.sqrt(2.0 / math.pi) * (x + 0.044715 * (x * x * x)))))


def _sortable(x):
    b = pltpu.bitcast(x + 0.0, I32)
    return jnp.where(b < 0, b ^ jnp.int32(0x7FFFFFFF), b)


def _kth_key(key_ref, k, red_axes):
    shp = tuple(1 if a in red_axes else s for a, s in enumerate(key_ref.shape))

    def count(mask):
        c = jnp.where(mask, 1.0, 0.0)
        for a in sorted(red_axes):
            c = jnp.sum(c, axis=a, keepdims=True)
        return c

    def body(it, tu):
        cand_u = tu | jnp.left_shift(jnp.int32(1), 31 - it)
        cand_s = cand_u ^ jnp.int32(SIGN)
        return jnp.where(count(key_ref[...] >= cand_s) >= k, cand_u, tu)

    tu = lax.fori_loop(0, 32, body, jnp.zeros(shp, I32), unroll=4)
    return tu ^ jnp.int32(SIGN), count


def _select_rows(key_ref, k, tri):
    ts, count = _kth_key(key_ref, k, (1,))
    keys = key_ref[...]
    gt = keys > ts
    need = k - count(gt)
    eqf = jnp.where(keys == ts, 1.0, 0.0)
    base = jnp.zeros_like(need)
    pieces = []
    for c in range(keys.shape[1] // LANES):
        ch = eqf[:, c * LANES:(c + 1) * LANES]
        pref = _dot(ch.astype(BF16), tri) + base
        pieces.append(jnp.where(pref < need, ch, 0.0))
        base = base + jnp.sum(ch, axis=-1, keepdims=True)
    sel_eq = pieces[0] if len(pieces) == 1 else jnp.concatenate(pieces, axis=1)
    return jnp.where(gt, 1.0, sel_eq)


def _select_packed(key_ref, k, tri, tril):
    shp = key_ref.shape
    ts, count = _kth_key(key_ref, k, (0, 1, 2))
    keys = key_ref[...]
    gt = keys > ts
    need = k - count(gt)
    eqf = jnp.where(keys == ts, 1.0, 0.0)
    eq2 = eqf.reshape(LANES, LANES)
    within = _dot(eq2.astype(BF16), tri)
    tot = jnp.broadcast_to(jnp.sum(eq2, axis=1, keepdims=True), (LANES, LANES))
    base = _dot(tril, tot.astype(BF16))
    sel_eq = jnp.where((within + base).reshape(shp) < need, eqf, 0.0)
    return jnp.where(gt, 1.0, sel_eq)


def _select_rank(sc, n, k):
    lane = lax.broadcasted_iota(I32, sc.shape, 1)
    rank = jnp.zeros(sc.shape, F32)
    for i in range(n):
        col = sc[:, i:i + 1]
        rank = rank + jnp.where(lane > i, jnp.where(col >= sc, 1.0, 0.0), jnp.where(col > sc, 1.0, 0.0))
    return jnp.where(rank < k, 1.0, 0.0)


LOG2E = 1.4426950408889634


def _masked_attn(qh, k, v, maskf):
    s = jnp.where(maskf > 0.5, _dot_nt(qh, k) * (ATT_SCALE * LOG2E), NEG)
    e = jnp.exp2(s - jnp.max(s, axis=-1, keepdims=True)) * maskf
    den = jnp.maximum(jnp.sum(e, axis=-1, keepdims=True), 1e-30)
    return _dot(e.astype(BF16), v) / den


def _proj_kernel(x_ref, g_ref, w_ref, c128_ref, s128_ref, c64_ref, s64_ref, gqa_ref, gka_ref, gqc_ref, gkc_ref,
                 xn_ref, aq_ref, arow_ref, aiq_ref, aiw_ref, bu_ref, cq_ref, crow_ref, wrow_ref, cg_ref, da_ref,
                 arow_t_ref=None):
    xn = _rms(x_ref[...], g_ref[...]).astype(BF16)
    xn_ref[...] = xn
    cos, sin = c128_ref[...], s128_ref[...]
    cos64, sin64 = c64_ref[...], s64_ref[...]
    lane = lax.broadcasted_iota(I32, cos.shape, 1)
    lo32 = (lane & 63) < 32
    lo64 = lane < 64

    def seg(name):
        a, w = _P[name]
        return _dot_nt(xn, w_ref[a:a + w, :])

    def rope128(v):
        return v * cos + pltpu.roll(v, 64, 1) * sin

    def rope64(v):
        rot = jnp.where(lo32, pltpu.roll(v, 96, 1), pltpu.roll(v, 32, 1))
        return v * cos64 + rot * sin64

    z = seg("aq")
    for h in range(A_HEADS):
        sl = slice(h * HEAD_DIM, (h + 1) * HEAD_DIM)
        aq_ref[:, sl] = rope128(_rms(z[:, sl], gqa_ref[...])).astype(BF16)
    ak = rope128(_rms(seg("ak"), gka_ref[...]))
    av = seg("av")
    aik = rope64(seg("aik"))
    arow_ref[:, 0:128] = ak
    arow_ref[:, 128:256] = av
    arow_ref[:, 256:320] = aik[:, 0:A_IDX_DIM]
    if arow_t_ref is not None:
        arow_t_ref[0:128, :] = ak.T
        arow_t_ref[128:256, :] = av.T
        arow_t_ref[256:320, :] = aik.T[0:A_IDX_DIM, :]
    z = seg("aiq")
    for j in range(A_IDX_HEADS // 2):
        r = rope64(z[:, j * LANES:(j + 1) * LANES])
        aiq_ref[:, (2 * j) * LANES:(2 * j + 1) * LANES] = jnp.where(lo64, r, 0.0).astype(BF16)
        aiq_ref[:, (2 * j + 1) * LANES:(2 * j + 2) * LANES] = jnp.where(lo64, 0.0, r).astype(BF16)
    aiw_ref[...] = seg("aiw") * (A_IDX_HEADS ** -0.5) * (A_IDX_DIM ** -0.5)
    bu_ref[...] = seg("bu")
    z = seg("cq")
    for h in range(C_HEADS):
        sl = slice(h * HEAD_DIM, (h + 1) * HEAD_DIM)
        cq_ref[:, sl] = rope128(_rms(z[:, sl], gqc_ref[...])).astype(BF16)
    z = seg("ckv")
    for br in range(3):
        kk = rope128(_rms(z[:, (2 * br) * LANES:(2 * br + 1) * LANES], gkc_ref[br:br + 1, :]))
        vv = z[:, (2 * br + 1) * LANES:(2 * br + 2) * LANES]
        if br < 2:
            crow_ref[:, (2 * br) * LANES:(2 * br + 1) * LANES] = kk
            crow_ref[:, (2 * br + 1) * LANES:(2 * br + 2) * LANES] = vv
        else:
            wrow_ref[:, 0:LANES] = kk
            wrow_ref[:, LANES:2 * LANES] = vv
    cg_ref[...] = _sigmoid(seg("cg"))
    z = seg("dglu")
    da_ref[...] = z[:, 0:BR_WIDTH] * _sigmoid(z[:, BR_WIDTH:2 * BR_WIDTH])


def _project(x2d, tabs, lw, *, tm, n_pos_blocks, bu_shape, bu_spec, a_rows_t=None):
    n = x2d.shape[0]
    row = lambda w: pl.BlockSpec((tm, w), lambda i: (i, 0))
    tab = pl.BlockSpec((tm, LANES), lambda i: (i % n_pos_blocks, 0))
    ins = [x2d, lw["g_mix"], lw["w1"], tabs[0], tabs[1], tabs[2], tabs[3], lw["a_gq"], lw["a_gk"], lw["c_gq"], lw["c_gk"]]
    w_spec = pl.BlockSpec(ins[2].shape, lambda i: (0, 0), pipeline_mode=pl.Buffered(1))
    in_specs = [row(D_MODEL), _full(ins[1], 1), w_spec, tab, tab, tab, tab] + [_full(a, 1) for a in ins[7:]]
    outs = [((n, D_MODEL), BF16, row(D_MODEL)),
            ((n, 512), BF16, row(512)),
            ((n, A_CACHE_DIM), F32, row(A_CACHE_DIM)),
            ((n, 1024), BF16, row(1024)),
            ((n, LANES), F32, row(LANES)),
            (bu_shape, F32, bu_spec),
            ((n, 512), BF16, row(512)),
            ((n, C_CACHE_DIM), F32, row(C_CACHE_DIM)),
            ((n, C_WIN_DIM), F32, row(C_WIN_DIM)),
            ((n, LANES), F32, row(LANES)),
            ((n, BR_WIDTH), F32, row(BR_WIDTH))]
    if a_rows_t is not None:
        outs.append((a_rows_t[0], F32, a_rows_t[1]))
    return pl.pallas_call(
        _proj_kernel,
        grid=(n // tm,),
        in_specs=in_specs,
        out_specs=[o[2] for o in outs],
        out_shape=[jax.ShapeDtypeStruct(o[0], o[1]) for o in outs],
        compiler_params=_cparams(1),
        name="project",
    )(*ins)


def _dsa_kernel(aq_ref, aiq_ref, aiw_ref, arow_ref, tri_ref, o_ref, k_sc, v_sc, ik_sc, key_sc, *, qb, L, n_sel):
    qi = pl.program_id(1)

    @pl.when(qi == 0)
    def _():
        k_sc[...] = arow_ref[:, 0:128].astype(BF16)
        v_sc[...] = arow_ref[:, 128:256].astype(BF16)
        ik = arow_ref[:, 256:320]
        ik_sc[...] = jnp.concatenate([ik, ik], axis=1).astype(BF16)

    def body(le):
        qpos = qi * qb + lax.broadcasted_iota(I32, (qb, 1), 0)
        valid = lax.broadcasted_iota(I32, (qb, le), 1) <= qpos
        if le <= n_sel:
            mask = jnp.where(valid, 1.0, 0.0)
        else:
            w = aiw_ref[...]
            score = None
            for h in range(A_IDX_HEADS):
                lg = _dot_nt(aiq_ref[:, h * LANES:(h + 1) * LANES], ik_sc[0:le, :])
                t = jnp.maximum(lg, 0.0) * w[:, h:h + 1]
                score = t if score is None else score + t
            keys = key_sc.at[:, 0:le]
            keys[...] = _sortable(jnp.where(valid, score, -jnp.inf))
            mask = jnp.where(valid, _select_rows(keys, n_sel, tri_ref[...]), 0.0)
        q = aq_ref[...]
        for h in range(A_HEADS):
            sl = slice(h * HEAD_DIM, (h + 1) * HEAD_DIM)
            o_ref[:, sl] = _masked_attn(q[:, sl], k_sc[0:le, :], v_sc[0:le, :], mask).astype(BF16)

    _causal_branches(qi, qb, L, body)


def _causal_branches(qi, qb, L, body):
    step = max(qb, L // 8)
    if L % step:
        step = qb
    per = step // qb
    for j in range(L // step):
        @pl.when((qi >= j * per) & (qi < (j + 1) * per))
        def _(j=j):
            body((j + 1) * step)


def _dsa(aq, aiq, aiw, arow3, tri, *, qb):
    bsz, L, _ = arow3.shape
    nq = L // qb
    n_sel = min(A_TOPK, L // 4)
    row = lambda w: pl.BlockSpec((qb, w), lambda b, i: (b * nq + i, 0))
    return pl.pallas_call(
        functools.partial(_dsa_kernel, qb=qb, L=L, n_sel=n_sel),
        grid=(bsz, nq),
        in_specs=[row(512), row(1024), row(LANES),
                  pl.BlockSpec((None, L, A_CACHE_DIM), lambda b, i: (b, 0, 0)), _full(tri, 2)],
        out_specs=row(512),
        out_shape=jax.ShapeDtypeStruct((bsz * L, 512), BF16),
        scratch_shapes=[pltpu.VMEM((L, 128), BF16), pltpu.VMEM((L, 128), BF16), pltpu.VMEM((L, 128), BF16),
                        pltpu.VMEM((qb, L), I32)],
        compiler_params=_cparams(2),
        name="dsa",
    )(aq, aiq, aiw, arow3, tri)


def _summaries(x, a, w, n_cmp):
    ns = x.shape[0] // C_CMP_STRIDE
    x3 = x.reshape(ns, C_CMP_STRIDE, HEAD_DIM)
    lo = jnp.sum(x3 * a[0:C_CMP_STRIDE][None], axis=1)
    hi = jnp.sum(x3 * a[C_CMP_STRIDE:C_CMP_LEN][None], axis=1)
    return lo, hi


def _finish_summaries(lo, hi, w, n_cmp):
    ns = lo.shape[0]
    comb = lo + pltpu.roll(hi, ns - 1, 0)
    comb = jnp.where(lax.broadcasted_iota(I32, comb.shape, 0) < n_cmp, comb, 0.0)
    return _dot(comb.astype(BF16), w).astype(BF16)


def _cmp_kernel(x_ref, a_ref, w_ref, kc_ref, vc_ref, *, n_cmp):
    for t, out in ((0, kc_ref), (1, vc_ref)):
        lo, hi = _summaries(x_ref[:, t * LANES:(t + 1) * LANES], a_ref[t], w_ref[t], n_cmp)
        out[...] = _finish_summaries(lo, hi, w_ref[t], n_cmp)


def _compress(crow3, cmp_a, cmp_w):
    bsz, L, _ = crow3.shape
    ns = L // C_CMP_STRIDE
    n_cmp = (L - C_CMP_LEN) // C_CMP_STRIDE + 1
    out = pl.BlockSpec((None, ns, HEAD_DIM), lambda b: (b, 0, 0))
    return pl.pallas_call(
        functools.partial(_cmp_kernel, n_cmp=n_cmp),
        grid=(bsz,),
        in_specs=[pl.BlockSpec((None, L, C_CACHE_DIM), lambda b: (b, 0, 0)), _full(cmp_a, 1), _full(cmp_w, 1)],
        out_specs=[out, out],
        out_shape=[jax.ShapeDtypeStruct((bsz, ns, HEAD_DIM), BF16)] * 2,
        compiler_params=_cparams(1),
        name="compress",
    )(crow3, cmp_a, cmp_w)


def _block_scores(imp, qpos):
    j = lax.broadcasted_iota(I32, imp.shape, 1)
    cur = lax.shift_right_logical(qpos, 6)
    forced = (j == 0) | (j == cur) | (j == cur - 1)
    return jnp.where(j <= cur, jnp.where(forced, C_FORCE, imp), -jnp.inf)


def _cmp_softmax(qh, kc, cvalid):
    s = _dot_nt(qh, kc) * ATT_SCALE
    s = jnp.where(cvalid, s, NEG)
    m = jnp.max(s, axis=-1, keepdims=True)
    e = jnp.where(cvalid, jnp.exp(s - m), 0.0)
    return e / jnp.maximum(jnp.sum(e, axis=-1, keepdims=True), 1e-30)


def _nsa_kernel(cq_ref, cg_ref, crow_ref, wrow_ref, kc_ref, vc_ref, ov_ref, ex_ref, o_ref,
                ks_sc, vs_sc, kw_sc, vw_sc, os_sc, *, qb, L, n_cmp, n_blk, n_top, wsl):
    qi = pl.program_id(1)

    @pl.when(qi == 0)
    def _():
        ks_sc[...] = crow_ref[:, 256:384].astype(BF16)
        vs_sc[...] = crow_ref[:, 384:512].astype(BF16)
        kw_sc[...] = wrow_ref[:, 0:128].astype(BF16)
        vw_sc[...] = wrow_ref[:, 128:256].astype(BF16)

    q0 = qi * qb
    qpos = q0 + lax.broadcasted_iota(I32, (qb, 1), 0)
    q = cq_ref[...]
    g = cg_ref[...]
    kc, vc = kc_ref[...], vc_ref[...]
    ncp = kc.shape[0]
    n_io = lax.broadcasted_iota(I32, (qb, ncp), 1)
    cvalid = (n_io < n_cmp) & (n_io * C_CMP_STRIDE + (C_CMP_LEN - 1) <= qpos)
    o_c, psum = [], None
    for h in range(C_HEADS):
        p = _cmp_softmax(q[:, h * HEAD_DIM:(h + 1) * HEAD_DIM], kc, cvalid)
        o_c.append(_dot(p.astype(BF16), vc))
        psum = p if psum is None else psum + p
    imp = _dot_split3(psum, ov_ref[...])
    sel = _select_rank(_block_scores(imp, qpos), n_blk, n_top).astype(BF16)

    def selected(le):
        selk = _dot(sel, ex_ref[:, 0:le])
        smask = jnp.where(lax.broadcasted_iota(I32, (qb, le), 1) <= qpos, selk, 0.0)
        for h in range(C_HEADS):
            sl = slice(h * HEAD_DIM, (h + 1) * HEAD_DIM)
            os_sc[:, sl] = _masked_attn(q[:, sl], ks_sc[0:le, :], vs_sc[0:le, :], smask)

    _causal_branches(qi, qb, L, selected)
    start = pl.multiple_of(jnp.minimum(jnp.maximum(q0 - C_WINDOW, 0), L - wsl), qb)
    dist = qpos - (start + lax.broadcasted_iota(I32, (qb, wsl), 1))
    wmask = jnp.where((dist >= 0) & (dist <= C_WINDOW), 1.0, 0.0)
    kw = kw_sc[pl.ds(start, wsl), :]
    vw = vw_sc[pl.ds(start, wsl), :]
    for h in range(C_HEADS):
        sl = slice(h * HEAD_DIM, (h + 1) * HEAD_DIM)
        o_w = _masked_attn(q[:, sl], kw, vw, wmask)
        out = g[:, 3 * h:3 * h + 1] * o_c[h] + g[:, 3 * h + 1:3 * h + 2] * os_sc[:, sl] + g[:, 3 * h + 2:3 * h + 3] * o_w
        o_ref[:, sl] = out.astype(BF16)


def _nsa(cq, cg, crow3, wrow3, kcmp, vcmp, ov, ex, *, qb):
    bsz, L, _ = crow3.shape
    nq = L // qb
    ns = kcmp.shape[1]
    n_cmp = (L - C_CMP_LEN) // C_CMP_STRIDE + 1
    n_blk = -(-L // C_SLC_BLOCK)
    wsl = min(L, C_WINDOW + qb)
    row = lambda w: pl.BlockSpec((qb, w), lambda b, i: (b * nq + i, 0))
    per_b = lambda r, w: pl.BlockSpec((None, r, w), lambda b, i: (b, 0, 0))
    return pl.pallas_call(
        functools.partial(_nsa_kernel, qb=qb, L=L, n_cmp=n_cmp, n_blk=n_blk, n_top=min(C_TOPN, n_blk), wsl=wsl),
        grid=(bsz, nq),
        in_specs=[row(512), row(LANES), per_b(L, C_CACHE_DIM), per_b(L, C_WIN_DIM),
                  per_b(ns, HEAD_DIM), per_b(ns, HEAD_DIM), _full(ov, 2), _full(ex, 2)],
        out_specs=row(512),
        out_shape=jax.ShapeDtypeStruct((bsz * L, 512), BF16),
        scratch_shapes=[pltpu.VMEM((L, 128), BF16)] * 4 + [pltpu.VMEM((qb, 512), F32)],
        compiler_params=_cparams(2),
        name="nsa",
    )(cq, cg, crow3, wrow3, kcmp, vcmp, ov, ex)


S5_BLOCKS = 4


def _s5_kernel(u_ref, bdr_ref, bdi_ref, ar_ref, ai_ref, cdr_ref, cdi_ref, d_ref, gw_ref, gb_ref,
               o_ref, hr_ref, hi_ref, xr_sc, xi_sc, h_sc, *, tc, nb):
    i = pl.program_id(0)

    @pl.when(i == 0)
    def _():
        h_sc[...] = jnp.zeros_like(h_sc)

    u = u_ref[...]
    ub = u.astype(BF16)
    nsb = bdr_ref.shape[0]
    wu, wx = BR_WIDTH // nsb, B_LANES // nsb
    for sb in range(nsb):
        us = ub[:, sb * wu:(sb + 1) * wu]
        xr_sc[:, sb * wx:(sb + 1) * wx] = _dot(us, bdr_ref[sb])
        xi_sc[:, sb * wx:(sb + 1) * wx] = _dot(us, bdi_ref[sb])
    ar = jnp.broadcast_to(ar_ref[...], (nb, B_LANES))
    ai = jnp.broadcast_to(ai_ref[...], (nb, B_LANES))

    def step(t, carry):
        hr, hi = carry
        r0 = pl.multiple_of(t * nb, nb)
        nhr = ar * hr - ai * hi + xr_sc[pl.ds(r0, nb), :]
        nhi = ar * hi + ai * hr + xi_sc[pl.ds(r0, nb), :]
        xr_sc[pl.ds(r0, nb), :] = nhr
        xi_sc[pl.ds(r0, nb), :] = nhi
        return nhr, nhi

    hr, hi = lax.fori_loop(0, tc, step, (h_sc[0], h_sc[1]))
    h_sc[0] = hr
    h_sc[1] = hi
    hr_ref[...] = hr
    hi_ref[...] = hi
    ch = [_dot(xr_sc[:, sb * wx:(sb + 1) * wx].astype(BF16), cdr_ref[sb])
          - _dot(xi_sc[:, sb * wx:(sb + 1) * wx].astype(BF16), cdi_ref[sb]) for sb in range(nsb)]
    y = jnp.concatenate(ch, axis=1) + d_ref[...] * u
    gl = _gelu(y)
    o_ref[...] = (gl * _sigmoid(_dot(gl.astype(BF16), gw_ref[...]) + gb_ref[...])).astype(BF16)


def _s5(u_tm, sp, *, nb, tc):
    rows = u_tm.shape[0]
    r = tc * nb
    consts = [sp["bdr4"], sp["bdi4"], sp["ar"], sp["ai"], sp["cdr4"], sp["cdi4"], sp["d"], sp["glu_w"], sp["glu_b"]]
    st = pl.BlockSpec((nb, B_LANES), lambda i: (0, 0))
    return pl.pallas_call(
        functools.partial(_s5_kernel, tc=tc, nb=nb),
        grid=(rows // r,),
        in_specs=[pl.BlockSpec((r, BR_WIDTH), lambda i: (i, 0))] + [_full(c, 1) for c in consts],
        out_specs=[pl.BlockSpec((r, BR_WIDTH), lambda i: (i, 0)), st, st],
        out_shape=[jax.ShapeDtypeStruct((rows, BR_WIDTH), BF16),
                   jax.ShapeDtypeStruct((nb, B_LANES), F32), jax.ShapeDtypeStruct((nb, B_LANES), F32)],
        scratch_shapes=[pltpu.VMEM((r, B_LANES), F32), pltpu.VMEM((r, B_LANES), F32), pltpu.VMEM((2, nb, B_LANES), F32)],
        compiler_params=_cparams(1),
        name="s5",
    )(u_tm, *consts)


HALO = 32


def _ln_swish(y, g, b):
    yc = y - jnp.mean(y, axis=-1, keepdims=True)
    yn = yc * lax.rsqrt(jnp.mean(yc * yc, axis=-1, keepdims=True) + NORM_EPS) * g + b
    return yn * _sigmoid(yn)


def _conv_kernel(cur_ref, halo_ref, w_ref, b_ref, lg_ref, lb_ref, o_ref, ext_sc, *, tm):
    i = pl.program_id(1)
    ext_sc[0, 0:HALO, :] = jnp.where(i == 0, 0.0, halo_ref[...])
    ext_sc[0, HALO:HALO + tm, :] = cur_ref[...]
    n = HALO + tm - SUBLANES
    for k in range(1, SUBLANES):
        ext_sc[k, 0:n, :] = ext_sc[0, pl.ds(k, n), :]
    acc = jnp.zeros((tm, BR_WIDTH), F32)
    for j in range(D_CONV):
        off = HALO - (D_CONV - 1) + j
        k = off % SUBLANES
        acc = acc + w_ref[j:j + 1, :] * ext_sc[k, off - k:off - k + tm, :]
    o_ref[...] = _ln_swish(acc + b_ref[...], lg_ref[...], lb_ref[...]).astype(BF16)


def _conv(da3, cw, cb, lg, lb, *, tm):
    bsz, T, _ = da3.shape
    nt = T // tm
    hb = tm // HALO
    consts = [cw, cb, lg, lb]
    return pl.pallas_call(
        functools.partial(_conv_kernel, tm=tm),
        grid=(bsz, nt),
        in_specs=[pl.BlockSpec((None, tm, BR_WIDTH), lambda b, i: (b, i, 0)),
                  pl.BlockSpec((None, HALO, BR_WIDTH), lambda b, i: (b, jnp.maximum(i * hb - 1, 0), 0))]
        + [_full(c, 2) for c in consts],
        out_specs=pl.BlockSpec((tm, BR_WIDTH), lambda b, i: (b * nt + i, 0)),
        out_shape=jax.ShapeDtypeStruct((bsz * T, BR_WIDTH), BF16),
        scratch_shapes=[pltpu.VMEM((SUBLANES, HALO + tm, BR_WIDTH), F32)],
        compiler_params=_cparams(2),
        name="conv",
    )(da3, da3, *consts)


MIX_CW = 512


def _mix_kernel(xn_ref, oa_ref, ob_ref, oc_ref, od_ref, wg_ref, wbr_ref, wo_ref, hm_ref, acc_sc):
    i = pl.program_id(1)

    @pl.when(i == 0)
    def _():
        acc_sc[...] = jnp.zeros_like(acc_sc)

    xn = xn_ref[...]
    br = jnp.where(i == 0, oa_ref[...], jnp.where(i == 1, ob_ref[...], jnp.where(i == 2, oc_ref[...], od_ref[...])))
    for c in range(D_MODEL // MIX_CW):
        sl = slice(c * MIX_CW, (c + 1) * MIX_CW)
        contrib = _sigmoid(_dot_nt(xn, wg_ref[sl, :])) * _dot(br, wbr_ref[:, sl])
        acc_sc[:, sl] = acc_sc[:, sl] + contrib

    @pl.when(i == N_BRANCH - 1)
    def _():
        hm_ref[...] = _dot(acc_sc[...].astype(BF16), wo_ref[...])


def _mix(xn, oa, ob, ob_spec, oc, od, wg, wbr, wo, *, tm):
    n = xn.shape[0]
    row = lambda w: pl.BlockSpec((tm, w), lambda r, i: (r, 0))
    return pl.pallas_call(
        _mix_kernel,
        grid=(n // tm, N_BRANCH),
        in_specs=[row(D_MODEL), row(BR_WIDTH), ob_spec, row(BR_WIDTH), row(BR_WIDTH),
                  pl.BlockSpec((D_MODEL, D_MODEL), lambda r, i: (i, 0)),
                  pl.BlockSpec((None, BR_WIDTH, D_MODEL), lambda r, i: (i, 0, 0)),
                  pl.BlockSpec((D_MODEL, D_MODEL), lambda r, i: (0, 0))],
        out_specs=row(D_MODEL),
        out_shape=jax.ShapeDtypeStruct((n, D_MODEL), F32),
        scratch_shapes=[pltpu.VMEM((tm, D_MODEL), F32)],
        compiler_params=_cparams(2),
        name="mix",
    )(xn, oa, ob, oc, od, wg, wbr, wo)


MLP_FC = 1024


def _mlp_kernel(x_ref, hm_ref, g_ref, wu_ref, wd_ref, y_ref, hn_sc, acc_sc):
    j = pl.program_id(1)

    @pl.when(j == 0)
    def _():
        hn_sc[...] = _rms(x_ref[...] + hm_ref[...], g_ref[...]).astype(BF16)
        acc_sc[...] = jnp.zeros_like(acc_sc)

    up = _dot(hn_sc[...], wu_ref[...])
    act = jnp.square(jnp.maximum(up, 0.0)).astype(BF16)
    d = _dot(act, wd_ref[...])
    acc_sc[...] = acc_sc[...] + d

    @pl.when(j == pl.num_programs(1) - 1)
    def _():
        y_ref[...] = (x_ref[...] + hm_ref[...]) + acc_sc[...]


def _mlp(x2d, hm, g, wu, wd, *, tm):
    n = x2d.shape[0]
    row = pl.BlockSpec((tm, D_MODEL), lambda r, j: (r, 0))
    return pl.pallas_call(
        _mlp_kernel,
        grid=(n // tm, D_FF // MLP_FC),
        in_specs=[row, row, _full(g, 2),
                  pl.BlockSpec((D_MODEL, MLP_FC), lambda r, j: (0, j)),
                  pl.BlockSpec((MLP_FC, D_MODEL), lambda r, j: (j, 0))],
        out_specs=row,
        out_shape=jax.ShapeDtypeStruct((n, D_MODEL), F32),
        scratch_shapes=[pltpu.VMEM((tm, D_MODEL), BF16), pltpu.VMEM((tm, D_MODEL), F32)],
        compiler_params=_cparams(2),
        name="mlp",
    )(x2d, hm, g, wu, wd)


PAGES_PER_STEP = 16


def _tile_attention(s_ref, bias_ref, v_ref, n, s_new, bias_new, v_new, v_transposed):
    c2 = ATT_SCALE * LOG2E
    s = s_ref[...] * c2 + bias_ref[...]
    sn = s_new * c2 + bias_new
    m = jnp.maximum(jnp.max(jnp.max(s, axis=0), axis=1, keepdims=True), sn)
    e = jnp.exp2(s - m)
    en = jnp.exp2(sn - m)
    den = jnp.sum(jnp.sum(e, axis=0), axis=1, keepdims=True) + en
    acc = en.astype(BF16).astype(F32) * v_new.astype(BF16).astype(F32)
    mm = _dot_nt if v_transposed else _dot
    for c in range(n):
        acc = acc + mm(e[c].astype(BF16), v_ref[c * LANES:(c + 1) * LANES, :])
    return acc / den


KEY_TILES = 16


def _dsa_s_kernel(pt_ref, q_ref, iq_ref, w_ref, new_ref, *rest, pg, n_pages, n_sel):
    pages = rest[:pg]
    tri_ref, tril_ref, o_ref, s_sc, bias_sc, vt_sc, sc_sc, key_sc = rest[pg:]
    g = pl.program_id(1)
    q = q_ref[...].astype(BF16)
    iq = iq_ref[...].astype(BF16)
    w = w_ref[...]

    @pl.when(g == 0)
    def _():
        sc_sc[...] = jnp.full(sc_sc.shape, -jnp.inf, F32)

    for i in range(pg):
        c = g * pg + i
        page = pages[i]
        s_sc[c] = _dot(q, page[0:128, :].astype(BF16))
        vt_sc[pl.ds(pl.multiple_of(c * LANES, LANES), LANES), :] = page[128:256, :].astype(BF16)
        lg = _dot(iq, page[192:320, :].astype(BF16))
        sc_sc[g * (pg // SUBLANES) + i // SUBLANES, i % SUBLANES:i % SUBLANES + 1, :] = jnp.sum(
            jnp.maximum(lg, 0.0) * w, axis=0, keepdims=True)

    @pl.when(g == pl.num_programs(1) - 1)
    def _():
        new = new_ref[...]
        k_new = new[0:1, 0:128].astype(BF16).astype(F32)
        s_new = jnp.sum(q.astype(F32) * k_new, axis=-1, keepdims=True)
        ik_new = new[0:1, 256:320].astype(BF16).astype(F32)
        lg_new = jnp.sum(iq[:, A_IDX_DIM:2 * A_IDX_DIM].astype(F32) * ik_new, axis=-1, keepdims=True)
        sc_new = jnp.sum(jnp.maximum(lg_new, 0.0) * w[:, 0:1], axis=0, keepdims=True)
        t_new = n_pages // SUBLANES
        first = (lax.broadcasted_iota(I32, (SUBLANES, LANES), 0) == 0) & (lax.broadcasted_iota(I32, (SUBLANES, LANES), 1) == 0)
        sc_sc[t_new] = jnp.where(first, sc_new, -jnp.inf)
        key_sc[...] = _sortable(sc_sc[...])
        sel = _select_packed(key_sc, n_sel, tri_ref[...], tril_ref[...])
        for c in range(n_pages):
            row = sel[c // SUBLANES][c % SUBLANES:c % SUBLANES + 1, :]
            bias_sc[c] = jnp.where(jnp.broadcast_to(row, (SUBLANES, LANES)) > 0.5, 0.0, NEG)
        bias_new = jnp.where(sel[t_new][0:1, 0:1] > 0.5, 0.0, NEG)
        o_ref[...] = _tile_attention(s_sc, bias_sc, vt_sc, n_pages, s_new, bias_new, new[0:1, 128:256], True)


def _dsa_sample(page_table, cache_t, layer, q8, iq8, w8, new8, tri, tril):
    bsz, n_pages = page_table.shape
    pg = PAGES_PER_STEP
    assert pg % SUBLANES == 0 and n_pages % pg == 0 and n_pages * PAGE + 1 <= KEY_TILES * SUBLANES * LANES
    n_sel = min(A_TOPK, (n_pages * PAGE + 1) // 4)
    per_b = lambda r, w: pl.BlockSpec((None, r, w), lambda b, g, pt: (b, 0, 0))
    cst = lambda a: pl.BlockSpec(a.shape, lambda b, g, pt: (0,) * a.ndim)
    page_spec = lambda i: pl.BlockSpec((None, None, A_CACHE_DIM, PAGE),
                                       lambda b, g, pt: (layer, pt[b, g * pg + i], 0, 0))
    grid_spec = pltpu.PrefetchScalarGridSpec(
        num_scalar_prefetch=1,
        grid=(bsz, n_pages // pg),
        in_specs=[per_b(SUBLANES, LANES), per_b(SUBLANES, LANES), per_b(SUBLANES, LANES), per_b(SUBLANES, A_CACHE_DIM)]
        + [page_spec(i) for i in range(pg)] + [cst(tri), cst(tril)],
        out_specs=per_b(SUBLANES, HEAD_DIM),
        scratch_shapes=[pltpu.VMEM((n_pages, SUBLANES, LANES), F32), pltpu.VMEM((n_pages, SUBLANES, LANES), F32),
                        pltpu.VMEM((n_pages * LANES, PAGE), BF16),
                        pltpu.VMEM((KEY_TILES, SUBLANES, LANES), F32), pltpu.VMEM((KEY_TILES, SUBLANES, LANES), I32)],
    )
    return pl.pallas_call(
        functools.partial(_dsa_s_kernel, pg=pg, n_pages=n_pages, n_sel=n_sel),
        grid_spec=grid_spec,
        out_shape=jax.ShapeDtypeStruct((bsz, SUBLANES, HEAD_DIM), F32),
        compiler_params=_cparams(2),
        name="dsa_sample",
    )(page_table, q8, iq8, w8, new8, *([cache_t] * pg), tri, tril)


def _nsa_s_kernel(pt_ref, q_ref, g3_ref, new_ref, win_ref, wnew_ref, a_ref, w_ref, ov_ref, *rest,
                  pg, n_pages, n_cmp, n_blk, n_top, past):
    pages = rest[:pg]
    o_ref, ss_sc, bias_sc, vs_sc, lok_sc, hik_sc, lov_sc, hiv_sc = rest[pg:]
    g = pl.program_id(1)
    q = q_ref[...].astype(BF16)
    sub = PAGE // C_CMP_STRIDE

    for i in range(pg):
        c = g * pg + i
        page = pages[i]
        r0 = pl.multiple_of(c * sub, sub)
        lo, hi = _summaries(page[:, 0:128], a_ref[0], None, n_cmp)
        lok_sc[pl.ds(r0, sub), :] = lo
        hik_sc[pl.ds(r0, sub), :] = hi
        lo, hi = _summaries(page[:, 128:256], a_ref[1], None, n_cmp)
        lov_sc[pl.ds(r0, sub), :] = lo
        hiv_sc[pl.ds(r0, sub), :] = hi
        ss_sc[c] = _dot_nt(q, page[:, 256:384].astype(BF16))
        vs_sc[pl.ds(pl.multiple_of(c * LANES, LANES), LANES), :] = page[:, 384:512].astype(BF16)

    @pl.when(g == pl.num_programs(1) - 1)
    def _():
        qf = q.astype(F32)
        new = new_ref[...]
        lane = lax.broadcasted_iota(I32, (SUBLANES, LANES), 1)
        qpos = jnp.full((SUBLANES, 1), past, I32)
        kc = _finish_summaries(lok_sc[...], hik_sc[...], w_ref[0], n_cmp)
        vc = _finish_summaries(lov_sc[...], hiv_sc[...], w_ref[1], n_cmp)
        ncp = kc.shape[0]
        n_io = lax.broadcasted_iota(I32, (SUBLANES, ncp), 1)
        cvalid = (n_io < n_cmp) & (n_io * C_CMP_STRIDE + (C_CMP_LEN - 1) <= qpos)
        p = _cmp_softmax(q, kc, cvalid)
        o_c = _dot(p.astype(BF16), vc)
        head = lax.broadcasted_iota(I32, p.shape, 0) < C_HEADS
        psum = jnp.broadcast_to(jnp.sum(jnp.where(head, p, 0.0), axis=0, keepdims=True), p.shape)
        imp = _dot_split3(psum, ov_ref[...])
        sel = _select_rank(_block_scores(imp, qpos), n_blk, n_top)
        for c in range(n_pages):
            pick = jnp.where(lane < C_SLC_BLOCK, sel[:, 2 * c:2 * c + 1], sel[:, 2 * c + 1:2 * c + 2])
            bias_sc[c] = jnp.where(pick > 0.5, 0.0, NEG)
        k_new = new[0:1, 256:384].astype(BF16).astype(F32)
        s_new = jnp.sum(qf * k_new, axis=-1, keepdims=True)
        bias_new = jnp.where(sel[:, 2 * n_pages:2 * n_pages + 1] > 0.5, 0.0, NEG)
        o_s = _tile_attention(ss_sc, bias_sc, vs_sc, n_pages, s_new, bias_new, new[0:1, 384:512], False)
        wb = win_ref.shape[0]
        kw = win_ref[:, 0:128].astype(BF16)
        vw = win_ref[:, 128:256].astype(BF16)
        wnew = wnew_ref[...]
        s_w = _dot_nt(q, kw) * ATT_SCALE
        dist = wb - lax.broadcasted_iota(I32, (SUBLANES, wb), 1)
        wvalid = (dist <= C_WINDOW) & (past - dist >= 0)
        s_w = jnp.where(wvalid, s_w, NEG)
        s_n = jnp.sum(qf * wnew[0:1, 0:128].astype(BF16).astype(F32), axis=-1, keepdims=True) * ATT_SCALE
        m = jnp.maximum(jnp.max(s_w, axis=-1, keepdims=True), s_n)
        e_w = jnp.where(wvalid, jnp.exp(s_w - m), 0.0)
        e_n = jnp.exp(s_n - m)
        den = jnp.sum(e_w, axis=-1, keepdims=True) + e_n
        v_n = wnew[0:1, 128:256].astype(BF16).astype(F32)
        o_w = (_dot(e_w.astype(BF16), vw) + e_n.astype(BF16).astype(F32) * v_n) / den
        o_ref[...] = g3_ref[0] * o_c + g3_ref[1] * o_s + g3_ref[2] * o_w


def _nsa_sample(page_table, cache, layer, q8, g3, new8, win, wnew8, cmp_a, cmp_w, ov):
    bsz, n_pages = page_table.shape
    pg = PAGES_PER_STEP
    past = n_pages * PAGE
    n_cmp = (past + 1 - C_CMP_LEN) // C_CMP_STRIDE + 1
    n_blk = -(-(past + 1) // C_SLC_BLOCK)
    ns = past // C_CMP_STRIDE
    wb = win.shape[2]
    cst = lambda a: pl.BlockSpec(a.shape, lambda b, g, pt: (0,) * a.ndim)
    per_b = lambda r, w: pl.BlockSpec((None, r, w), lambda b, g, pt: (b, 0, 0))
    page_spec = lambda i: pl.BlockSpec((None, None, PAGE, C_CACHE_DIM),
                                       lambda b, g, pt: (layer, pt[b, g * pg + i], 0, 0))
    grid_spec = pltpu.PrefetchScalarGridSpec(
        num_scalar_prefetch=1,
        grid=(bsz, n_pages // pg),
        in_specs=[per_b(SUBLANES, LANES),
                  pl.BlockSpec((None, 3, SUBLANES, LANES), lambda b, g, pt: (b, 0, 0, 0)),
                  per_b(SUBLANES, C_CACHE_DIM),
                  pl.BlockSpec((None, None, wb, C_WIN_DIM), lambda b, g, pt: (layer, b, 0, 0)),
                  per_b(SUBLANES, C_WIN_DIM), cst(cmp_a), cst(cmp_w), cst(ov)]
        + [page_spec(i) for i in range(pg)],
        out_specs=per_b(SUBLANES, HEAD_DIM),
        scratch_shapes=[pltpu.VMEM((n_pages, SUBLANES, LANES), F32), pltpu.VMEM((n_pages, SUBLANES, LANES), F32),
                        pltpu.VMEM((n_pages * LANES, HEAD_DIM), BF16)]
        + [pltpu.VMEM((ns, HEAD_DIM), F32)] * 4,
    )
    return pl.pallas_call(
        functools.partial(_nsa_s_kernel, pg=pg, n_pages=n_pages, n_cmp=n_cmp, n_blk=n_blk, n_top=min(C_TOPN, n_blk),
                          past=past),
        grid_spec=grid_spec,
        out_shape=jax.ShapeDtypeStruct((bsz, SUBLANES, HEAD_DIM), F32),
        compiler_params=_cparams(2),
        name="nsa_sample",
    )(page_table, q8, g3, new8, win, wnew8, cmp_a, cmp_w, ov, *([cache] * pg))


def _step_kernel(u_ref, h0r_ref, h0i_ref, bdr_h_ref, bdr_l_ref, bdi_h_ref, bdi_l_ref, ar_ref, ai_ref,
                 cdr_ref, cdi_ref, d_ref, gw_ref, gb_ref, da_ref, cst_ref, cw_ref, cb_ref, lg_ref, lb_ref,
                 ob_ref, hr_ref, hi_ref, od_ref):
    u = u_ref[...]
    uh = u.astype(BF16)
    ul = (u - uh.astype(F32)).astype(BF16)

    def bmat(h_ref, l_ref):
        return _dot(uh, h_ref[...]) + (_dot(uh, l_ref[...]) + _dot(ul, h_ref[...]))

    ar, ai = ar_ref[...], ai_ref[...]
    h0r, h0i = h0r_ref[...], h0i_ref[...]
    hr = bmat(bdr_h_ref, bdr_l_ref) + (ar * h0r - ai * h0i)
    hi = bmat(bdi_h_ref, bdi_l_ref) + (ar * h0i + ai * h0r)
    hr_ref[...] = hr
    hi_ref[...] = hi
    y = _dot(hr.astype(BF16), cdr_ref[...]) - _dot(hi.astype(BF16), cdi_ref[...]) + d_ref[...] * u
    gl = _gelu(y)
    ob_ref[...] = (gl * _sigmoid(_dot(gl.astype(BF16), gw_ref[...]) + gb_ref[...])).astype(BF16)
    cw = cw_ref[...]
    y = jnp.sum(cst_ref[...] * cw[0:D_CONV - 1][None], axis=1) + cw[D_CONV - 1:D_CONV] * da_ref[...] + cb_ref[...]
    od_ref[...] = _ln_swish(y, lg_ref[...], lb_ref[...]).astype(BF16)


def _sample_step(u, h0r, h0i, sp, da, conv_state, cw, cb, lg, lb):
    bsz = u.shape[0]
    ins = [u, h0r, h0i, sp["bdr"], sp["bdr_lo"], sp["bdi"], sp["bdi_lo"], sp["ar"], sp["ai"], sp["cdr"], sp["cdi"],
           sp["d"], sp["glu_w"], sp["glu_b"], da, conv_state, cw, cb, lg, lb]
    outs = [((bsz, BR_WIDTH), BF16), ((bsz, B_LANES), F32), ((bsz, B_LANES), F32), ((bsz, BR_WIDTH), BF16)]
    return pl.pallas_call(
        _step_kernel,
        grid=(1,),
        in_specs=[_full(a, 1) for a in ins],
        out_specs=[pl.BlockSpec(o[0], lambda i: (0, 0)) for o in outs],
        out_shape=[jax.ShapeDtypeStruct(o[0], o[1]) for o in outs],
        compiler_params=_cparams(1),
        name="sample_step",
    )(*ins)


def _rope_tables(pos):
    pos = pos.astype(F32)[:, None]

    def tab(dim):
        half = dim // 2
        inv = ROPE_THETA ** (-jnp.arange(half, dtype=F32) / half)
        ang = pos * inv
        cos, sin = jnp.cos(ang), jnp.sin(ang)
        reps = LANES // dim
        return jnp.tile(jnp.concatenate([cos, cos], axis=1), (1, reps)), jnp.tile(jnp.concatenate([-sin, sin], axis=1), (1, reps))

    c128, s128 = tab(HEAD_DIM)
    c64, s64 = tab(A_IDX_DIM)
    return c128, s128, c64, s64


def _pack_w1(wt):
    def padded(a, b, rows):
        return jnp.pad(wt[a:b], ((0, rows - (b - a)), (0, 0)))
    parts = [wt[_O[0]:_O[4]], padded(_O[4], _O[5], 128), padded(_O[5], _O[6], 128), wt[_O[6]:_O[9]],
             padded(_O[9], _O[10], 128), wt[_O[10]:_O[11]]]
    return jnp.concatenate(parts, axis=0).astype(BF16)


def _s5_params(lam_re, lam_im, log_dt, b_re, b_im, c_re, c_im, d, glu_w, glu_b):
    lr, li = lam_re.astype(F32), lam_im.astype(F32)
    dt = jnp.exp(log_dt.astype(F32))[:, None]
    mag = jnp.exp(lr * dt)
    ar, ai = mag * jnp.cos(li * dt), mag * jnp.sin(li * dt)
    den = lr * lr + li * li
    fr = ((ar - 1.0) * lr + ai * li) / den
    fi = (ai * lr - (ar - 1.0) * li) / den
    br, bi = b_re.astype(F32), b_im.astype(F32)
    bbr = fr[..., None] * br - fi[..., None] * bi
    bbi = fr[..., None] * bi + fi[..., None] * br
    eye = jnp.eye(B_GROUPS, dtype=F32)
    bd = lambda m: jnp.einsum("gpc,gh->gchp", m, eye).reshape(BR_WIDTH, B_LANES)
    cd = lambda m: jnp.einsum("gcp,gh->gphc", m.astype(F32), eye).reshape(B_LANES, BR_WIDTH)
    bdr, bdi = bd(bbr), bd(bbi)
    hi_lo = lambda m: (m.astype(BF16), (m - m.astype(BF16).astype(F32)).astype(BF16))
    bdr_h, bdr_l = hi_lo(bdr)
    bdi_h, bdi_l = hi_lo(bdi)
    nsb = S5_BLOCKS
    diag_b = lambda m: jnp.stack([m[s * (BR_WIDTH // nsb):(s + 1) * (BR_WIDTH // nsb),
                                    s * (B_LANES // nsb):(s + 1) * (B_LANES // nsb)] for s in range(nsb)])
    diag_c = lambda m: jnp.stack([m[s * (B_LANES // nsb):(s + 1) * (B_LANES // nsb),
                                    s * (BR_WIDTH // nsb):(s + 1) * (BR_WIDTH // nsb)] for s in range(nsb)])
    return dict(bdr=bdr_h, bdr_lo=bdr_l, bdi=bdi_h, bdi_lo=bdi_l,
                bdr4=diag_b(bdr_h), bdi4=diag_b(bdi_h), cdr4=diag_c(cd(c_re).astype(BF16)), cdi4=diag_c(cd(c_im).astype(BF16)),
                ar=ar.reshape(1, B_LANES), ai=ai.reshape(1, B_LANES),
                cdr=cd(c_re).astype(BF16), cdi=cd(c_im).astype(BF16), d=d.astype(F32).reshape(1, BR_WIDTH),
                glu_w=glu_w.astype(BF16), glu_b=glu_b.astype(F32).reshape(1, BR_WIDTH))


def _overlap(n_cmp, n_blk, rows, cols):
    start = np.arange(n_cmp)[:, None] * C_CMP_STRIDE
    blk = np.arange(n_blk)[None, :]
    m = (start <= (blk + 1) * C_SLC_BLOCK - 1) & (start + C_CMP_LEN - 1 >= blk * C_SLC_BLOCK)
    out = np.zeros((rows, cols), np.float32)
    out[:n_cmp, :n_blk] = m
    return jnp.asarray(out, BF16)


def _expand(n_keys):
    e = (np.arange(LANES)[:, None] == (np.arange(n_keys)[None, :] // C_SLC_BLOCK)).astype(np.float32)
    return jnp.asarray(e, BF16)


def _tri(lower=False):
    i = np.arange(LANES)
    m = (i[:, None] > i[None, :]) if lower else (i[:, None] < i[None, :])
    return jnp.asarray(m.astype(np.float32), BF16)


def _make_consts(T, sb, past):
    n_cmp_p = (T - C_CMP_LEN) // C_CMP_STRIDE + 1
    n_blk_p = -(-T // C_SLC_BLOCK)
    n_cmp_s = (past + 1 - C_CMP_LEN) // C_CMP_STRIDE + 1
    n_blk_s = -(-(past + 1) // C_SLC_BLOCK)
    return dict(
        tabs_p=_rope_tables(jnp.arange(T)),
        tabs_s=_rope_tables(jnp.full((sb,), past)),
        tri=_tri(),
        tril=_tri(lower=True),
        ov_p=_overlap(n_cmp_p, n_blk_p, T // C_CMP_STRIDE, LANES),
        ex_p=_expand(T),
        ov_s=_overlap(n_cmp_s, n_blk_s, past // C_CMP_STRIDE, -(-n_blk_s // LANES) * LANES),
    )


def _pick_tile(n, cands):
    for c in cands:
        if n % c == 0:
            return c
    return n


def _pad_rows(a, rows):
    return jnp.pad(a[:, None, :], ((0, 0), (0, rows - 1), (0, 0)))


def _layer_weights(l, norm_mix, w_in, a_gq, a_gk, c_gq, c_gk, c_cmp_a, c_cmp_w, d_conv_w, d_conv_b, d_ln_g, d_ln_b,
                   w_br, w_o, norm_mlp, w_up, w_down):
    row = lambda v: v.astype(F32).reshape(1, -1)
    return dict(
        g_mix=row(norm_mix[l]), w1=_pack_w1(jnp.transpose(w_in[l])),
        wg=jnp.transpose(w_in[l])[_O[11]:_O[12]].astype(BF16),
        a_gq=row(a_gq[l]), a_gk=row(a_gk[l]), c_gq=row(c_gq[l]), c_gk=c_gk[l].astype(F32),
        cmp_a=c_cmp_a[l].astype(F32), cmp_w=c_cmp_w[l].astype(BF16),
        conv_w=jnp.pad(d_conv_w[l].astype(F32), ((0, 1), (0, 0))), conv_b=row(d_conv_b[l]),
        ln_g=row(d_ln_g[l]), ln_b=row(d_ln_b[l]),
        w_br=w_br[l].astype(BF16), w_o=w_o[l].astype(BF16), g_mlp=row(norm_mlp[l]),
        w_up=w_up[l].astype(BF16), w_down=w_down[l].astype(BF16))


def _prompt_layer(x, lw, sp, consts):
    bsz, T, _ = x.shape
    n = bsz * T
    x2d = x.reshape(n, D_MODEL)
    tm = _pick_tile(T, (256, 128))
    qb = _pick_tile(T, (256, 128))
    tp = _pick_tile(T, (512, 256, 128))
    nt = T // tp
    bu_spec = pl.BlockSpec((tp, BR_WIDTH), lambda i: (i % nt, i // nt))
    a_t = ((bsz, A_CACHE_DIM, T), pl.BlockSpec((None, A_CACHE_DIM, tp), lambda i: (i // nt, 0, i % nt)))
    (xn, aq, arow, aiq, aiw, bu, cq, crow, wrow, cg, da, arow_t) = _project(
        x2d, consts["tabs_p"], lw, tm=tp, n_pos_blocks=nt, bu_shape=(T, bsz * BR_WIDTH), bu_spec=bu_spec, a_rows_t=a_t)
    arow3 = arow.reshape(bsz, T, A_CACHE_DIM)
    crow3 = crow.reshape(bsz, T, C_CACHE_DIM)
    wrow3 = wrow.reshape(bsz, T, C_WIN_DIM)
    da3 = da.reshape(bsz, T, BR_WIDTH)
    o_a = _dsa(aq, aiq, aiw, arow3, consts["tri"], qb=qb)
    kcmp, vcmp = _compress(crow3, lw["cmp_a"], lw["cmp_w"])
    o_c = _nsa(cq, cg, crow3, wrow3, kcmp, vcmp, consts["ov_p"], consts["ex_p"], qb=qb)
    tc = _pick_tile(T, (128, 64))
    o_b, hr, hi = _s5(bu.reshape(T * bsz, BR_WIDTH), sp, nb=bsz, tc=tc)
    o_d = _conv(da3, lw["conv_w"], lw["conv_b"], lw["ln_g"], lw["ln_b"], tm=tm)
    tmx = _pick_tile(T, (512, 256, 128))
    ntx = T // tmx
    ob_spec = pl.BlockSpec((tmx, BR_WIDTH), lambda r, i: (r % ntx, r // ntx))
    hm = _mix(xn, o_a, o_b.reshape(T, bsz * BR_WIDTH), ob_spec, o_c, o_d, lw["wg"], lw["w_br"], lw["w_o"], tm=tmx)
    y = _mlp(x2d, hm, lw["g_mlp"], lw["w_up"], lw["w_down"], tm=tmx)
    wk = min(C_WINDOW, T)
    return (y.reshape(bsz, T, D_MODEL), jnp.swapaxes(arow_t, 1, 2), crow3, wrow3[:, T - wk:],
            hr.reshape(bsz, B_GROUPS, B_STATE), hi.reshape(bsz, B_GROUPS, B_STATE), da3[:, T - (D_CONV - 1):])


def _sample_layer(x, l, cache_a, cache_c, cache_c_win, h_re, h_im, conv_l, page_table, lw, sp, consts):
    bsz = x.shape[0]
    cache_a_t = jnp.swapaxes(cache_a, 2, 3)
    x2d = x.reshape(bsz, D_MODEL)
    row = lambda w: pl.BlockSpec((bsz, w), lambda i: (0, 0))
    (xn, aq, arow, aiq, aiw, bu, cq, crow, wrow, cg, da) = _project(
        x2d, consts["tabs_s"], lw, tm=bsz, n_pos_blocks=1, bu_shape=(bsz, BR_WIDTH), bu_spec=row(BR_WIDTH))
    q8 = jnp.pad(aq.astype(F32).reshape(bsz, A_HEADS, HEAD_DIM), ((0, 0), (0, SUBLANES - A_HEADS), (0, 0)))
    iq8 = jnp.sum(aiq.astype(F32).reshape(bsz, A_IDX_HEADS, 2, A_IDX_DIM), axis=2)
    iq8 = jnp.pad(iq8, ((0, 0), (0, 0), (LANES - A_IDX_DIM, 0)))
    w8 = jnp.broadcast_to(aiw[:, :A_IDX_HEADS, None], (bsz, A_IDX_HEADS, LANES))
    o_a = _dsa_sample(page_table, cache_a_t, l, q8, iq8, w8, _pad_rows(arow, SUBLANES), consts["tri"], consts["tril"])
    o_a = o_a[:, :A_HEADS].reshape(bsz, BR_WIDTH).astype(BF16)
    cq8 = jnp.pad(cq.astype(F32).reshape(bsz, C_HEADS, HEAD_DIM), ((0, 0), (0, SUBLANES - C_HEADS), (0, 0)))
    g3 = jnp.transpose(cg[:, :3 * C_HEADS].reshape(bsz, C_HEADS, 3), (0, 2, 1))
    g3 = jnp.broadcast_to(jnp.pad(g3, ((0, 0), (0, 0), (0, SUBLANES - C_HEADS)))[..., None], (bsz, 3, SUBLANES, LANES))
    o_c = _nsa_sample(page_table, cache_c, l, cq8, g3, _pad_rows(crow, SUBLANES), cache_c_win,
                      _pad_rows(wrow, SUBLANES), lw["cmp_a"], lw["cmp_w"], consts["ov_s"])
    o_c = o_c[:, :C_HEADS].reshape(bsz, BR_WIDTH).astype(BF16)
    o_b, hr, hi, o_d = _sample_step(bu, h_re.reshape(bsz, B_LANES), h_im.reshape(bsz, B_LANES), sp, da, conv_l,
                                    lw["conv_w"], lw["conv_b"], lw["ln_g"], lw["ln_b"])
    hm = _mix(xn, o_a, o_b, pl.BlockSpec((bsz, BR_WIDTH), lambda r, i: (r, 0)), o_c, o_d,
              lw["wg"], lw["w_br"], lw["w_o"], tm=bsz)
    y = _mlp(x2d, hm, lw["g_mlp"], lw["w_up"], lw["w_down"], tm=bsz)
    new_win = jnp.concatenate([cache_c_win[l][:, 1:], wrow[:, None, :]], axis=1)
    new_conv = jnp.concatenate([conv_l[:, 1:], da[:, None, :]], axis=1)
    return (y.reshape(bsz, 1, D_MODEL), arow[:, None, :], crow[:, None, :], new_win,
            hr.reshape(bsz, B_GROUPS, B_STATE), hi.reshape(bsz, B_GROUPS, B_STATE), new_conv)


def kernel(x_prompt, x_sample, cache_a, cache_c, cache_c_win, state_b_re, state_b_im, state_d_conv, page_table, norm_mix, w_in, a_gq, a_gk, b_lam_re, b_lam_im, b_log_dt, b_b_re, b_b_im, b_c_re, b_c_im, b_d, b_glu_w, b_glu_b, c_gq, c_gk, c_cmp_a, c_cmp_w, d_conv_w, d_conv_b, d_ln_g, d_ln_b, w_br, w_o, norm_mlp, w_up, w_down):
    depth = w_in.shape[0]
    bsz, T, _ = x_prompt.shape
    sb, st, _ = x_sample.shape
    assert st == 1 and bsz == SUBLANES
    assert cache_a.shape[2] == PAGE and cache_c.shape[2] == PAGE
    consts = _make_consts(T, sb, page_table.shape[1] * PAGE)
    xp, xs = x_prompt, x_sample
    order_p = (0, 2, 4, 6, 7, 10)
    order_s = (1, 3, 5, 8, 9, 11)
    outs = [[] for _ in range(12)]
    for l in range(depth):
        lw = _layer_weights(l, norm_mix, w_in, a_gq, a_gk, c_gq, c_gk, c_cmp_a, c_cmp_w, d_conv_w, d_conv_b,
                            d_ln_g, d_ln_b, w_br, w_o, norm_mlp, w_up, w_down)
        sp = _s5_params(b_lam_re[l], b_lam_im[l], b_log_dt[l], b_b_re[l], b_b_im[l], b_c_re[l], b_c_im[l],
                        b_d[l], b_glu_w[l], b_glu_b[l])
        xp, *rp = _prompt_layer(xp, lw, sp, consts)
        xs, *rs = _sample_layer(xs, l, cache_a, cache_c, cache_c_win, state_b_re[l], state_b_im[l], state_d_conv[l],
                                page_table, lw, sp, consts)
        for k in range(6):
            outs[order_p[k]].append(rp[k])
            outs[order_s[k]].append(rs[k])
    return (xp, xs) + tuple(jnp.stack(o) for o in outs)
```

```python
import functools
import math

import numpy as np
import jax
import jax.numpy as jnp
from jax import lax
from jax.experimental import pallas as pl
from jax.experimental.pallas import tpu as pltpu

F32 = jnp.float32
BF16 = jnp.bfloat16
I32 = jnp.int32

D_MODEL = 2048
HEAD_DIM = 128
N_BRANCH = 4
BR_WIDTH = D_MODEL // N_BRANCH
ROPE_THETA = 10000.0
NORM_EPS = 1e-6
A_HEADS = BR_WIDTH // HEAD_DIM
A_IDX_HEADS = 8
A_IDX_DIM = 64
A_TOPK = 256
B_GROUP = 16
B_GROUPS = BR_WIDTH // B_GROUP
B_STATE = 64
B_LANES = B_GROUPS * B_STATE
C_HEADS = BR_WIDTH // HEAD_DIM
C_CMP_STRIDE = 16
C_CMP_LEN = 2 * C_CMP_STRIDE
C_SLC_BLOCK = 64
C_TOPN = 16
C_WINDOW = 512
C_FORCE = 1e4
D_CONV = 31
D_FF = 4 * D_MODEL
A_CACHE_DIM = 2 * HEAD_DIM + A_IDX_DIM
C_CACHE_DIM = 4 * HEAD_DIM
C_WIN_DIM = 2 * HEAD_DIM
PAGE = 128

LANES = 128
SUBLANES = 8
VMEM_LIMIT_MB = 56

_W = (A_HEADS * HEAD_DIM, HEAD_DIM, HEAD_DIM, A_IDX_HEADS * A_IDX_DIM, A_IDX_DIM, A_IDX_HEADS,
      BR_WIDTH, C_HEADS * HEAD_DIM, 6 * HEAD_DIM, 3 * C_HEADS, 2 * BR_WIDTH, N_BRANCH * D_MODEL)
_O = tuple(int(v) for v in np.cumsum((0,) + _W))
_P = {}
_cur = 0
for _name, _w in (("aq", 512), ("ak", 128), ("av", 128), ("aiq", 512), ("aik", 128), ("aiw", 128),
                  ("bu", 512), ("cq", 512), ("ckv", 768), ("cg", 128), ("dglu", 1024)):
    _P[_name] = (_cur, _w)
    _cur += _w
P_TOTAL = _cur

NEG = -1e30
ATT_SCALE = HEAD_DIM ** -0.5
SIGN = -2 ** 31


def _cparams(n_axes):
    return pltpu.CompilerParams(dimension_semantics=("arbitrary",) * n_axes,
                                vmem_limit_bytes=VMEM_LIMIT_MB * 1024 * 1024)


def _full(a, n_grid):
    nd = a.ndim
    return pl.BlockSpec(a.shape, lambda *_: (0,) * nd)


def _dot(a, b):
    return jnp.dot(a, b, preferred_element_type=F32)


def _dot_nt(a, b):
    return lax.dot_general(a, b, (((1,), (1,)), ((), ())), preferred_element_type=F32)


def _dot_split3(p, m):
    hi = p.astype(BF16)
    r = p - hi.astype(F32)
    mid = r.astype(BF16)
    lo = (r - mid.astype(F32)).astype(BF16)
    return _dot(hi, m) + _dot(mid, m) + _dot(lo, m)


def _rms(x, g):
    return x * lax.rsqrt(jnp.mean(x * x, axis=-1, keepdims=True) + NORM_EPS) * g


def _sigmoid(x):
    return 1.0 / (1.0 + jnp.exp(-x))


def _gelu(x):
    return x * (0.5 * (1.0 + jnp.tanh(math.sqrt(2.0 / math.pi) * (x + 0.044715 * (x * x * x)))))


def _sortable(x):
    b = pltpu.bitcast(x + 0.0, I32)
    return jnp.where(b < 0, b ^ jnp.int32(0x7FFFFFFF), b)


def _kth_key(key_ref, k, red_axes):
    shp = tuple(1 if a in red_axes else s for a, s in enumerate(key_ref.shape))

    def count(mask):
        c = jnp.where(mask, 1.0, 0.0)
        for a in sorted(red_axes):
            c = jnp.sum(c, axis=a, keepdims=True)
        return c

    def body(it, tu):
        cand_u = tu | jnp.left_shift(jnp.int32(1), 31 - it)
        cand_s = cand_u ^ jnp.int32(SIGN)
        return jnp.where(count(key_ref[...] >= cand_s) >= k, cand_u, tu)

    tu = lax.fori_loop(0, 32, body, jnp.zeros(shp, I32), unroll=4)
    return tu ^ jnp.int32(SIGN), count


def _select_rows(key_ref, k, tri):
    ts, count = _kth_key(key_ref, k, (1,))
    keys = key_ref[...]
    gt = keys > ts
    need = k - count(gt)
    eqf = jnp.where(keys == ts, 1.0, 0.0)
    base = jnp.zeros_like(need)
    pieces = []
    for c in range(keys.shape[1] // LANES):
        ch = eqf[:, c * LANES:(c + 1) * LANES]
        pref = _dot(ch.astype(BF16), tri) + base
        pieces.append(jnp.where(pref < need, ch, 0.0))
        base = base + jnp.sum(ch, axis=-1, keepdims=True)
    sel_eq = pieces[0] if len(pieces) == 1 else jnp.concatenate(pieces, axis=1)
    return jnp.where(gt, 1.0, sel_eq)


def _select_packed(key_ref, k, tri, tril):
    shp = key_ref.shape
    ts, count = _kth_key(key_ref, k, (0, 1, 2))
    keys = key_ref[...]
    gt = keys > ts
    need = k - count(gt)
    eqf = jnp.where(keys == ts, 1.0, 0.0)
    eq2 = eqf.reshape(LANES, LANES)
    within = _dot(eq2.astype(BF16), tri)
    tot = jnp.broadcast_to(jnp.sum(eq2, axis=1, keepdims=True), (LANES, LANES))
    base = _dot(tril, tot.astype(BF16))
    sel_eq = jnp.where((within + base).reshape(shp) < need, eqf, 0.0)
    return jnp.where(gt, 1.0, sel_eq)


def _select_rank(sc, n, k):
    lane = lax.broadcasted_iota(I32, sc.shape, 1)
    rank = jnp.zeros(sc.shape, F32)
    for i in range(n):
        col = sc[:, i:i + 1]
        rank = rank + jnp.where(lane > i, jnp.where(col >= sc, 1.0, 0.0), jnp.where(col > sc, 1.0, 0.0))
    return jnp.where(rank < k, 1.0, 0.0)


LOG2E = 1.4426950408889634


def _masked_attn(qh, k, v, maskf):
    s = jnp.where(maskf > 0.5, _dot_nt(qh, k) * (ATT_SCALE * LOG2E), NEG)
    e = jnp.exp2(s - jnp.max(s, axis=-1, keepdims=True)) * maskf
    den = jnp.maximum(jnp.sum(e, axis=-1, keepdims=True), 1e-30)
    return _dot(e.astype(BF16), v) / den


def _proj_kernel(x_ref, g_ref, w_ref, c128_ref, s128_ref, c64_ref, s64_ref, gqa_ref, gka_ref, gqc_ref, gkc_ref,
                 xn_ref, aq_ref, arow_ref, aiq_ref, aiw_ref, bu_ref, cq_ref, crow_ref, wrow_ref, cg_ref, da_ref,
                 arow_t_ref=None):
    xn = _rms(x_ref[...], g_ref[...]).astype(BF16)
    xn_ref[...] = xn
    cos, sin = c128_ref[...], s128_ref[...]
    cos64, sin64 = c64_ref[...], s64_ref[...]
    lane = lax.broadcasted_iota(I32, cos.shape, 1)
    lo32 = (lane & 63) < 32
    lo64 = lane < 64

    def seg(name):
        a, w = _P[name]
        return _dot_nt(xn, w_ref[a:a + w, :])

    def rope128(v):
        return v * cos + pltpu.roll(v, 64, 1) * sin

    def rope64(v):
        rot = jnp.where(lo32, pltpu.roll(v, 96, 1), pltpu.roll(v, 32, 1))
        return v * cos64 + rot * sin64

    z = seg("aq")
    for h in range(A_HEADS):
        sl = slice(h * HEAD_DIM, (h + 1) * HEAD_DIM)
        aq_ref[:, sl] = rope128(_rms(z[:, sl], gqa_ref[...])).astype(BF16)
    ak = rope128(_rms(seg("ak"), gka_ref[...]))
    av = seg("av")
    aik = rope64(seg("aik"))
    arow_ref[:, 0:128] = ak
    arow_ref[:, 128:256] = av
    arow_ref[:, 256:320] = aik[:, 0:A_IDX_DIM]
    if arow_t_ref is not None:
        arow_t_ref[0:128, :] = ak.T
        arow_t_ref[128:256, :] = av.T
        arow_t_ref[256:320, :] = aik.T[0:A_IDX_DIM, :]
    z = seg("aiq")
    for j in range(A_IDX_HEADS // 2):
        r = rope64(z[:, j * LANES:(j + 1) * LANES])
        aiq_ref[:, (2 * j) * LANES:(2 * j + 1) * LANES] = jnp.where(lo64, r, 0.0).astype(BF16)
        aiq_ref[:, (2 * j + 1) * LANES:(2 * j + 2) * LANES] = jnp.where(lo64, 0.0, r).astype(BF16)
    aiw_ref[...] = seg("aiw") * (A_IDX_HEADS ** -0.5) * (A_IDX_DIM ** -0.5)
    bu_ref[...] = seg("bu")
    z = seg("cq")
    for h in range(C_HEADS):
        sl = slice(h * HEAD_DIM, (h + 1) * HEAD_DIM)
        cq_ref[:, sl] = rope128(_rms(z[:, sl], gqc_ref[...])).astype(BF16)
    z = seg("ckv")
    for br in range(3):
        kk = rope128(_rms(z[:, (2 * br) * LANES:(2 * br + 1) * LANES], gkc_ref[br:br + 1, :]))
        vv = z[:, (2 * br + 1) * LANES:(2 * br + 2) * LANES]
        if br < 2:
            crow_ref[:, (2 * br) * LANES:(2 * br + 1) * LANES] = kk
            crow_ref[:, (2 * br + 1) * LANES:(2 * br + 2) * LANES] = vv
        else:
            wrow_ref[:, 0:LANES] = kk
            wrow_ref[:, LANES:2 * LANES] = vv
    cg_ref[...] = _sigmoid(seg("cg"))
    z = seg("dglu")
    da_ref[...] = z[:, 0:BR_WIDTH] * _sigmoid(z[:, BR_WIDTH:2 * BR_WIDTH])


def _project(x2d, tabs, lw, *, tm, n_pos_blocks, bu_shape, bu_spec, a_rows_t=None):
    n = x2d.shape[0]
    row = lambda w: pl.BlockSpec((tm, w), lambda i: (i, 0))
    tab = pl.BlockSpec((tm, LANES), lambda i: (i % n_pos_blocks, 0))
    ins = [x2d, lw["g_mix"], lw["w1"], tabs[0], tabs[1], tabs[2], tabs[3], lw["a_gq"], lw["a_gk"], lw["c_gq"], lw["c_gk"]]
    w_spec = pl.BlockSpec(ins[2].shape, lambda i: (0, 0), pipeline_mode=pl.Buffered(1))
    in_specs = [row(D_MODEL), _full(ins[1], 1), w_spec, tab, tab, tab, tab] + [_full(a, 1) for a in ins[7:]]
    outs = [((n, D_MODEL), BF16, row(D_MODEL)),
            ((n, 512), BF16, row(512)),
            ((n, A_CACHE_DIM), F32, row(A_CACHE_DIM)),
            ((n, 1024), BF16, row(1024)),
            ((n, LANES), F32, row(LANES)),
            (bu_shape, F32, bu_spec),
            ((n, 512), BF16, row(512)),
            ((n, C_CACHE_DIM), F32, row(C_CACHE_DIM)),
            ((n, C_WIN_DIM), F32, row(C_WIN_DIM)),
            ((n, LANES), F32, row(LANES)),
            ((n, BR_WIDTH), F32, row(BR_WIDTH))]
    if a_rows_t is not None:
        outs.append((a_rows_t[0], F32, a_rows_t[1]))
    return pl.pallas_call(
        _proj_kernel,
        grid=(n // tm,),
        in_specs=in_specs,
        out_specs=[o[2] for o in outs],
        out_shape=[jax.ShapeDtypeStruct(o[0], o[1]) for o in outs],
        compiler_params=_cparams(1),
        name="project",
    )(*ins)


def _dsa_kernel(aq_ref, aiq_ref, aiw_ref, arow_ref, tri_ref, o_ref, k_sc, v_sc, ik_sc, key_sc, *, qb, L, n_sel):
    qi = pl.program_id(1)

    @pl.when(qi == 0)
    def _():
        k_sc[...] = arow_ref[:, 0:128].astype(BF16)
        v_sc[...] = arow_ref[:, 128:256].astype(BF16)
        ik = arow_ref[:, 256:320]
        ik_sc[...] = jnp.concatenate([ik, ik], axis=1).astype(BF16)

    def body(le):
        qpos = qi * qb + lax.broadcasted_iota(I32, (qb, 1), 0)
        valid = lax.broadcasted_iota(I32, (qb, le), 1) <= qpos
        if le <= n_sel:
            mask = jnp.where(valid, 1.0, 0.0)
        else:
            w = aiw_ref[...]
            score = None
            for h in range(A_IDX_HEADS):
                lg = _dot_nt(aiq_ref[:, h * LANES:(h + 1) * LANES], ik_sc[0:le, :])
                t = jnp.maximum(lg, 0.0) * w[:, h:h + 1]
                score = t if score is None else score + t
            keys = key_sc.at[:, 0:le]
            keys[...] = _sortable(jnp.where(valid, score, -jnp.inf))
            mask = jnp.where(valid, _select_rows(keys, n_sel, tri_ref[...]), 0.0)
        q = aq_ref[...]
        for h in range(A_HEADS):
            sl = slice(h * HEAD_DIM, (h + 1) * HEAD_DIM)
            o_ref[:, sl] = _masked_attn(q[:, sl], k_sc[0:le, :], v_sc[0:le, :], mask).astype(BF16)

    _causal_branches(qi, qb, L, body)


def _causal_branches(qi, qb, L, body):
    step = max(qb, L // 8)
    if L % step:
        step = qb
    per = step // qb
    for j in range(L // step):
        @pl.when((qi >= j * per) & (qi < (j + 1) * per))
        def _(j=j):
            body((j + 1) * step)


def _dsa(aq, aiq, aiw, arow3, tri, *, qb):
    bsz, L, _ = arow3.shape
    nq = L // qb
    n_sel = min(A_TOPK, L // 4)
    row = lambda w: pl.BlockSpec((qb, w), lambda b, i: (b * nq + i, 0))
    return pl.pallas_call(
        functools.partial(_dsa_kernel, qb=qb, L=L, n_sel=n_sel),
        grid=(bsz, nq),
        in_specs=[row(512), row(1024), row(LANES),
                  pl.BlockSpec((None, L, A_CACHE_DIM), lambda b, i: (b, 0, 0)), _full(tri, 2)],
        out_specs=row(512),
        out_shape=jax.ShapeDtypeStruct((bsz * L, 512), BF16),
        scratch_shapes=[pltpu.VMEM((L, 128), BF16), pltpu.VMEM((L, 128), BF16), pltpu.VMEM((L, 128), BF16),
                        pltpu.VMEM((qb, L), I32)],
        compiler_params=_cparams(2),
        name="dsa",
    )(aq, aiq, aiw, arow3, tri)


def _summaries(x, a, w, n_cmp):
    ns = x.shape[0] // C_CMP_STRIDE
    x3 = x.reshape(ns, C_CMP_STRIDE, HEAD_DIM)
    lo = jnp.sum(x3 * a[0:C_CMP_STRIDE][None], axis=1)
    hi = jnp.sum(x3 * a[C_CMP_STRIDE:C_CMP_LEN][None], axis=1)
    return lo, hi


def _finish_summaries(lo, hi, w, n_cmp):
    ns = lo.shape[0]
    comb = lo + pltpu.roll(hi, ns - 1, 0)
    comb = jnp.where(lax.broadcasted_iota(I32, comb.shape, 0) < n_cmp, comb, 0.0)
    return _dot(comb.astype(BF16), w).astype(BF16)


def _cmp_kernel(x_ref, a_ref, w_ref, kc_ref, vc_ref, *, n_cmp):
    for t, out in ((0, kc_ref), (1, vc_ref)):
        lo, hi = _summaries(x_ref[:, t * LANES:(t + 1) * LANES], a_ref[t], w_ref[t], n_cmp)
        out[...] = _finish_summaries(lo, hi, w_ref[t], n_cmp)


def _compress(crow3, cmp_a, cmp_w):
    bsz, L, _ = crow3.shape
    ns = L // C_CMP_STRIDE
    n_cmp = (L - C_CMP_LEN) // C_CMP_STRIDE + 1
    out = pl.BlockSpec((None, ns, HEAD_DIM), lambda b: (b, 0, 0))
    return pl.pallas_call(
        functools.partial(_cmp_kernel, n_cmp=n_cmp),
        grid=(bsz,),
        in_specs=[pl.BlockSpec((None, L, C_CACHE_DIM), lambda b: (b, 0, 0)), _full(cmp_a, 1), _full(cmp_w, 1)],
        out_specs=[out, out],
        out_shape=[jax.ShapeDtypeStruct((bsz, ns, HEAD_DIM), BF16)] * 2,
        compiler_params=_cparams(1),
        name="compress",
    )(crow3, cmp_a, cmp_w)


def _block_scores(imp, qpos):
    j = lax.broadcasted_iota(I32, imp.shape, 1)
    cur = lax.shift_right_logical(qpos, 6)
    forced = (j == 0) | (j == cur) | (j == cur - 1)
    return jnp.where(j <= cur, jnp.where(forced, C_FORCE, imp), -jnp.inf)


def _cmp_softmax(qh, kc, cvalid):
    s = _dot_nt(qh, kc) * ATT_SCALE
    s = jnp.where(cvalid, s, NEG)
    m = jnp.max(s, axis=-1, keepdims=True)
    e = jnp.where(cvalid, jnp.exp(s - m), 0.0)
    return e / jnp.maximum(jnp.sum(e, axis=-1, keepdims=True), 1e-30)


def _nsa_kernel(cq_ref, cg_ref, crow_ref, wrow_ref, kc_ref, vc_ref, ov_ref, ex_ref, o_ref,
                ks_sc, vs_sc, kw_sc, vw_sc, os_sc, *, qb, L, n_cmp, n_blk, n_top, wsl):
    qi = pl.program_id(1)

    @pl.when(qi == 0)
    def _():
        ks_sc[...] = crow_ref[:, 256:384].astype(BF16)
        vs_sc[...] = crow_ref[:, 384:512].astype(BF16)
        kw_sc[...] = wrow_ref[:, 0:128].astype(BF16)
        vw_sc[...] = wrow_ref[:, 128:256].astype(BF16)

    q0 = qi * qb
    qpos = q0 + lax.broadcasted_iota(I32, (qb, 1), 0)
    q = cq_ref[...]
    g = cg_ref[...]
    kc, vc = kc_ref[...], vc_ref[...]
    ncp = kc.shape[0]
    n_io = lax.broadcasted_iota(I32, (qb, ncp), 1)
    cvalid = (n_io < n_cmp) & (n_io * C_CMP_STRIDE + (C_CMP_LEN - 1) <= qpos)
    o_c, psum = [], None
    for h in range(C_HEADS):
        p = _cmp_softmax(q[:, h * HEAD_DIM:(h + 1) * HEAD_DIM], kc, cvalid)
        o_c.append(_dot(p.astype(BF16), vc))
        psum = p if psum is None else psum + p
    imp = _dot_split3(psum, ov_ref[...])
    sel = _select_rank(_block_scores(imp, qpos), n_blk, n_top).astype(BF16)

    def selected(le):
        selk = _dot(sel, ex_ref[:, 0:le])
        smask = jnp.where(lax.broadcasted_iota(I32, (qb, le), 1) <= qpos, selk, 0.0)
        for h in range(C_HEADS):
            sl = slice(h * HEAD_DIM, (h + 1) * HEAD_DIM)
            os_sc[:, sl] = _masked_attn(q[:, sl], ks_sc[0:le, :], vs_sc[0:le, :], smask)

    _causal_branches(qi, qb, L, selected)
    start = pl.multiple_of(jnp.minimum(jnp.maximum(q0 - C_WINDOW, 0), L - wsl), qb)
    dist = qpos - (start + lax.broadcasted_iota(I32, (qb, wsl), 1))
    wmask = jnp.where((dist >= 0) & (dist <= C_WINDOW), 1.0, 0.0)
    kw = kw_sc[pl.ds(start, wsl), :]
    vw = vw_sc[pl.ds(start, wsl), :]
    for h in range(C_HEADS):
        sl = slice(h * HEAD_DIM, (h + 1) * HEAD_DIM)
        o_w = _masked_attn(q[:, sl], kw, vw, wmask)
        out = g[:, 3 * h:3 * h + 1] * o_c[h] + g[:, 3 * h + 1:3 * h + 2] * os_sc[:, sl] + g[:, 3 * h + 2:3 * h + 3] * o_w
        o_ref[:, sl] = out.astype(BF16)


def _nsa(cq, cg, crow3, wrow3, kcmp, vcmp, ov, ex, *, qb):
    bsz, L, _ = crow3.shape
    nq = L // qb
    ns = kcmp.shape[1]
    n_cmp = (L - C_CMP_LEN) // C_CMP_STRIDE + 1
    n_blk = -(-L // C_SLC_BLOCK)
    wsl = min(L, C_WINDOW + qb)
    row = lambda w: pl.BlockSpec((qb, w), lambda b, i: (b * nq + i, 0))
    per_b = lambda r, w: pl.BlockSpec((None, r, w), lambda b, i: (b, 0, 0))
    return pl.pallas_call(
        functools.partial(_nsa_kernel, qb=qb, L=L, n_cmp=n_cmp, n_blk=n_blk, n_top=min(C_TOPN, n_blk), wsl=wsl),
        grid=(bsz, nq),
        in_specs=[row(512), row(LANES), per_b(L, C_CACHE_DIM), per_b(L, C_WIN_DIM),
                  per_b(ns, HEAD_DIM), per_b(ns, HEAD_DIM), _full(ov, 2), _full(ex, 2)],
        out_specs=row(512),
        out_shape=jax.ShapeDtypeStruct((bsz * L, 512), BF16),
        scratch_shapes=[pltpu.VMEM((L, 128), BF16)] * 4 + [pltpu.VMEM((qb, 512), F32)],
        compiler_params=_cparams(2),
        name="nsa",
    )(cq, cg, crow3, wrow3, kcmp, vcmp, ov, ex)


S5_BLOCKS = 4


def _s5_kernel(u_ref, bdr_ref, bdi_ref, ar_ref, ai_ref, cdr_ref, cdi_ref, d_ref, gw_ref, gb_ref,
               o_ref, hr_ref, hi_ref, xr_sc, xi_sc, h_sc, *, tc, nb):
    i = pl.program_id(0)

    @pl.when(i == 0)
    def _():
        h_sc[...] = jnp.zeros_like(h_sc)

    u = u_ref[...]
    ub = u.astype(BF16)
    nsb = bdr_ref.shape[0]
    wu, wx = BR_WIDTH // nsb, B_LANES // nsb
    for sb in range(nsb):
        us = ub[:, sb * wu:(sb + 1) * wu]
        xr_sc[:, sb * wx:(sb + 1) * wx] = _dot(us, bdr_ref[sb])
        xi_sc[:, sb * wx:(sb + 1) * wx] = _dot(us, bdi_ref[sb])
    ar = jnp.broadcast_to(ar_ref[...], (nb, B_LANES))
    ai = jnp.broadcast_to(ai_ref[...], (nb, B_LANES))

    def step(t, carry):
        hr, hi = carry
        r0 = pl.multiple_of(t * nb, nb)
        nhr = ar * hr - ai * hi + xr_sc[pl.ds(r0, nb), :]
        nhi = ar * hi + ai * hr + xi_sc[pl.ds(r0, nb), :]
        xr_sc[pl.ds(r0, nb), :] = nhr
        xi_sc[pl.ds(r0, nb), :] = nhi
        return nhr, nhi

    hr, hi = lax.fori_loop(0, tc, step, (h_sc[0], h_sc[1]))
    h_sc[0] = hr
    h_sc[1] = hi
    hr_ref[...] = hr
    hi_ref[...] = hi
    ch = [_dot(xr_sc[:, sb * wx:(sb + 1) * wx].astype(BF16), cdr_ref[sb])
          - _dot(xi_sc[:, sb * wx:(sb + 1) * wx].astype(BF16), cdi_ref[sb]) for sb in range(nsb)]
    y = jnp.concatenate(ch, axis=1) + d_ref[...] * u
    gl = _gelu(y)
    o_ref[...] = (gl * _sigmoid(_dot(gl.astype(BF16), gw_ref[...]) + gb_ref[...])).astype(BF16)


def _s5(u_tm, sp, *, nb, tc):
    rows = u_tm.shape[0]
    r = tc * nb
    consts = [sp["bdr4"], sp["bdi4"], sp["ar"], sp["ai"], sp["cdr4"], sp["cdi4"], sp["d"], sp["glu_w"], sp["glu_b"]]
    st = pl.BlockSpec((nb, B_LANES), lambda i: (0, 0))
    return pl.pallas_call(
        functools.partial(_s5_kernel, tc=tc, nb=nb),
        grid=(rows // r,),
        in_specs=[pl.BlockSpec((r, BR_WIDTH), lambda i: (i, 0))] + [_full(c, 1) for c in consts],
        out_specs=[pl.BlockSpec((r, BR_WIDTH), lambda i: (i, 0)), st, st],
        out_shape=[jax.ShapeDtypeStruct((rows, BR_WIDTH), BF16),
                   jax.ShapeDtypeStruct((nb, B_LANES), F32), jax.ShapeDtypeStruct((nb, B_LANES), F32)],
        scratch_shapes=[pltpu.VMEM((r, B_LANES), F32), pltpu.VMEM((r, B_LANES), F32), pltpu.VMEM((2, nb, B_LANES), F32)],
        compiler_params=_cparams(1),
        name="s5",
    )(u_tm, *consts)


HALO = 32


def _ln_swish(y, g, b):
    yc = y - jnp.mean(y, axis=-1, keepdims=True)
    yn = yc * lax.rsqrt(jnp.mean(yc * yc, axis=-1, keepdims=True) + NORM_EPS) * g + b
    return yn * _sigmoid(yn)


def _conv_kernel(cur_ref, halo_ref, w_ref, b_ref, lg_ref, lb_ref, o_ref, ext_sc, *, tm):
    i = pl.program_id(1)
    ext_sc[0, 0:HALO, :] = jnp.where(i == 0, 0.0, halo_ref[...])
    ext_sc[0, HALO:HALO + tm, :] = cur_ref[...]
    n = HALO + tm - SUBLANES
    for k in range(1, SUBLANES):
        ext_sc[k, 0:n, :] = ext_sc[0, pl.ds(k, n), :]
    acc = jnp.zeros((tm, BR_WIDTH), F32)
    for j in range(D_CONV):
        off = HALO - (D_CONV - 1) + j
        k = off % SUBLANES
        acc = acc + w_ref[j:j + 1, :] * ext_sc[k, off - k:off - k + tm, :]
    o_ref[...] = _ln_swish(acc + b_ref[...], lg_ref[...], lb_ref[...]).astype(BF16)


def _conv(da3, cw, cb, lg, lb, *, tm):
    bsz, T, _ = da3.shape
    nt = T // tm
    hb = tm // HALO
    consts = [cw, cb, lg, lb]
    return pl.pallas_call(
        functools.partial(_conv_kernel, tm=tm),
        grid=(bsz, nt),
        in_specs=[pl.BlockSpec((None, tm, BR_WIDTH), lambda b, i: (b, i, 0)),
                  pl.BlockSpec((None, HALO, BR_WIDTH), lambda b, i: (b, jnp.maximum(i * hb - 1, 0), 0))]
        + [_full(c, 2) for c in consts],
        out_specs=pl.BlockSpec((tm, BR_WIDTH), lambda b, i: (b * nt + i, 0)),
        out_shape=jax.ShapeDtypeStruct((bsz * T, BR_WIDTH), BF16),
        scratch_shapes=[pltpu.VMEM((SUBLANES, HALO + tm, BR_WIDTH), F32)],
        compiler_params=_cparams(2),
        name="conv",
    )(da3, da3, *consts)


MIX_CW = 512


def _mix_kernel(xn_ref, oa_ref, ob_ref, oc_ref, od_ref, wg_ref, wbr_ref, wo_ref, hm_ref, acc_sc):
    i = pl.program_id(1)

    @pl.when(i == 0)
    def _():
        acc_sc[...] = jnp.zeros_like(acc_sc)

    xn = xn_ref[...]
    br = jnp.where(i == 0, oa_ref[...], jnp.where(i == 1, ob_ref[...], jnp.where(i == 2, oc_ref[...], od_ref[...])))
    for c in range(D_MODEL // MIX_CW):
        sl = slice(c * MIX_CW, (c + 1) * MIX_CW)
        contrib = _sigmoid(_dot_nt(xn, wg_ref[sl, :])) * _dot(br, wbr_ref[:, sl])
        acc_sc[:, sl] = acc_sc[:, sl] + contrib

    @pl.when(i == N_BRANCH - 1)
    def _():
        hm_ref[...] = _dot(acc_sc[...].astype(BF16), wo_ref[...])


def _mix(xn, oa, ob, ob_spec, oc, od, wg, wbr, wo, *, tm):
    n = xn.shape[0]
    row = lambda w: pl.BlockSpec((tm, w), lambda r, i: (r, 0))
    return pl.pallas_call(
        _mix_kernel,
        grid=(n // tm, N_BRANCH),
        in_specs=[row(D_MODEL), row(BR_WIDTH), ob_spec, row(BR_WIDTH), row(BR_WIDTH),
                  pl.BlockSpec((D_MODEL, D_MODEL), lambda r, i: (i, 0)),
                  pl.BlockSpec((None, BR_WIDTH, D_MODEL), lambda r, i: (i, 0, 0)),
                  pl.BlockSpec((D_MODEL, D_MODEL), lambda r, i: (0, 0))],
        out_specs=row(D_MODEL),
        out_shape=jax.ShapeDtypeStruct((n, D_MODEL), F32),
        scratch_shapes=[pltpu.VMEM((tm, D_MODEL), F32)],
        compiler_params=_cparams(2),
        name="mix",
    )(xn, oa, ob, oc, od, wg, wbr, wo)


MLP_FC = 1024


def _mlp_kernel(x_ref, hm_ref, g_ref, wu_ref, wd_ref, y_ref, hn_sc, acc_sc):
    j = pl.program_id(1)

    @pl.when(j == 0)
    def _():
        hn_sc[...] = _rms(x_ref[...] + hm_ref[...], g_ref[...]).astype(BF16)
        acc_sc[...] = jnp.zeros_like(acc_sc)

    up = _dot(hn_sc[...], wu_ref[...])
    act = jnp.square(jnp.maximum(up, 0.0)).astype(BF16)
    d = _dot(act, wd_ref[...])
    acc_sc[...] = acc_sc[...] + d

    @pl.when(j == pl.num_programs(1) - 1)
    def _():
        y_ref[...] = (x_ref[...] + hm_ref[...]) + acc_sc[...]


def _mlp(x2d, hm, g, wu, wd, *, tm):
    n = x2d.shape[0]
    row = pl.BlockSpec((tm, D_MODEL), lambda r, j: (r, 0))
    return pl.pallas_call(
        _mlp_kernel,
        grid=(n // tm, D_FF // MLP_FC),
        in_specs=[row, row, _full(g, 2),
                  pl.BlockSpec((D_MODEL, MLP_FC), lambda r, j: (0, j)),
                  pl.BlockSpec((MLP_FC, D_MODEL), lambda r, j: (j, 0))],
        out_specs=row,
        out_shape=jax.ShapeDtypeStruct((n, D_MODEL), F32),
        scratch_shapes=[pltpu.VMEM((tm, D_MODEL), BF16), pltpu.VMEM((tm, D_MODEL), F32)],
        compiler_params=_cparams(2),
        name="mlp",
    )(x2d, hm, g, wu, wd)


PAGES_PER_STEP = 16


def _tile_attention(s_ref, bias_ref, v_ref, n, s_new, bias_new, v_new, v_transposed):
    c2 = ATT_SCALE * LOG2E
    s = s_ref[...] * c2 + bias_ref[...]
    sn = s_new * c2 + bias_new
    m = jnp.maximum(jnp.max(jnp.max(s, axis=0), axis=1, keepdims=True), sn)
    e = jnp.exp2(s - m)
    en = jnp.exp2(sn - m)
    den = jnp.sum(jnp.sum(e, axis=0), axis=1, keepdims=True) + en
    acc = en.astype(BF16).astype(F32) * v_new.astype(BF16).astype(F32)
    mm = _dot_nt if v_transposed else _dot
    for c in range(n):
        acc = acc + mm(e[c].astype(BF16), v_ref[c * LANES:(c + 1) * LANES, :])
    return acc / den


KEY_TILES = 16


def _dsa_s_kernel(pt_ref, q_ref, iq_ref, w_ref, new_ref, *rest, pg, n_pages, n_sel):
    pages = rest[:pg]
    tri_ref, tril_ref, o_ref, s_sc, bias_sc, vt_sc, sc_sc, key_sc = rest[pg:]
    g = pl.program_id(1)
    q = q_ref[...].astype(BF16)
    iq = iq_ref[...].astype(BF16)
    w = w_ref[...]

    @pl.when(g == 0)
    def _():
        sc_sc[...] = jnp.full(sc_sc.shape, -jnp.inf, F32)

    for i in range(pg):
        c = g * pg + i
        page = pages[i]
        s_sc[c] = _dot(q, page[0:128, :].astype(BF16))
        vt_sc[pl.ds(pl.multiple_of(c * LANES, LANES), LANES), :] = page[128:256, :].astype(BF16)
        lg = _dot(iq, page[192:320, :].astype(BF16))
        sc_sc[g * (pg // SUBLANES) + i // SUBLANES, i % SUBLANES:i % SUBLANES + 1, :] = jnp.sum(
            jnp.maximum(lg, 0.0) * w, axis=0, keepdims=True)

    @pl.when(g == pl.num_programs(1) - 1)
    def _():
        new = new_ref[...]
        k_new = new[0:1, 0:128].astype(BF16).astype(F32)
        s_new = jnp.sum(q.astype(F32) * k_new, axis=-1, keepdims=True)
        ik_new = new[0:1, 256:320].astype(BF16).astype(F32)
        lg_new = jnp.sum(iq[:, A_IDX_DIM:2 * A_IDX_DIM].astype(F32) * ik_new, axis=-1, keepdims=True)
        sc_new = jnp.sum(jnp.maximum(lg_new, 0.0) * w[:, 0:1], axis=0, keepdims=True)
        t_new = n_pages // SUBLANES
        first = (lax.broadcasted_iota(I32, (SUBLANES, LANES), 0) == 0) & (lax.broadcasted_iota(I32, (SUBLANES, LANES), 1) == 0)
        sc_sc[t_new] = jnp.where(first, sc_new, -jnp.inf)
        key_sc[...] = _sortable(sc_sc[...])
        sel = _select_packed(key_sc, n_sel, tri_ref[...], tril_ref[...])
        for c in range(n_pages):
            row = sel[c // SUBLANES][c % SUBLANES:c % SUBLANES + 1, :]
            bias_sc[c] = jnp.where(jnp.broadcast_to(row, (SUBLANES, LANES)) > 0.5, 0.0, NEG)
        bias_new = jnp.where(sel[t_new][0:1, 0:1] > 0.5, 0.0, NEG)
        o_ref[...] = _tile_attention(s_sc, bias_sc, vt_sc, n_pages, s_new, bias_new, new[0:1, 128:256], True)


def _dsa_sample(page_table, cache_t, layer, q8, iq8, w8, new8, tri, tril):
    bsz, n_pages = page_table.shape
    pg = PAGES_PER_STEP
    assert pg % SUBLANES == 0 and n_pages % pg == 0 and n_pages * PAGE + 1 <= KEY_TILES * SUBLANES * LANES
    n_sel = min(A_TOPK, (n_pages * PAGE + 1) // 4)
    per_b = lambda r, w: pl.BlockSpec((None, r, w), lambda b, g, pt: (b, 0, 0))
    cst = lambda a: pl.BlockSpec(a.shape, lambda b, g, pt: (0,) * a.ndim)
    page_spec = lambda i: pl.BlockSpec((None, None, A_CACHE_DIM, PAGE),
                                       lambda b, g, pt: (layer, pt[b, g * pg + i], 0, 0))
    grid_spec = pltpu.PrefetchScalarGridSpec(
        num_scalar_prefetch=1,
        grid=(bsz, n_pages // pg),
        in_specs=[per_b(SUBLANES, LANES), per_b(SUBLANES, LANES), per_b(SUBLANES, LANES), per_b(SUBLANES, A_CACHE_DIM)]
        + [page_spec(i) for i in range(pg)] + [cst(tri), cst(tril)],
        out_specs=per_b(SUBLANES, HEAD_DIM),
        scratch_shapes=[pltpu.VMEM((n_pages, SUBLANES, LANES), F32), pltpu.VMEM((n_pages, SUBLANES, LANES), F32),
                        pltpu.VMEM((n_pages * LANES, PAGE), BF16),
                        pltpu.VMEM((KEY_TILES, SUBLANES, LANES), F32), pltpu.VMEM((KEY_TILES, SUBLANES, LANES), I32)],
    )
    return pl.pallas_call(
        functools.partial(_dsa_s_kernel, pg=pg, n_pages=n_pages, n_sel=n_sel),
        grid_spec=grid_spec,
        out_shape=jax.ShapeDtypeStruct((bsz, SUBLANES, HEAD_DIM), F32),
        compiler_params=_cparams(2),
        name="dsa_sample",
    )(page_table, q8, iq8, w8, new8, *([cache_t] * pg), tri, tril)


def _nsa_s_kernel(pt_ref, q_ref, g3_ref, new_ref, win_ref, wnew_ref, a_ref, w_ref, ov_ref, *rest,
                  pg, n_pages, n_cmp, n_blk, n_top, past):
    pages = rest[:pg]
    o_ref, ss_sc, bias_sc, vs_sc, lok_sc, hik_sc, lov_sc, hiv_sc = rest[pg:]
    g = pl.program_id(1)
    q = q_ref[...].astype(BF16)
    sub = PAGE // C_CMP_STRIDE

    for i in range(pg):
        c = g * pg + i
        page = pages[i]
        r0 = pl.multiple_of(c * sub, sub)
        lo, hi = _summaries(page[:, 0:128], a_ref[0], None, n_cmp)
        lok_sc[pl.ds(r0, sub), :] = lo
        hik_sc[pl.ds(r0, sub), :] = hi
        lo, hi = _summaries(page[:, 128:256], a_ref[1], None, n_cmp)
        lov_sc[pl.ds(r0, sub), :] = lo
        hiv_sc[pl.ds(r0, sub), :] = hi
        ss_sc[c] = _dot_nt(q, page[:, 256:384].astype(BF16))
        vs_sc[pl.ds(pl.multiple_of(c * LANES, LANES), LANES), :] = page[:, 384:512].astype(BF16)

    @pl.when(g == pl.num_programs(1) - 1)
    def _():
        qf = q.astype(F32)
        new = new_ref[...]
        lane = lax.broadcasted_iota(I32, (SUBLANES, LANES), 1)
        qpos = jnp.full((SUBLANES, 1), past, I32)
        kc = _finish_summaries(lok_sc[...], hik_sc[...], w_ref[0], n_cmp)
        vc = _finish_summaries(lov_sc[...], hiv_sc[...], w_ref[1], n_cmp)
        ncp = kc.shape[0]
        n_io = lax.broadcasted_iota(I32, (SUBLANES, ncp), 1)
        cvalid = (n_io < n_cmp) & (n_io * C_CMP_STRIDE + (C_CMP_LEN - 1) <= qpos)
        p = _cmp_softmax(q, kc, cvalid)
        o_c = _dot(p.astype(BF16), vc)
        head = lax.broadcasted_iota(I32, p.shape, 0) < C_HEADS
        psum = jnp.broadcast_to(jnp.sum(jnp.where(head, p, 0.0), axis=0, keepdims=True), p.shape)
        imp = _dot_split3(psum, ov_ref[...])
        sel = _select_rank(_block_scores(imp, qpos), n_blk, n_top)
        for c in range(n_pages):
            pick = jnp.where(lane < C_SLC_BLOCK, sel[:, 2 * c:2 * c + 1], sel[:, 2 * c + 1:2 * c + 2])
            bias_sc[c] = jnp.where(pick > 0.5, 0.0, NEG)
        k_new = new[0:1, 256:384].astype(BF16).astype(F32)
        s_new = jnp.sum(qf * k_new, axis=-1, keepdims=True)
        bias_new = jnp.where(sel[:, 2 * n_pages:2 * n_pages + 1] > 0.5, 0.0, NEG)
        o_s = _tile_attention(ss_sc, bias_sc, vs_sc, n_pages, s_new, bias_new, new[0:1, 384:512], False)
        wb = win_ref.shape[0]
        kw = win_ref[:, 0:128].astype(BF16)
        vw = win_ref[:, 128:256].astype(BF16)
        wnew = wnew_ref[...]
        s_w = _dot_nt(q, kw) * ATT_SCALE
        dist = wb - lax.broadcasted_iota(I32, (SUBLANES, wb), 1)
        wvalid = (dist <= C_WINDOW) & (past - dist >= 0)
        s_w = jnp.where(wvalid, s_w, NEG)
        s_n = jnp.sum(qf * wnew[0:1, 0:128].astype(BF16).astype(F32), axis=-1, keepdims=True) * ATT_SCALE
        m = jnp.maximum(jnp.max(s_w, axis=-1, keepdims=True), s_n)
        e_w = jnp.where(wvalid, jnp.exp(s_w - m), 0.0)
        e_n = jnp.exp(s_n - m)
        den = jnp.sum(e_w, axis=-1, keepdims=True) + e_n
        v_n = wnew[0:1, 128:256].astype(BF16).astype(F32)
        o_w = (_dot(e_w.astype(BF16), vw) + e_n.astype(BF16).astype(F32) * v_n) / den
        o_ref[...] = g3_ref[0] * o_c + g3_ref[1] * o_s + g3_ref[2] * o_w


def _nsa_sample(page_table, cache, layer, q8, g3, new8, win, wnew8, cmp_a, cmp_w, ov):
    bsz, n_pages = page_table.shape
    pg = PAGES_PER_STEP
    past = n_pages * PAGE
    n_cmp = (past + 1 - C_CMP_LEN) // C_CMP_STRIDE + 1
    n_blk = -(-(past + 1) // C_SLC_BLOCK)
    ns = past // C_CMP_STRIDE
    wb = win.shape[2]
    cst = lambda a: pl.BlockSpec(a.shape, lambda b, g, pt: (0,) * a.ndim)
    per_b = lambda r, w: pl.BlockSpec((None, r, w), lambda b, g, pt: (b, 0, 0))
    page_spec = lambda i: pl.BlockSpec((None, None, PAGE, C_CACHE_DIM),
                                       lambda b, g, pt: (layer, pt[b, g * pg + i], 0, 0))
    grid_spec = pltpu.PrefetchScalarGridSpec(
        num_scalar_prefetch=1,
        grid=(bsz, n_pages // pg),
        in_specs=[per_b(SUBLANES, LANES),
                  pl.BlockSpec((None, 3, SUBLANES, LANES), lambda b, g, pt: (b, 0, 0, 0)),
                  per_b(SUBLANES, C_CACHE_DIM),
                  pl.BlockSpec((None, None, wb, C_WIN_DIM), lambda b, g, pt: (layer, b, 0, 0)),
                  per_b(SUBLANES, C_WIN_DIM), cst(cmp_a), cst(cmp_w), cst(ov)]
        + [page_spec(i) for i in range(pg)],
        out_specs=per_b(SUBLANES, HEAD_DIM),
        scratch_shapes=[pltpu.VMEM((n_pages, SUBLANES, LANES), F32), pltpu.VMEM((n_pages, SUBLANES, LANES), F32),
                        pltpu.VMEM((n_pages * LANES, HEAD_DIM), BF16)]
        + [pltpu.VMEM((ns, HEAD_DIM), F32)] * 4,
    )
    return pl.pallas_call(
        functools.partial(_nsa_s_kernel, pg=pg, n_pages=n_pages, n_cmp=n_cmp, n_blk=n_blk, n_top=min(C_TOPN, n_blk),
                          past=past),
        grid_spec=grid_spec,
        out_shape=jax.ShapeDtypeStruct((bsz, SUBLANES, HEAD_DIM), F32),
        compiler_params=_cparams(2),
        name="nsa_sample",
    )(page_table, q8, g3, new8, win, wnew8, cmp_a, cmp_w, ov, *([cache] * pg))


def _step_kernel(u_ref, h0r_ref, h0i_ref, bdr_h_ref, bdr_l_ref, bdi_h_ref, bdi_l_ref, ar_ref, ai_ref,
                 cdr_ref, cdi_ref, d_ref, gw_ref, gb_ref, da_ref, cst_ref, cw_ref, cb_ref, lg_ref, lb_ref,
                 ob_ref, hr_ref, hi_ref, od_ref):
    u = u_ref[...]
    uh = u.astype(BF16)
    ul = (u - uh.astype(F32)).astype(BF16)

    def bmat(h_ref, l_ref):
        return _dot(uh, h_ref[...]) + (_dot(uh, l_ref[...]) + _dot(ul, h_ref[...]))

    ar, ai = ar_ref[...], ai_ref[...]
    h0r, h0i = h0r_ref[...], h0i_ref[...]
    hr = bmat(bdr_h_ref, bdr_l_ref) + (ar * h0r - ai * h0i)
    hi = bmat(bdi_h_ref, bdi_l_ref) + (ar * h0i + ai * h0r)
    hr_ref[...] = hr
    hi_ref[...] = hi
    y = _dot(hr.astype(BF16), cdr_ref[...]) - _dot(hi.astype(BF16), cdi_ref[...]) + d_ref[...] * u
    gl = _gelu(y)
    ob_ref[...] = (gl * _sigmoid(_dot(gl.astype(BF16), gw_ref[...]) + gb_ref[...])).astype(BF16)
    cw = cw_ref[...]
    y = jnp.sum(cst_ref[...] * cw[0:D_CONV - 1][None], axis=1) + cw[D_CONV - 1:D_CONV] * da_ref[...] + cb_ref[...]
    od_ref[...] = _ln_swish(y, lg_ref[...], lb_ref[...]).astype(BF16)


def _sample_step(u, h0r, h0i, sp, da, conv_state, cw, cb, lg, lb):
    bsz = u.shape[0]
    ins = [u, h0r, h0i, sp["bdr"], sp["bdr_lo"], sp["bdi"], sp["bdi_lo"], sp["ar"], sp["ai"], sp["cdr"], sp["cdi"],
           sp["d"], sp["glu_w"], sp["glu_b"], da, conv_state, cw, cb, lg, lb]
    outs = [((bsz, BR_WIDTH), BF16), ((bsz, B_LANES), F32), ((bsz, B_LANES), F32), ((bsz, BR_WIDTH), BF16)]
    return pl.pallas_call(
        _step_kernel,
        grid=(1,),
        in_specs=[_full(a, 1) for a in ins],
        out_specs=[pl.BlockSpec(o[0], lambda i: (0, 0)) for o in outs],
        out_shape=[jax.ShapeDtypeStruct(o[0], o[1]) for o in outs],
        compiler_params=_cparams(1),
        name="sample_step",
    )(*ins)


def _rope_tables(pos):
    pos = pos.astype(F32)[:, None]

    def tab(dim):
        half = dim // 2
        inv = ROPE_THETA ** (-jnp.arange(half, dtype=F32) / half)
        ang = pos * inv
        cos, sin = jnp.cos(ang), jnp.sin(ang)
        reps = LANES // dim
        return jnp.tile(jnp.concatenate([cos, cos], axis=1), (1, reps)), jnp.tile(jnp.concatenate([-sin, sin], axis=1), (1, reps))

    c128, s128 = tab(HEAD_DIM)
    c64, s64 = tab(A_IDX_DIM)
    return c128, s128, c64, s64


def _pack_w1(wt):
    def padded(a, b, rows):
        return jnp.pad(wt[a:b], ((0, rows - (b - a)), (0, 0)))
    parts = [wt[_O[0]:_O[4]], padded(_O[4], _O[5], 128), padded(_O[5], _O[6], 128), wt[_O[6]:_O[9]],
             padded(_O[9], _O[10], 128), wt[_O[10]:_O[11]]]
    return jnp.concatenate(parts, axis=0).astype(BF16)


def _s5_params(lam_re, lam_im, log_dt, b_re, b_im, c_re, c_im, d, glu_w, glu_b):
    lr, li = lam_re.astype(F32), lam_im.astype(F32)
    dt = jnp.exp(log_dt.astype(F32))[:, None]
    mag = jnp.exp(lr * dt)
    ar, ai = mag * jnp.cos(li * dt), mag * jnp.sin(li * dt)
    den = lr * lr + li * li
    fr = ((ar - 1.0) * lr + ai * li) / den
    fi = (ai * lr - (ar - 1.0) * li) / den
    br, bi = b_re.astype(F32), b_im.astype(F32)
    bbr = fr[..., None] * br - fi[..., None] * bi
    bbi = fr[..., None] * bi + fi[..., None] * br
    eye = jnp.eye(B_GROUPS, dtype=F32)
    bd = lambda m: jnp.einsum("gpc,gh->gchp", m, eye).reshape(BR_WIDTH, B_LANES)
    cd = lambda m: jnp.einsum("gcp,gh->gphc", m.astype(F32), eye).reshape(B_LANES, BR_WIDTH)
    bdr, bdi = bd(bbr), bd(bbi)
    hi_lo = lambda m: (m.astype(BF16), (m - m.astype(BF16).astype(F32)).astype(BF16))
    bdr_h, bdr_l = hi_lo(bdr)
    bdi_h, bdi_l = hi_lo(bdi)
    nsb = S5_BLOCKS
    diag_b = lambda m: jnp.stack([m[s * (BR_WIDTH // nsb):(s + 1) * (BR_WIDTH // nsb),
                                    s * (B_LANES // nsb):(s + 1) * (B_LANES // nsb)] for s in range(nsb)])
    diag_c = lambda m: jnp.stack([m[s * (B_LANES // nsb):(s + 1) * (B_LANES // nsb),
                                    s * (BR_WIDTH // nsb):(s + 1) * (BR_WIDTH // nsb)] for s in range(nsb)])
    return dict(bdr=bdr_h, bdr_lo=bdr_l, bdi=bdi_h, bdi_lo=bdi_l,
                bdr4=diag_b(bdr_h), bdi4=diag_b(bdi_h), cdr4=diag_c(cd(c_re).astype(BF16)), cdi4=diag_c(cd(c_im).astype(BF16)),
                ar=ar.reshape(1, B_LANES), ai=ai.reshape(1, B_LANES),
                cdr=cd(c_re).astype(BF16), cdi=cd(c_im).astype(BF16), d=d.astype(F32).reshape(1, BR_WIDTH),
                glu_w=glu_w.astype(BF16), glu_b=glu_b.astype(F32).reshape(1, BR_WIDTH))


def _overlap(n_cmp, n_blk, rows, cols):
    start = np.arange(n_cmp)[:, None] * C_CMP_STRIDE
    blk = np.arange(n_blk)[None, :]
    m = (start <= (blk + 1) * C_SLC_BLOCK - 1) & (start + C_CMP_LEN - 1 >= blk * C_SLC_BLOCK)
    out = np.zeros((rows, cols), np.float32)
    out[:n_cmp, :n_blk] = m
    return jnp.asarray(out, BF16)


def _expand(n_keys):
    e = (np.arange(LANES)[:, None] == (np.arange(n_keys)[None, :] // C_SLC_BLOCK)).astype(np.float32)
    return jnp.asarray(e, BF16)


def _tri(lower=False):
    i = np.arange(LANES)
    m = (i[:, None] > i[None, :]) if lower else (i[:, None] < i[None, :])
    return jnp.asarray(m.astype(np.float32), BF16)


def _make_consts(T, sb, past):
    n_cmp_p = (T - C_CMP_LEN) // C_CMP_STRIDE + 1
    n_blk_p = -(-T // C_SLC_BLOCK)
    n_cmp_s = (past + 1 - C_CMP_LEN) // C_CMP_STRIDE + 1
    n_blk_s = -(-(past + 1) // C_SLC_BLOCK)
    return dict(
        tabs_p=_rope_tables(jnp.arange(T)),
        tabs_s=_rope_tables(jnp.full((sb,), past)),
        tri=_tri(),
        tril=_tri(lower=True),
        ov_p=_overlap(n_cmp_p, n_blk_p, T // C_CMP_STRIDE, LANES),
        ex_p=_expand(T),
        ov_s=_overlap(n_cmp_s, n_blk_s, past // C_CMP_STRIDE, -(-n_blk_s // LANES) * LANES),
    )


def _pick_tile(n, cands):
    for c in cands:
        if n % c == 0:
            return c
    return n


def _pad_rows(a, rows):
    return jnp.pad(a[:, None, :], ((0, 0), (0, rows - 1), (0, 0)))


def _layer_weights(l, norm_mix, w_in, a_gq, a_gk, c_gq, c_gk, c_cmp_a, c_cmp_w, d_conv_w, d_conv_b, d_ln_g, d_ln_b,
                   w_br, w_o, norm_mlp, w_up, w_down):
    row = lambda v: v.astype(F32).reshape(1, -1)
    return dict(
        g_mix=row(norm_mix[l]), w1=_pack_w1(jnp.transpose(w_in[l])),
        wg=jnp.transpose(w_in[l])[_O[11]:_O[12]].astype(BF16),
        a_gq=row(a_gq[l]), a_gk=row(a_gk[l]), c_gq=row(c_gq[l]), c_gk=c_gk[l].astype(F32),
        cmp_a=c_cmp_a[l].astype(F32), cmp_w=c_cmp_w[l].astype(BF16),
        conv_w=jnp.pad(d_conv_w[l].astype(F32), ((0, 1), (0, 0))), conv_b=row(d_conv_b[l]),
        ln_g=row(d_ln_g[l]), ln_b=row(d_ln_b[l]),
        w_br=w_br[l].astype(BF16), w_o=w_o[l].astype(BF16), g_mlp=row(norm_mlp[l]),
        w_up=w_up[l].astype(BF16), w_down=w_down[l].astype(BF16))


def _prompt_layer(x, lw, sp, consts):
    bsz, T, _ = x.shape
    n = bsz * T
    x2d = x.reshape(n, D_MODEL)
    tm = _pick_tile(T, (256, 128))
    qb = _pick_tile(T, (256, 128))
    tp = _pick_tile(T, (512, 256, 128))
    nt = T // tp
    bu_spec = pl.BlockSpec((tp, BR_WIDTH), lambda i: (i % nt, i // nt))
    a_t = ((bsz, A_CACHE_DIM, T), pl.BlockSpec((None, A_CACHE_DIM, tp), lambda i: (i // nt, 0, i % nt)))
    (xn, aq, arow, aiq, aiw, bu, cq, crow, wrow, cg, da, arow_t) = _project(
        x2d, consts["tabs_p"], lw, tm=tp, n_pos_blocks=nt, bu_shape=(T, bsz * BR_WIDTH), bu_spec=bu_spec, a_rows_t=a_t)
    arow3 = arow.reshape(bsz, T, A_CACHE_DIM)
    crow3 = crow.reshape(bsz, T, C_CACHE_DIM)
    wrow3 = wrow.reshape(bsz, T, C_WIN_DIM)
    da3 = da.reshape(bsz, T, BR_WIDTH)
    o_a = _dsa(aq, aiq, aiw, arow3, consts["tri"], qb=128)
    kcmp, vcmp = _compress(crow3, lw["cmp_a"], lw["cmp_w"])
    o_c = _nsa(cq, cg, crow3, wrow3, kcmp, vcmp, consts["ov_p"], consts["ex_p"], qb=qb)
    tc = _pick_tile(T, (128, 64))
    o_b, hr, hi = _s5(bu.reshape(T * bsz, BR_WIDTH), sp, nb=bsz, tc=tc)
    o_d = _conv(da3, lw["conv_w"], lw["conv_b"], lw["ln_g"], lw["ln_b"], tm=tm)
    tmx = _pick_tile(T, (512, 256, 128))
    ntx = T // tmx
    ob_spec = pl.BlockSpec((tmx, BR_WIDTH), lambda r, i: (r % ntx, r // ntx))
    hm = _mix(xn, o_a, o_b.reshape(T, bsz * BR_WIDTH), ob_spec, o_c, o_d, lw["wg"], lw["w_br"], lw["w_o"], tm=tmx)
    y = _mlp(x2d, hm, lw["g_mlp"], lw["w_up"], lw["w_down"], tm=tmx)
    wk = min(C_WINDOW, T)
    return (y.reshape(bsz, T, D_MODEL), jnp.swapaxes(arow_t, 1, 2), crow3, wrow3[:, T - wk:],
            hr.reshape(bsz, B_GROUPS, B_STATE), hi.reshape(bsz, B_GROUPS, B_STATE), da3[:, T - (D_CONV - 1):])


def _sample_layer(x, l, cache_a, cache_c, cache_c_win, h_re, h_im, conv_l, page_table, lw, sp, consts):
    bsz = x.shape[0]
    cache_a_t = jnp.swapaxes(cache_a, 2, 3)
    x2d = x.reshape(bsz, D_MODEL)
    row = lambda w: pl.BlockSpec((bsz, w), lambda i: (0, 0))
    (xn, aq, arow, aiq, aiw, bu, cq, crow, wrow, cg, da) = _project(
        x2d, consts["tabs_s"], lw, tm=bsz, n_pos_blocks=1, bu_shape=(bsz, BR_WIDTH), bu_spec=row(BR_WIDTH))
    q8 = jnp.pad(aq.astype(F32).reshape(bsz, A_HEADS, HEAD_DIM), ((0, 0), (0, SUBLANES - A_HEADS), (0, 0)))
    iq8 = jnp.sum(aiq.astype(F32).reshape(bsz, A_IDX_HEADS, 2, A_IDX_DIM), axis=2)
    iq8 = jnp.pad(iq8, ((0, 0), (0, 0), (LANES - A_IDX_DIM, 0)))
    w8 = jnp.broadcast_to(aiw[:, :A_IDX_HEADS, None], (bsz, A_IDX_HEADS, LANES))
    o_a = _dsa_sample(page_table, cache_a_t, l, q8, iq8, w8, _pad_rows(arow, SUBLANES), consts["tri"], consts["tril"])
    o_a = o_a[:, :A_HEADS].reshape(bsz, BR_WIDTH).astype(BF16)
    cq8 = jnp.pad(cq.astype(F32).reshape(bsz, C_HEADS, HEAD_DIM), ((0, 0), (0, SUBLANES - C_HEADS), (0, 0)))
    g3 = jnp.transpose(cg[:, :3 * C_HEADS].reshape(bsz, C_HEADS, 3), (0, 2, 1))
    g3 = jnp.broadcast_to(jnp.pad(g3, ((0, 0), (0, 0), (0, SUBLANES - C_HEADS)))[..., None], (bsz, 3, SUBLANES, LANES))
    o_c = _nsa_sample(page_table, cache_c, l, cq8, g3, _pad_rows(crow, SUBLANES), cache_c_win,
                      _pad_rows(wrow, SUBLANES), lw["cmp_a"], lw["cmp_w"], consts["ov_s"])
    o_c = o_c[:, :C_HEADS].reshape(bsz, BR_WIDTH).astype(BF16)
    o_b, hr, hi, o_d = _sample_step(bu, h_re.reshape(bsz, B_LANES), h_im.reshape(bsz, B_LANES), sp, da, conv_l,
                                    lw["conv_w"], lw["conv_b"], lw["ln_g"], lw["ln_b"])
    hm = _mix(xn, o_a, o_b, pl.BlockSpec((bsz, BR_WIDTH), lambda r, i: (r, 0)), o_c, o_d,
              lw["wg"], lw["w_br"], lw["w_o"], tm=bsz)
    y = _mlp(x2d, hm, lw["g_mlp"], lw["w_up"], lw["w_down"], tm=bsz)
    new_win = jnp.concatenate([cache_c_win[l][:, 1:], wrow[:, None, :]], axis=1)
    new_conv = jnp.concatenate([conv_l[:, 1:], da[:, None, :]], axis=1)
    return (y.reshape(bsz, 1, D_MODEL), arow[:, None, :], crow[:, None, :], new_win,
            hr.reshape(bsz, B_GROUPS, B_STATE), hi.reshape(bsz, B_GROUPS, B_STATE), new_conv)


def kernel(x_prompt, x_sample, cache_a, cache_c, cache_c_win, state_b_re, state_b_im, state_d_conv, page_table, norm_mix, w_in, a_gq, a_gk, b_lam_re, b_lam_im, b_log_dt, b_b_re, b_b_im, b_c_re, b_c_im, b_d, b_glu_w, b_glu_b, c_gq, c_gk, c_cmp_a, c_cmp_w, d_conv_w, d_conv_b, d_ln_g, d_ln_b, w_br, w_o, norm_mlp, w_up, w_down):
    depth = w_in.shape[0]
    bsz, T, _ = x_prompt.shape
    sb, st, _ = x_sample.shape
    assert st == 1 and bsz == SUBLANES
    assert cache_a.shape[2] == PAGE and cache_c.shape[2] == PAGE
    consts = _make_consts(T, sb, page_table.shape[1] * PAGE)
    xp, xs = x_prompt, x_sample
    order_p = (0, 2, 4, 6, 7, 10)
    order_s = (1, 3, 5, 8, 9, 11)
    outs = [[] for _ in range(12)]
    for l in range(depth):
        lw = _layer_weights(l, norm_mix, w_in, a_gq, a_gk, c_gq, c_gk, c_cmp_a, c_cmp_w, d_conv_w, d_conv_b,
                            d_ln_g, d_ln_b, w_br, w_o, norm_mlp, w_up, w_down)
        sp = _s5_params(b_lam_re[l], b_lam_im[l], b_log_dt[l], b_b_re[l], b_b_im[l], b_c_re[l], b_c_im[l],
                        b_d[l], b_glu_w[l], b_glu_b[l])
        xp, *rp = _prompt_layer(xp, lw, sp, consts)
        xs, *rs = _sample_layer(xs, l, cache_a, cache_c, cache_c_win, state_b_re[l], state_b_im[l], state_d_conv[l],
                                page_table, lw, sp, consts)
        for k in range(6):
            outs[order_p[k]].append(rp[k])
            outs[order_s[k]].append(rs[k])
    return (xp, xs) + tuple(jnp.stack(o) for o in outs)
```

```python
import functools
import math

import numpy as np
import jax
import jax.numpy as jnp
from jax import lax
from jax.experimental import pallas as pl
from jax.experimental.pallas import tpu as pltpu

F32 = jnp.float32
BF16 = jnp.bfloat16
I32 = jnp.int32

D_MODEL = 2048
HEAD_DIM = 128
N_BRANCH = 4
BR_WIDTH = D_MODEL // N_BRANCH
ROPE_THETA = 10000.0
NORM_EPS = 1e-6
A_HEADS = BR_WIDTH // HEAD_DIM
A_IDX_HEADS = 8
A_IDX_DIM = 64
A_TOPK = 256
B_GROUP = 16
B_GROUPS = BR_WIDTH // B_GROUP
B_STATE = 64
B_LANES = B_GROUPS * B_STATE
C_HEADS = BR_WIDTH // HEAD_DIM
C_CMP_STRIDE = 16
C_CMP_LEN = 2 * C_CMP_STRIDE
C_SLC_BLOCK = 64
C_TOPN = 16
C_WINDOW = 512
C_FORCE = 1e4
D_CONV = 31
D_FF = 4 * D_MODEL
A_CACHE_DIM = 2 * HEAD_DIM + A_IDX_DIM
C_CACHE_DIM = 4 * HEAD_DIM
C_WIN_DIM = 2 * HEAD_DIM
PAGE = 128

LANES = 128
SUBLANES = 8
VMEM_LIMIT_MB = 56

_W = (A_HEADS * HEAD_DIM, HEAD_DIM, HEAD_DIM, A_IDX_HEADS * A_IDX_DIM, A_IDX_DIM, A_IDX_HEADS,
      BR_WIDTH, C_HEADS * HEAD_DIM, 6 * HEAD_DIM, 3 * C_HEADS, 2 * BR_WIDTH, N_BRANCH * D_MODEL)
_O = tuple(int(v) for v in np.cumsum((0,) + _W))
_P = {}
_cur = 0
for _name, _w in (("aq", 512), ("ak", 128), ("av", 128), ("aiq", 512), ("aik", 128), ("aiw", 128),
                  ("bu", 512), ("cq", 512), ("ckv", 768), ("cg", 128), ("dglu", 1024)):
    _P[_name] = (_cur, _w)
    _cur += _w
P_TOTAL = _cur

NEG = -1e30
ATT_SCALE = HEAD_DIM ** -0.5
SIGN = -2 ** 31


def _cparams(n_axes):
    return pltpu.CompilerParams(dimension_semantics=("arbitrary",) * n_axes,
                                vmem_limit_bytes=VMEM_LIMIT_MB * 1024 * 1024)


def _full(a, n_grid):
    nd = a.ndim
    return pl.BlockSpec(a.shape, lambda *_: (0,) * nd)


def _dot(a, b):
    return jnp.dot(a, b, preferred_element_type=F32)


def _dot_nt(a, b):
    return lax.dot_general(a, b, (((1,), (1,)), ((), ())), preferred_element_type=F32)


def _dot_split3(p, m):
    hi = p.astype(BF16)
    r = p - hi.astype(F32)
    mid = r.astype(BF16)
    lo = (r - mid.astype(F32)).astype(BF16)
    return _dot(hi, m) + _dot(mid, m) + _dot(lo, m)


def _rms(x, g):
    return x * lax.rsqrt(jnp.mean(x * x, axis=-1, keepdims=True) + NORM_EPS) * g


def _sigmoid(x):
    return 1.0 / (1.0 + jnp.exp(-x))


def _gelu(x):
    return x * (0.5 * (1.0 + jnp.tanh(math.sqrt(2.0 / math.pi) * (x + 0.044715 * (x * x * x)))))


def _sortable(x):
    b = pltpu.bitcast(x + 0.0, I32)
    return jnp.where(b < 0, b ^ jnp.int32(0x7FFFFFFF), b)


def _kth_key(key_ref, k, red_axes):
    shp = tuple(1 if a in red_axes else s for a, s in enumerate(key_ref.shape))

    def count(mask):
        c = jnp.where(mask, 1.0, 0.0)
        for a in sorted(red_axes):
            c = jnp.sum(c, axis=a, keepdims=True)
        return c

    def body(it, tu):
        cand_u = tu | jnp.left_shift(jnp.int32(1), 31 - it)
        cand_s = cand_u ^ jnp.int32(SIGN)
        return jnp.where(count(key_ref[...] >= cand_s) >= k, cand_u, tu)

    tu = lax.fori_loop(0, 32, body, jnp.zeros(shp, I32), unroll=4)
    return tu ^ jnp.int32(SIGN), count


def _select_rows(key_ref, k, tri):
    ts, count = _kth_key(key_ref, k, (1,))
    keys = key_ref[...]
    gt = keys > ts
    need = k - count(gt)
    eqf = jnp.where(keys == ts, 1.0, 0.0)
    base = jnp.zeros_like(need)
    pieces = []
    for c in range(keys.shape[1] // LANES):
        ch = eqf[:, c * LANES:(c + 1) * LANES]
        pref = _dot(ch.astype(BF16), tri) + base
        pieces.append(jnp.where(pref < need, ch, 0.0))
        base = base + jnp.sum(ch, axis=-1, keepdims=True)
    sel_eq = pieces[0] if len(pieces) == 1 else jnp.concatenate(pieces, axis=1)
    return jnp.where(gt, 1.0, sel_eq)


def _select_packed(key_ref, k, tri, tril):
    shp = key_ref.shape
    ts, count = _kth_key(key_ref, k, (0, 1, 2))
    keys = key_ref[...]
    gt = keys > ts
    need = k - count(gt)
    eqf = jnp.where(keys == ts, 1.0, 0.0)
    eq2 = eqf.reshape(LANES, LANES)
    within = _dot(eq2.astype(BF16), tri)
    tot = jnp.broadcast_to(jnp.sum(eq2, axis=1, keepdims=True), (LANES, LANES))
    base = _dot(tril, tot.astype(BF16))
    sel_eq = jnp.where((within + base).reshape(shp) < need, eqf, 0.0)
    return jnp.where(gt, 1.0, sel_eq)


def _select_rank(sc, n, k):
    lane = lax.broadcasted_iota(I32, sc.shape, 1)
    rank = jnp.zeros(sc.shape, F32)
    for i in range(n):
        col = sc[:, i:i + 1]
        rank = rank + jnp.where(lane > i, jnp.where(col >= sc, 1.0, 0.0), jnp.where(col > sc, 1.0, 0.0))
    return jnp.where(rank < k, 1.0, 0.0)


LOG2E = 1.4426950408889634


def _masked_attn(qh, k, v, maskf):
    s = jnp.where(maskf > 0.5, _dot_nt(qh, k) * (ATT_SCALE * LOG2E), NEG)
    e = jnp.exp2(s - jnp.max(s, axis=-1, keepdims=True)) * maskf
    den = jnp.maximum(jnp.sum(e, axis=-1, keepdims=True), 1e-30)
    return _dot(e.astype(BF16), v) / den


def _proj_kernel(x_ref, g_ref, w_ref, c128_ref, s128_ref, c64_ref, s64_ref, gqa_ref, gka_ref, gqc_ref, gkc_ref,
                 xn_ref, aq_ref, arow_ref, aiq_ref, aiw_ref, bu_ref, cq_ref, crow_ref, wrow_ref, cg_ref, da_ref,
                 arow_t_ref=None):
    xn = _rms(x_ref[...], g_ref[...]).astype(BF16)
    xn_ref[...] = xn
    cos, sin = c128_ref[...], s128_ref[...]
    cos64, sin64 = c64_ref[...], s64_ref[...]
    lane = lax.broadcasted_iota(I32, cos.shape, 1)
    lo32 = (lane & 63) < 32
    lo64 = lane < 64

    def seg(name):
        a, w = _P[name]
        return _dot_nt(xn, w_ref[a:a + w, :])

    def rope128(v):
        return v * cos + pltpu.roll(v, 64, 1) * sin

    def rope64(v):
        rot = jnp.where(lo32, pltpu.roll(v, 96, 1), pltpu.roll(v, 32, 1))
        return v * cos64 + rot * sin64

    z = seg("aq")
    for h in range(A_HEADS):
        sl = slice(h * HEAD_DIM, (h + 1) * HEAD_DIM)
        aq_ref[:, sl] = rope128(_rms(z[:, sl], gqa_ref[...])).astype(BF16)
    ak = rope128(_rms(seg("ak"), gka_ref[...]))
    av = seg("av")
    aik = rope64(seg("aik"))
    arow_ref[:, 0:128] = ak
    arow_ref[:, 128:256] = av
    arow_ref[:, 256:320] = aik[:, 0:A_IDX_DIM]
    if arow_t_ref is not None:
        arow_t_ref[0:128, :] = ak.T
        arow_t_ref[128:256, :] = av.T
        arow_t_ref[256:320, :] = aik.T[0:A_IDX_DIM, :]
    z = seg("aiq")
    for j in range(A_IDX_HEADS // 2):
        r = rope64(z[:, j * LANES:(j + 1) * LANES])
        aiq_ref[:, (2 * j) * LANES:(2 * j + 1) * LANES] = jnp.where(lo64, r, 0.0).astype(BF16)
        aiq_ref[:, (2 * j + 1) * LANES:(2 * j + 2) * LANES] = jnp.where(lo64, 0.0, r).astype(BF16)
    aiw_ref[...] = seg("aiw") * (A_IDX_HEADS ** -0.5) * (A_IDX_DIM ** -0.5)
    bu_ref[...] = seg("bu")
    z = seg("cq")
    for h in range(C_HEADS):
        sl = slice(h * HEAD_DIM, (h + 1) * HEAD_DIM)
        cq_ref[:, sl] = rope128(_rms(z[:, sl], gqc_ref[...])).astype(BF16)
    z = seg("ckv")
    for br in range(3):
        kk = rope128(_rms(z[:, (2 * br) * LANES:(2 * br + 1) * LANES], gkc_ref[br:br + 1, :]))
        vv = z[:, (2 * br + 1) * LANES:(2 * br + 2) * LANES]
        if br < 2:
            crow_ref[:, (2 * br) * LANES:(2 * br + 1) * LANES] = kk
            crow_ref[:, (2 * br + 1) * LANES:(2 * br + 2) * LANES] = vv
        else:
            wrow_ref[:, 0:LANES] = kk
            wrow_ref[:, LANES:2 * LANES] = vv
    cg_ref[...] = _sigmoid(seg("cg"))
    z = seg("dglu")
    da_ref[...] = z[:, 0:BR_WIDTH] * _sigmoid(z[:, BR_WIDTH:2 * BR_WIDTH])


def _project(x2d, tabs, lw, *, tm, n_pos_blocks, bu_shape, bu_spec, a_rows_t=None):
    n = x2d.shape[0]
    row = lambda w: pl.BlockSpec((tm, w), lambda i: (i, 0))
    tab = pl.BlockSpec((tm, LANES), lambda i: (i % n_pos_blocks, 0))
    ins = [x2d, lw["g_mix"], lw["w1"], tabs[0], tabs[1], tabs[2], tabs[3], lw["a_gq"], lw["a_gk"], lw["c_gq"], lw["c_gk"]]
    w_spec = pl.BlockSpec(ins[2].shape, lambda i: (0, 0), pipeline_mode=pl.Buffered(1))
    in_specs = [row(D_MODEL), _full(ins[1], 1), w_spec, tab, tab, tab, tab] + [_full(a, 1) for a in ins[7:]]
    outs = [((n, D_MODEL), BF16, row(D_MODEL)),
            ((n, 512), BF16, row(512)),
            ((n, A_CACHE_DIM), F32, row(A_CACHE_DIM)),
            ((n, 1024), BF16, row(1024)),
            ((n, LANES), F32, row(LANES)),
            (bu_shape, F32, bu_spec),
            ((n, 512), BF16, row(512)),
            ((n, C_CACHE_DIM), F32, row(C_CACHE_DIM)),
            ((n, C_WIN_DIM), F32, row(C_WIN_DIM)),
            ((n, LANES), F32, row(LANES)),
            ((n, BR_WIDTH), F32, row(BR_WIDTH))]
    if a_rows_t is not None:
        outs.append((a_rows_t[0], F32, a_rows_t[1]))
    return pl.pallas_call(
        _proj_kernel,
        grid=(n // tm,),
        in_specs=in_specs,
        out_specs=[o[2] for o in outs],
        out_shape=[jax.ShapeDtypeStruct(o[0], o[1]) for o in outs],
        compiler_params=_cparams(1),
        name="project",
    )(*ins)


def _dsa_kernel(aq_ref, aiq_ref, aiw_ref, arow_ref, tri_ref, *rest, qb, L, n_sel, q_first, q_count, aliased):
    o_ref, k_sc, v_sc, ik_sc, key_sc = rest[1:] if aliased else rest
    qi = pl.program_id(1) + q_first

    @pl.when(pl.program_id(1) == 0)
    def _():
        k_sc[...] = arow_ref[:, 0:128].astype(BF16)
        v_sc[...] = arow_ref[:, 128:256].astype(BF16)
        ik = arow_ref[:, 256:320]
        ik_sc[...] = jnp.concatenate([ik, ik], axis=1).astype(BF16)

    def body(le):
        qpos = qi * qb + lax.broadcasted_iota(I32, (qb, 1), 0)
        valid = lax.broadcasted_iota(I32, (qb, le), 1) <= qpos
        if le <= n_sel:
            mask = jnp.where(valid, 1.0, 0.0)
        else:
            w = aiw_ref[...]
            score = None
            for h in range(A_IDX_HEADS):
                lg = _dot_nt(aiq_ref[:, h * LANES:(h + 1) * LANES], ik_sc[0:le, :])
                t = jnp.maximum(lg, 0.0) * w[:, h:h + 1]
                score = t if score is None else score + t
            keys = key_sc.at[:, 0:le]
            keys[...] = _sortable(jnp.where(valid, score, -jnp.inf))
            mask = jnp.where(valid, _select_rows(keys, n_sel, tri_ref[...]), 0.0)
        q = aq_ref[...]
        for h in range(A_HEADS):
            sl = slice(h * HEAD_DIM, (h + 1) * HEAD_DIM)
            o_ref[:, sl] = _masked_attn(q[:, sl], k_sc[0:le, :], v_sc[0:le, :], mask).astype(BF16)

    _causal_branches(qi, qb, L, body, q_range=(q_first, q_first + q_count))


def _causal_branches(qi, qb, L, body, q_range=None):
    step = max(qb, L // 8)
    if L % step:
        step = qb
    per = step // qb
    for j in range(L // step):
        if q_range is not None and ((j + 1) * per <= q_range[0] or j * per >= q_range[1]):
            continue

        @pl.when((qi >= j * per) & (qi < (j + 1) * per))
        def _(j=j):
            body((j + 1) * step)


def _dsa(aq, aiq, aiw, arow3, tri, *, qb):
    bsz, L, _ = arow3.shape
    nq = L // qb
    bounds = sorted({0, nq // 2, (3 * nq) // 4, nq})
    n_sel = min(A_TOPK, L // 4)
    out = None
    for q0, q1 in zip(bounds[:-1], bounds[1:]):
        npq = q1 - q0
        row = lambda w, q0=q0: pl.BlockSpec((qb, w), lambda b, i: (b * nq + q0 + i, 0))
        ins = [aq, aiq, aiw, arow3, tri]
        in_specs = [row(512), row(1024), row(LANES),
                    pl.BlockSpec((None, L, A_CACHE_DIM), lambda b, i: (b, 0, 0)), _full(tri, 2)]
        aliases = {}
        if out is not None:
            ins.append(out)
            in_specs.append(pl.BlockSpec(memory_space=pl.ANY))
            aliases = {len(ins) - 1: 0}
        out = pl.pallas_call(
            functools.partial(_dsa_kernel, qb=qb, L=L, n_sel=n_sel, q_first=q0, q_count=npq,
                              aliased=out is not None),
            grid=(bsz, npq),
            in_specs=in_specs,
            out_specs=row(512),
            out_shape=jax.ShapeDtypeStruct((bsz * L, 512), BF16),
            scratch_shapes=[pltpu.VMEM((L, 128), BF16), pltpu.VMEM((L, 128), BF16), pltpu.VMEM((L, 128), BF16),
                            pltpu.VMEM((qb, L), I32)],
            input_output_aliases=aliases,
            compiler_params=_cparams(2),
            name="dsa",
        )(*ins)
    return out


def _summaries(x, a, w, n_cmp):
    ns = x.shape[0] // C_CMP_STRIDE
    x3 = x.reshape(ns, C_CMP_STRIDE, HEAD_DIM)
    lo = jnp.sum(x3 * a[0:C_CMP_STRIDE][None], axis=1)
    hi = jnp.sum(x3 * a[C_CMP_STRIDE:C_CMP_LEN][None], axis=1)
    return lo, hi


def _finish_summaries(lo, hi, w, n_cmp):
    ns = lo.shape[0]
    comb = lo + pltpu.roll(hi, ns - 1, 0)
    comb = jnp.where(lax.broadcasted_iota(I32, comb.shape, 0) < n_cmp, comb, 0.0)
    return _dot(comb.astype(BF16), w).astype(BF16)


def _cmp_kernel(x_ref, a_ref, w_ref, kc_ref, vc_ref, *, n_cmp):
    for t, out in ((0, kc_ref), (1, vc_ref)):
        lo, hi = _summaries(x_ref[:, t * LANES:(t + 1) * LANES], a_ref[t], w_ref[t], n_cmp)
        out[...] = _finish_summaries(lo, hi, w_ref[t], n_cmp)


def _compress(crow3, cmp_a, cmp_w):
    bsz, L, _ = crow3.shape
    ns = L // C_CMP_STRIDE
    n_cmp = (L - C_CMP_LEN) // C_CMP_STRIDE + 1
    out = pl.BlockSpec((None, ns, HEAD_DIM), lambda b: (b, 0, 0))
    return pl.pallas_call(
        functools.partial(_cmp_kernel, n_cmp=n_cmp),
        grid=(bsz,),
        in_specs=[pl.BlockSpec((None, L, C_CACHE_DIM), lambda b: (b, 0, 0)), _full(cmp_a, 1), _full(cmp_w, 1)],
        out_specs=[out, out],
        out_shape=[jax.ShapeDtypeStruct((bsz, ns, HEAD_DIM), BF16)] * 2,
        compiler_params=_cparams(1),
        name="compress",
    )(crow3, cmp_a, cmp_w)


def _block_scores(imp, qpos):
    j = lax.broadcasted_iota(I32, imp.shape, 1)
    cur = lax.shift_right_logical(qpos, 6)
    forced = (j == 0) | (j == cur) | (j == cur - 1)
    return jnp.where(j <= cur, jnp.where(forced, C_FORCE, imp), -jnp.inf)


def _cmp_softmax(qh, kc, cvalid):
    s = _dot_nt(qh, kc) * ATT_SCALE
    s = jnp.where(cvalid, s, NEG)
    m = jnp.max(s, axis=-1, keepdims=True)
    e = jnp.where(cvalid, jnp.exp(s - m), 0.0)
    return e / jnp.maximum(jnp.sum(e, axis=-1, keepdims=True), 1e-30)


def _nsa_kernel(cq_ref, cg_ref, crow_ref, wrow_ref, kc_ref, vc_ref, ov_ref, ex_ref, o_ref,
                ks_sc, vs_sc, kw_sc, vw_sc, os_sc, *, qb, L, n_cmp, n_blk, n_top, wsl):
    qi = pl.program_id(1)

    @pl.when(qi == 0)
    def _():
        ks_sc[...] = crow_ref[:, 256:384].astype(BF16)
        vs_sc[...] = crow_ref[:, 384:512].astype(BF16)
        kw_sc[...] = wrow_ref[:, 0:128].astype(BF16)
        vw_sc[...] = wrow_ref[:, 128:256].astype(BF16)

    q0 = qi * qb
    qpos = q0 + lax.broadcasted_iota(I32, (qb, 1), 0)
    q = cq_ref[...]
    g = cg_ref[...]
    kc, vc = kc_ref[...], vc_ref[...]
    ncp = kc.shape[0]
    n_io = lax.broadcasted_iota(I32, (qb, ncp), 1)
    cvalid = (n_io < n_cmp) & (n_io * C_CMP_STRIDE + (C_CMP_LEN - 1) <= qpos)
    o_c, psum = [], None
    for h in range(C_HEADS):
        p = _cmp_softmax(q[:, h * HEAD_DIM:(h + 1) * HEAD_DIM], kc, cvalid)
        o_c.append(_dot(p.astype(BF16), vc))
        psum = p if psum is None else psum + p
    imp = _dot_split3(psum, ov_ref[...])
    sel = _select_rank(_block_scores(imp, qpos), n_blk, n_top).astype(BF16)

    def selected(le):
        selk = _dot(sel, ex_ref[:, 0:le])
        smask = jnp.where(lax.broadcasted_iota(I32, (qb, le), 1) <= qpos, selk, 0.0)
        for h in range(C_HEADS):
            sl = slice(h * HEAD_DIM, (h + 1) * HEAD_DIM)
            os_sc[:, sl] = _masked_attn(q[:, sl], ks_sc[0:le, :], vs_sc[0:le, :], smask)

    _causal_branches(qi, qb, L, selected)
    start = pl.multiple_of(jnp.minimum(jnp.maximum(q0 - C_WINDOW, 0), L - wsl), qb)
    dist = qpos - (start + lax.broadcasted_iota(I32, (qb, wsl), 1))
    wmask = jnp.where((dist >= 0) & (dist <= C_WINDOW), 1.0, 0.0)
    kw = kw_sc[pl.ds(start, wsl), :]
    vw = vw_sc[pl.ds(start, wsl), :]
    for h in range(C_HEADS):
        sl = slice(h * HEAD_DIM, (h + 1) * HEAD_DIM)
        o_w = _masked_attn(q[:, sl], kw, vw, wmask)
        out = g[:, 3 * h:3 * h + 1] * o_c[h] + g[:, 3 * h + 1:3 * h + 2] * os_sc[:, sl] + g[:, 3 * h + 2:3 * h + 3] * o_w
        o_ref[:, sl] = out.astype(BF16)


def _nsa(cq, cg, crow3, wrow3, kcmp, vcmp, ov, ex, *, qb):
    bsz, L, _ = crow3.shape
    nq = L // qb
    ns = kcmp.shape[1]
    n_cmp = (L - C_CMP_LEN) // C_CMP_STRIDE + 1
    n_blk = -(-L // C_SLC_BLOCK)
    wsl = min(L, C_WINDOW + qb)
    row = lambda w: pl.BlockSpec((qb, w), lambda b, i: (b * nq + i, 0))
    per_b = lambda r, w: pl.BlockSpec((None, r, w), lambda b, i: (b, 0, 0))
    return pl.pallas_call(
        functools.partial(_nsa_kernel, qb=qb, L=L, n_cmp=n_cmp, n_blk=n_blk, n_top=min(C_TOPN, n_blk), wsl=wsl),
        grid=(bsz, nq),
        in_specs=[row(512), row(LANES), per_b(L, C_CACHE_DIM), per_b(L, C_WIN_DIM),
                  per_b(ns, HEAD_DIM), per_b(ns, HEAD_DIM), _full(ov, 2), _full(ex, 2)],
        out_specs=row(512),
        out_shape=jax.ShapeDtypeStruct((bsz * L, 512), BF16),
        scratch_shapes=[pltpu.VMEM((L, 128), BF16)] * 4 + [pltpu.VMEM((qb, 512), F32)],
        compiler_params=_cparams(2),
        name="nsa",
    )(cq, cg, crow3, wrow3, kcmp, vcmp, ov, ex)


S5_BLOCKS = 4


def _s5_kernel(u_ref, bdr_ref, bdi_ref, ar_ref, ai_ref, cdr_ref, cdi_ref, d_ref, gw_ref, gb_ref,
               o_ref, hr_ref, hi_ref, xr_sc, xi_sc, h_sc, *, tc, nb):
    i = pl.program_id(0)

    @pl.when(i == 0)
    def _():
        h_sc[...] = jnp.zeros_like(h_sc)

    u = u_ref[...]
    ub = u.astype(BF16)
    nsb = bdr_ref.shape[0]
    wu, wx = BR_WIDTH // nsb, B_LANES // nsb
    for sb in range(nsb):
        us = ub[:, sb * wu:(sb + 1) * wu]
        xr_sc[:, sb * wx:(sb + 1) * wx] = _dot(us, bdr_ref[sb])
        xi_sc[:, sb * wx:(sb + 1) * wx] = _dot(us, bdi_ref[sb])
    ar = jnp.broadcast_to(ar_ref[...], (nb, B_LANES))
    ai = jnp.broadcast_to(ai_ref[...], (nb, B_LANES))

    def step(t, carry):
        hr, hi = carry
        r0 = pl.multiple_of(t * nb, nb)
        nhr = ar * hr - ai * hi + xr_sc[pl.ds(r0, nb), :]
        nhi = ar * hi + ai * hr + xi_sc[pl.ds(r0, nb), :]
        xr_sc[pl.ds(r0, nb), :] = nhr
        xi_sc[pl.ds(r0, nb), :] = nhi
        return nhr, nhi

    hr, hi = lax.fori_loop(0, tc, step, (h_sc[0], h_sc[1]))
    h_sc[0] = hr
    h_sc[1] = hi
    hr_ref[...] = hr
    hi_ref[...] = hi
    ch = [_dot(xr_sc[:, sb * wx:(sb + 1) * wx].astype(BF16), cdr_ref[sb])
          - _dot(xi_sc[:, sb * wx:(sb + 1) * wx].astype(BF16), cdi_ref[sb]) for sb in range(nsb)]
    y = jnp.concatenate(ch, axis=1) + d_ref[...] * u
    gl = _gelu(y)
    o_ref[...] = (gl * _sigmoid(_dot(gl.astype(BF16), gw_ref[...]) + gb_ref[...])).astype(BF16)


def _s5(u_tm, sp, *, nb, tc):
    rows = u_tm.shape[0]
    r = tc * nb
    consts = [sp["bdr4"], sp["bdi4"], sp["ar"], sp["ai"], sp["cdr4"], sp["cdi4"], sp["d"], sp["glu_w"], sp["glu_b"]]
    st = pl.BlockSpec((nb, B_LANES), lambda i: (0, 0))
    return pl.pallas_call(
        functools.partial(_s5_kernel, tc=tc, nb=nb),
        grid=(rows // r,),
        in_specs=[pl.BlockSpec((r, BR_WIDTH), lambda i: (i, 0))] + [_full(c, 1) for c in consts],
        out_specs=[pl.BlockSpec((r, BR_WIDTH), lambda i: (i, 0)), st, st],
        out_shape=[jax.ShapeDtypeStruct((rows, BR_WIDTH), BF16),
                   jax.ShapeDtypeStruct((nb, B_LANES), F32), jax.ShapeDtypeStruct((nb, B_LANES), F32)],
        scratch_shapes=[pltpu.VMEM((r, B_LANES), F32), pltpu.VMEM((r, B_LANES), F32), pltpu.VMEM((2, nb, B_LANES), F32)],
        compiler_params=_cparams(1),
        name="s5",
    )(u_tm, *consts)


HALO = 32


def _ln_swish(y, g, b):
    yc = y - jnp.mean(y, axis=-1, keepdims=True)
    yn = yc * lax.rsqrt(jnp.mean(yc * yc, axis=-1, keepdims=True) + NORM_EPS) * g + b
    return yn * _sigmoid(yn)


def _conv_kernel(cur_ref, halo_ref, w_ref, b_ref, lg_ref, lb_ref, o_ref, ext_sc, *, tm):
    i = pl.program_id(1)
    ext_sc[0, 0:HALO, :] = jnp.where(i == 0, 0.0, halo_ref[...])
    ext_sc[0, HALO:HALO + tm, :] = cur_ref[...]
    n = HALO + tm - SUBLANES
    for k in range(1, SUBLANES):
        ext_sc[k, 0:n, :] = ext_sc[0, pl.ds(k, n), :]
    acc = jnp.zeros((tm, BR_WIDTH), F32)
    for j in range(D_CONV):
        off = HALO - (D_CONV - 1) + j
        k = off % SUBLANES
        acc = acc + w_ref[j:j + 1, :] * ext_sc[k, off - k:off - k + tm, :]
    o_ref[...] = _ln_swish(acc + b_ref[...], lg_ref[...], lb_ref[...]).astype(BF16)


def _conv(da3, cw, cb, lg, lb, *, tm):
    bsz, T, _ = da3.shape
    nt = T // tm
    hb = tm // HALO
    consts = [cw, cb, lg, lb]
    return pl.pallas_call(
        functools.partial(_conv_kernel, tm=tm),
        grid=(bsz, nt),
        in_specs=[pl.BlockSpec((None, tm, BR_WIDTH), lambda b, i: (b, i, 0)),
                  pl.BlockSpec((None, HALO, BR_WIDTH), lambda b, i: (b, jnp.maximum(i * hb - 1, 0), 0))]
        + [_full(c, 2) for c in consts],
        out_specs=pl.BlockSpec((tm, BR_WIDTH), lambda b, i: (b * nt + i, 0)),
        out_shape=jax.ShapeDtypeStruct((bsz * T, BR_WIDTH), BF16),
        scratch_shapes=[pltpu.VMEM((SUBLANES, HALO + tm, BR_WIDTH), F32)],
        compiler_params=_cparams(2),
        name="conv",
    )(da3, da3, *consts)


MIX_CW = 512


def _mix_kernel(xn_ref, oa_ref, ob_ref, oc_ref, od_ref, wg_ref, wbr_ref, wo_ref, hm_ref, acc_sc):
    i = pl.program_id(1)

    @pl.when(i == 0)
    def _():
        acc_sc[...] = jnp.zeros_like(acc_sc)

    xn = xn_ref[...]
    br = jnp.where(i == 0, oa_ref[...], jnp.where(i == 1, ob_ref[...], jnp.where(i == 2, oc_ref[...], od_ref[...])))
    for c in range(D_MODEL // MIX_CW):
        sl = slice(c * MIX_CW, (c + 1) * MIX_CW)
        contrib = _sigmoid(_dot_nt(xn, wg_ref[sl, :])) * _dot(br, wbr_ref[:, sl])
        acc_sc[:, sl] = acc_sc[:, sl] + contrib

    @pl.when(i == N_BRANCH - 1)
    def _():
        hm_ref[...] = _dot(acc_sc[...].astype(BF16), wo_ref[...])


def _mix(xn, oa, ob, ob_spec, oc, od, wg, wbr, wo, *, tm):
    n = xn.shape[0]
    row = lambda w: pl.BlockSpec((tm, w), lambda r, i: (r, 0))
    return pl.pallas_call(
        _mix_kernel,
        grid=(n // tm, N_BRANCH),
        in_specs=[row(D_MODEL), row(BR_WIDTH), ob_spec, row(BR_WIDTH), row(BR_WIDTH),
                  pl.BlockSpec((D_MODEL, D_MODEL), lambda r, i: (i, 0)),
                  pl.BlockSpec((None, BR_WIDTH, D_MODEL), lambda r, i: (i, 0, 0)),
                  pl.BlockSpec((D_MODEL, D_MODEL), lambda r, i: (0, 0))],
        out_specs=row(D_MODEL),
        out_shape=jax.ShapeDtypeStruct((n, D_MODEL), F32),
        scratch_shapes=[pltpu.VMEM((tm, D_MODEL), F32)],
        compiler_params=_cparams(2),
        name="mix",
    )(xn, oa, ob, oc, od, wg, wbr, wo)


MLP_FC = 1024


def _mlp_kernel(x_ref, hm_ref, g_ref, wu_ref, wd_ref, y_ref, hn_sc, acc_sc):
    j = pl.program_id(1)

    @pl.when(j == 0)
    def _():
        hn_sc[...] = _rms(x_ref[...] + hm_ref[...], g_ref[...]).astype(BF16)
        acc_sc[...] = jnp.zeros_like(acc_sc)

    up = _dot(hn_sc[...], wu_ref[...])
    act = jnp.square(jnp.maximum(up, 0.0)).astype(BF16)
    d = _dot(act, wd_ref[...])
    acc_sc[...] = acc_sc[...] + d

    @pl.when(j == pl.num_programs(1) - 1)
    def _():
        y_ref[...] = (x_ref[...] + hm_ref[...]) + acc_sc[...]


def _mlp(x2d, hm, g, wu, wd, *, tm):
    n = x2d.shape[0]
    row = pl.BlockSpec((tm, D_MODEL), lambda r, j: (r, 0))
    return pl.pallas_call(
        _mlp_kernel,
        grid=(n // tm, D_FF // MLP_FC),
        in_specs=[row, row, _full(g, 2),
                  pl.BlockSpec((D_MODEL, MLP_FC), lambda r, j: (0, j)),
                  pl.BlockSpec((MLP_FC, D_MODEL), lambda r, j: (j, 0))],
        out_specs=row,
        out_shape=jax.ShapeDtypeStruct((n, D_MODEL), F32),
        scratch_shapes=[pltpu.VMEM((tm, D_MODEL), BF16), pltpu.VMEM((tm, D_MODEL), F32)],
        compiler_params=_cparams(2),
        name="mlp",
    )(x2d, hm, g, wu, wd)


PAGES_PER_STEP = 16


def _tile_attention(s_ref, bias_ref, v_ref, n, s_new, bias_new, v_new, v_transposed):
    c2 = ATT_SCALE * LOG2E
    s = s_ref[...] * c2 + bias_ref[...]
    sn = s_new * c2 + bias_new
    m = jnp.maximum(jnp.max(jnp.max(s, axis=0), axis=1, keepdims=True), sn)
    e = jnp.exp2(s - m)
    en = jnp.exp2(sn - m)
    den = jnp.sum(jnp.sum(e, axis=0), axis=1, keepdims=True) + en
    acc = en.astype(BF16).astype(F32) * v_new.astype(BF16).astype(F32)
    mm = _dot_nt if v_transposed else _dot
    for c in range(n):
        acc = acc + mm(e[c].astype(BF16), v_ref[c * LANES:(c + 1) * LANES, :])
    return acc / den


KEY_TILES = 16


def _dsa_s_kernel(pt_ref, q_ref, iq_ref, w_ref, new_ref, *rest, pg, n_pages, n_sel):
    pages = rest[:pg]
    tri_ref, tril_ref, o_ref, s_sc, bias_sc, vt_sc, sc_sc, key_sc = rest[pg:]
    g = pl.program_id(1)
    q = q_ref[...].astype(BF16)
    iq = iq_ref[...].astype(BF16)
    w = w_ref[...]

    @pl.when(g == 0)
    def _():
        sc_sc[...] = jnp.full(sc_sc.shape, -jnp.inf, F32)

    for i in range(pg):
        c = g * pg + i
        page = pages[i]
        s_sc[c] = _dot(q, page[0:128, :].astype(BF16))
        vt_sc[pl.ds(pl.multiple_of(c * LANES, LANES), LANES), :] = page[128:256, :].astype(BF16)
        lg = _dot(iq, page[192:320, :].astype(BF16))
        sc_sc[g * (pg // SUBLANES) + i // SUBLANES, i % SUBLANES:i % SUBLANES + 1, :] = jnp.sum(
            jnp.maximum(lg, 0.0) * w, axis=0, keepdims=True)

    @pl.when(g == pl.num_programs(1) - 1)
    def _():
        new = new_ref[...]
        k_new = new[0:1, 0:128].astype(BF16).astype(F32)
        s_new = jnp.sum(q.astype(F32) * k_new, axis=-1, keepdims=True)
        ik_new = new[0:1, 256:320].astype(BF16).astype(F32)
        lg_new = jnp.sum(iq[:, A_IDX_DIM:2 * A_IDX_DIM].astype(F32) * ik_new, axis=-1, keepdims=True)
        sc_new = jnp.sum(jnp.maximum(lg_new, 0.0) * w[:, 0:1], axis=0, keepdims=True)
        t_new = n_pages // SUBLANES
        first = (lax.broadcasted_iota(I32, (SUBLANES, LANES), 0) == 0) & (lax.broadcasted_iota(I32, (SUBLANES, LANES), 1) == 0)
        sc_sc[t_new] = jnp.where(first, sc_new, -jnp.inf)
        key_sc[...] = _sortable(sc_sc[...])
        sel = _select_packed(key_sc, n_sel, tri_ref[...], tril_ref[...])
        for c in range(n_pages):
            row = sel[c // SUBLANES][c % SUBLANES:c % SUBLANES + 1, :]
            bias_sc[c] = jnp.where(jnp.broadcast_to(row, (SUBLANES, LANES)) > 0.5, 0.0, NEG)
        bias_new = jnp.where(sel[t_new][0:1, 0:1] > 0.5, 0.0, NEG)
        o_ref[...] = _tile_attention(s_sc, bias_sc, vt_sc, n_pages, s_new, bias_new, new[0:1, 128:256], True)


def _dsa_sample(page_table, cache_t, layer, q8, iq8, w8, new8, tri, tril):
    bsz, n_pages = page_table.shape
    pg = PAGES_PER_STEP
    assert pg % SUBLANES == 0 and n_pages % pg == 0 and n_pages * PAGE + 1 <= KEY_TILES * SUBLANES * LANES
    n_sel = min(A_TOPK, (n_pages * PAGE + 1) // 4)
    per_b = lambda r, w: pl.BlockSpec((None, r, w), lambda b, g, pt: (b, 0, 0))
    cst = lambda a: pl.BlockSpec(a.shape, lambda b, g, pt: (0,) * a.ndim)
    page_spec = lambda i: pl.BlockSpec((None, None, A_CACHE_DIM, PAGE),
                                       lambda b, g, pt: (layer, pt[b, g * pg + i], 0, 0))
    grid_spec = pltpu.PrefetchScalarGridSpec(
        num_scalar_prefetch=1,
        grid=(bsz, n_pages // pg),
        in_specs=[per_b(SUBLANES, LANES), per_b(SUBLANES, LANES), per_b(SUBLANES, LANES), per_b(SUBLANES, A_CACHE_DIM)]
        + [page_spec(i) for i in range(pg)] + [cst(tri), cst(tril)],
        out_specs=per_b(SUBLANES, HEAD_DIM),
        scratch_shapes=[pltpu.VMEM((n_pages, SUBLANES, LANES), F32), pltpu.VMEM((n_pages, SUBLANES, LANES), F32),
                        pltpu.VMEM((n_pages * LANES, PAGE), BF16),
                        pltpu.VMEM((KEY_TILES, SUBLANES, LANES), F32), pltpu.VMEM((KEY_TILES, SUBLANES, LANES), I32)],
    )
    return pl.pallas_call(
        functools.partial(_dsa_s_kernel, pg=pg, n_pages=n_pages, n_sel=n_sel),
        grid_spec=grid_spec,
        out_shape=jax.ShapeDtypeStruct((bsz, SUBLANES, HEAD_DIM), F32),
        compiler_params=_cparams(2),
        name="dsa_sample",
    )(page_table, q8, iq8, w8, new8, *([cache_t] * pg), tri, tril)


def _nsa_s_kernel(pt_ref, q_ref, g3_ref, new_ref, win_ref, wnew_ref, a_ref, w_ref, ov_ref, *rest,
                  pg, n_pages, n_cmp, n_blk, n_top, past):
    pages = rest[:pg]
    o_ref, ss_sc, bias_sc, vs_sc, lok_sc, hik_sc, lov_sc, hiv_sc = rest[pg:]
    g = pl.program_id(1)
    q = q_ref[...].astype(BF16)
    sub = PAGE // C_CMP_STRIDE

    for i in range(pg):
        c = g * pg + i
        page = pages[i]
        r0 = pl.multiple_of(c * sub, sub)
        lo, hi = _summaries(page[:, 0:128], a_ref[0], None, n_cmp)
        lok_sc[pl.ds(r0, sub), :] = lo
        hik_sc[pl.ds(r0, sub), :] = hi
        lo, hi = _summaries(page[:, 128:256], a_ref[1], None, n_cmp)
        lov_sc[pl.ds(r0, sub), :] = lo
        hiv_sc[pl.ds(r0, sub), :] = hi
        ss_sc[c] = _dot_nt(q, page[:, 256:384].astype(BF16))
        vs_sc[pl.ds(pl.multiple_of(c * LANES, LANES), LANES), :] = page[:, 384:512].astype(BF16)

    @pl.when(g == pl.num_programs(1) - 1)
    def _():
        qf = q.astype(F32)
        new = new_ref[...]
        lane = lax.broadcasted_iota(I32, (SUBLANES, LANES), 1)
        qpos = jnp.full((SUBLANES, 1), past, I32)
        kc = _finish_summaries(lok_sc[...], hik_sc[...], w_ref[0], n_cmp)
        vc = _finish_summaries(lov_sc[...], hiv_sc[...], w_ref[1], n_cmp)
        ncp = kc.shape[0]
        n_io = lax.broadcasted_iota(I32, (SUBLANES, ncp), 1)
        cvalid = (n_io < n_cmp) & (n_io * C_CMP_STRIDE + (C_CMP_LEN - 1) <= qpos)
        p = _cmp_softmax(q, kc, cvalid)
        o_c = _dot(p.astype(BF16), vc)
        head = lax.broadcasted_iota(I32, p.shape, 0) < C_HEADS
        psum = jnp.broadcast_to(jnp.sum(jnp.where(head, p, 0.0), axis=0, keepdims=True), p.shape)
        imp = _dot_split3(psum, ov_ref[...])
        sel = _select_rank(_block_scores(imp, qpos), n_blk, n_top)
        for c in range(n_pages):
            pick = jnp.where(lane < C_SLC_BLOCK, sel[:, 2 * c:2 * c + 1], sel[:, 2 * c + 1:2 * c + 2])
            bias_sc[c] = jnp.where(pick > 0.5, 0.0, NEG)
        k_new = new[0:1, 256:384].astype(BF16).astype(F32)
        s_new = jnp.sum(qf * k_new, axis=-1, keepdims=True)
        bias_new = jnp.where(sel[:, 2 * n_pages:2 * n_pages + 1] > 0.5, 0.0, NEG)
        o_s = _tile_attention(ss_sc, bias_sc, vs_sc, n_pages, s_new, bias_new, new[0:1, 384:512], False)
        wb = win_ref.shape[0]
        kw = win_ref[:, 0:128].astype(BF16)
        vw = win_ref[:, 128:256].astype(BF16)
        wnew = wnew_ref[...]
        s_w = _dot_nt(q, kw) * ATT_SCALE
        dist = wb - lax.broadcasted_iota(I32, (SUBLANES, wb), 1)
        wvalid = (dist <= C_WINDOW) & (past - dist >= 0)
        s_w = jnp.where(wvalid, s_w, NEG)
        s_n = jnp.sum(qf * wnew[0:1, 0:128].astype(BF16).astype(F32), axis=-1, keepdims=True) * ATT_SCALE
        m = jnp.maximum(jnp.max(s_w, axis=-1, keepdims=True), s_n)
        e_w = jnp.where(wvalid, jnp.exp(s_w - m), 0.0)
        e_n = jnp.exp(s_n - m)
        den = jnp.sum(e_w, axis=-1, keepdims=True) + e_n
        v_n = wnew[0:1, 128:256].astype(BF16).astype(F32)
        o_w = (_dot(e_w.astype(BF16), vw) + e_n.astype(BF16).astype(F32) * v_n) / den
        o_ref[...] = g3_ref[0] * o_c + g3_ref[1] * o_s + g3_ref[2] * o_w


def _nsa_sample(page_table, cache, layer, q8, g3, new8, win, wnew8, cmp_a, cmp_w, ov):
    bsz, n_pages = page_table.shape
    pg = PAGES_PER_STEP
    past = n_pages * PAGE
    n_cmp = (past + 1 - C_CMP_LEN) // C_CMP_STRIDE + 1
    n_blk = -(-(past + 1) // C_SLC_BLOCK)
    ns = past // C_CMP_STRIDE
    wb = win.shape[2]
    cst = lambda a: pl.BlockSpec(a.shape, lambda b, g, pt: (0,) * a.ndim)
    per_b = lambda r, w: pl.BlockSpec((None, r, w), lambda b, g, pt: (b, 0, 0))
    page_spec = lambda i: pl.BlockSpec((None, None, PAGE, C_CACHE_DIM),
                                       lambda b, g, pt: (layer, pt[b, g * pg + i], 0, 0))
    grid_spec = pltpu.PrefetchScalarGridSpec(
        num_scalar_prefetch=1,
        grid=(bsz, n_pages // pg),
        in_specs=[per_b(SUBLANES, LANES),
                  pl.BlockSpec((None, 3, SUBLANES, LANES), lambda b, g, pt: (b, 0, 0, 0)),
                  per_b(SUBLANES, C_CACHE_DIM),
                  pl.BlockSpec((None, None, wb, C_WIN_DIM), lambda b, g, pt: (layer, b, 0, 0)),
                  per_b(SUBLANES, C_WIN_DIM), cst(cmp_a), cst(cmp_w), cst(ov)]
        + [page_spec(i) for i in range(pg)],
        out_specs=per_b(SUBLANES, HEAD_DIM),
        scratch_shapes=[pltpu.VMEM((n_pages, SUBLANES, LANES), F32), pltpu.VMEM((n_pages, SUBLANES, LANES), F32),
                        pltpu.VMEM((n_pages * LANES, HEAD_DIM), BF16)]
        + [pltpu.VMEM((ns, HEAD_DIM), F32)] * 4,
    )
    return pl.pallas_call(
        functools.partial(_nsa_s_kernel, pg=pg, n_pages=n_pages, n_cmp=n_cmp, n_blk=n_blk, n_top=min(C_TOPN, n_blk),
                          past=past),
        grid_spec=grid_spec,
        out_shape=jax.ShapeDtypeStruct((bsz, SUBLANES, HEAD_DIM), F32),
        compiler_params=_cparams(2),
        name="nsa_sample",
    )(page_table, q8, g3, new8, win, wnew8, cmp_a, cmp_w, ov, *([cache] * pg))


def _step_kernel(u_ref, h0r_ref, h0i_ref, bdr_h_ref, bdr_l_ref, bdi_h_ref, bdi_l_ref, ar_ref, ai_ref,
                 cdr_ref, cdi_ref, d_ref, gw_ref, gb_ref, da_ref, cst_ref, cw_ref, cb_ref, lg_ref, lb_ref,
                 ob_ref, hr_ref, hi_ref, od_ref):
    u = u_ref[...]
    uh = u.astype(BF16)
    ul = (u - uh.astype(F32)).astype(BF16)

    def bmat(h_ref, l_ref):
        return _dot(uh, h_ref[...]) + (_dot(uh, l_ref[...]) + _dot(ul, h_ref[...]))

    ar, ai = ar_ref[...], ai_ref[...]
    h0r, h0i = h0r_ref[...], h0i_ref[...]
    hr = bmat(bdr_h_ref, bdr_l_ref) + (ar * h0r - ai * h0i)
    hi = bmat(bdi_h_ref, bdi_l_ref) + (ar * h0i + ai * h0r)
    hr_ref[...] = hr
    hi_ref[...] = hi
    y = _dot(hr.astype(BF16), cdr_ref[...]) - _dot(hi.astype(BF16), cdi_ref[...]) + d_ref[...] * u
    gl = _gelu(y)
    ob_ref[...] = (gl * _sigmoid(_dot(gl.astype(BF16), gw_ref[...]) + gb_ref[...])).astype(BF16)
    cw = cw_ref[...]
    y = jnp.sum(cst_ref[...] * cw[0:D_CONV - 1][None], axis=1) + cw[D_CONV - 1:D_CONV] * da_ref[...] + cb_ref[...]
    od_ref[...] = _ln_swish(y, lg_ref[...], lb_ref[...]).astype(BF16)


def _sample_step(u, h0r, h0i, sp, da, conv_state, cw, cb, lg, lb):
    bsz = u.shape[0]
    ins = [u, h0r, h0i, sp["bdr"], sp["bdr_lo"], sp["bdi"], sp["bdi_lo"], sp["ar"], sp["ai"], sp["cdr"], sp["cdi"],
           sp["d"], sp["glu_w"], sp["glu_b"], da, conv_state, cw, cb, lg, lb]
    outs = [((bsz, BR_WIDTH), BF16), ((bsz, B_LANES), F32), ((bsz, B_LANES), F32), ((bsz, BR_WIDTH), BF16)]
    return pl.pallas_call(
        _step_kernel,
        grid=(1,),
        in_specs=[_full(a, 1) for a in ins],
        out_specs=[pl.BlockSpec(o[0], lambda i: (0, 0)) for o in outs],
        out_shape=[jax.ShapeDtypeStruct(o[0], o[1]) for o in outs],
        compiler_params=_cparams(1),
        name="sample_step",
    )(*ins)


def _rope_tables(pos):
    pos = pos.astype(F32)[:, None]

    def tab(dim):
        half = dim // 2
        inv = ROPE_THETA ** (-jnp.arange(half, dtype=F32) / half)
        ang = pos * inv
        cos, sin = jnp.cos(ang), jnp.sin(ang)
        reps = LANES // dim
        return jnp.tile(jnp.concatenate([cos, cos], axis=1), (1, reps)), jnp.tile(jnp.concatenate([-sin, sin], axis=1), (1, reps))

    c128, s128 = tab(HEAD_DIM)
    c64, s64 = tab(A_IDX_DIM)
    return c128, s128, c64, s64


def _pack_w1(wt):
    def padded(a, b, rows):
        return jnp.pad(wt[a:b], ((0, rows - (b - a)), (0, 0)))
    parts = [wt[_O[0]:_O[4]], padded(_O[4], _O[5], 128), padded(_O[5], _O[6], 128), wt[_O[6]:_O[9]],
             padded(_O[9], _O[10], 128), wt[_O[10]:_O[11]]]
    return jnp.concatenate(parts, axis=0).astype(BF16)


def _s5_params(lam_re, lam_im, log_dt, b_re, b_im, c_re, c_im, d, glu_w, glu_b):
    lr, li = lam_re.astype(F32), lam_im.astype(F32)
    dt = jnp.exp(log_dt.astype(F32))[:, None]
    mag = jnp.exp(lr * dt)
    ar, ai = mag * jnp.cos(li * dt), mag * jnp.sin(li * dt)
    den = lr * lr + li * li
    fr = ((ar - 1.0) * lr + ai * li) / den
    fi = (ai * lr - (ar - 1.0) * li) / den
    br, bi = b_re.astype(F32), b_im.astype(F32)
    bbr = fr[..., None] * br - fi[..., None] * bi
    bbi = fr[..., None] * bi + fi[..., None] * br
    eye = jnp.eye(B_GROUPS, dtype=F32)
    bd = lambda m: jnp.einsum("gpc,gh->gchp", m, eye).reshape(BR_WIDTH, B_LANES)
    cd = lambda m: jnp.einsum("gcp,gh->gphc", m.astype(F32), eye).reshape(B_LANES, BR_WIDTH)
    bdr, bdi = bd(bbr), bd(bbi)
    hi_lo = lambda m: (m.astype(BF16), (m - m.astype(BF16).astype(F32)).astype(BF16))
    bdr_h, bdr_l = hi_lo(bdr)
    bdi_h, bdi_l = hi_lo(bdi)
    nsb = S5_BLOCKS
    diag_b = lambda m: jnp.stack([m[s * (BR_WIDTH // nsb):(s + 1) * (BR_WIDTH // nsb),
                                    s * (B_LANES // nsb):(s + 1) * (B_LANES // nsb)] for s in range(nsb)])
    diag_c = lambda m: jnp.stack([m[s * (B_LANES // nsb):(s + 1) * (B_LANES // nsb),
                                    s * (BR_WIDTH // nsb):(s + 1) * (BR_WIDTH // nsb)] for s in range(nsb)])
    return dict(bdr=bdr_h, bdr_lo=bdr_l, bdi=bdi_h, bdi_lo=bdi_l,
                bdr4=diag_b(bdr_h), bdi4=diag_b(bdi_h), cdr4=diag_c(cd(c_re).astype(BF16)), cdi4=diag_c(cd(c_im).astype(BF16)),
                ar=ar.reshape(1, B_LANES), ai=ai.reshape(1, B_LANES),
                cdr=cd(c_re).astype(BF16), cdi=cd(c_im).astype(BF16), d=d.astype(F32).reshape(1, BR_WIDTH),
                glu_w=glu_w.astype(BF16), glu_b=glu_b.astype(F32).reshape(1, BR_WIDTH))


def _overlap(n_cmp, n_blk, rows, cols):
    start = np.arange(n_cmp)[:, None] * C_CMP_STRIDE
    blk = np.arange(n_blk)[None, :]
    m = (start <= (blk + 1) * C_SLC_BLOCK - 1) & (start + C_CMP_LEN - 1 >= blk * C_SLC_BLOCK)
    out = np.zeros((rows, cols), np.float32)
    out[:n_cmp, :n_blk] = m
    return jnp.asarray(out, BF16)


def _expand(n_keys):
    e = (np.arange(LANES)[:, None] == (np.arange(n_keys)[None, :] // C_SLC_BLOCK)).astype(np.float32)
    return jnp.asarray(e, BF16)


def _tri(lower=False):
    i = np.arange(LANES)
    m = (i[:, None] > i[None, :]) if lower else (i[:, None] < i[None, :])
    return jnp.asarray(m.astype(np.float32), BF16)


def _make_consts(T, sb, past):
    n_cmp_p = (T - C_CMP_LEN) // C_CMP_STRIDE + 1
    n_blk_p = -(-T // C_SLC_BLOCK)
    n_cmp_s = (past + 1 - C_CMP_LEN) // C_CMP_STRIDE + 1
    n_blk_s = -(-(past + 1) // C_SLC_BLOCK)
    return dict(
        tabs_p=_rope_tables(jnp.arange(T)),
        tabs_s=_rope_tables(jnp.full((sb,), past)),
        tri=_tri(),
        tril=_tri(lower=True),
        ov_p=_overlap(n_cmp_p, n_blk_p, T // C_CMP_STRIDE, LANES),
        ex_p=_expand(T),
        ov_s=_overlap(n_cmp_s, n_blk_s, past // C_CMP_STRIDE, -(-n_blk_s // LANES) * LANES),
    )


def _pick_tile(n, cands):
    for c in cands:
        if n % c == 0:
            return c
    return n


def _pad_rows(a, rows):
    return jnp.pad(a[:, None, :], ((0, 0), (0, rows - 1), (0, 0)))


def _layer_weights(l, norm_mix, w_in, a_gq, a_gk, c_gq, c_gk, c_cmp_a, c_cmp_w, d_conv_w, d_conv_b, d_ln_g, d_ln_b,
                   w_br, w_o, norm_mlp, w_up, w_down):
    row = lambda v: v.astype(F32).reshape(1, -1)
    return dict(
        g_mix=row(norm_mix[l]), w1=_pack_w1(jnp.transpose(w_in[l])),
        wg=jnp.transpose(w_in[l])[_O[11]:_O[12]].astype(BF16),
        a_gq=row(a_gq[l]), a_gk=row(a_gk[l]), c_gq=row(c_gq[l]), c_gk=c_gk[l].astype(F32),
        cmp_a=c_cmp_a[l].astype(F32), cmp_w=c_cmp_w[l].astype(BF16),
        conv_w=jnp.pad(d_conv_w[l].astype(F32), ((0, 1), (0, 0))), conv_b=row(d_conv_b[l]),
        ln_g=row(d_ln_g[l]), ln_b=row(d_ln_b[l]),
        w_br=w_br[l].astype(BF16), w_o=w_o[l].astype(BF16), g_mlp=row(norm_mlp[l]),
        w_up=w_up[l].astype(BF16), w_down=w_down[l].astype(BF16))


def _prompt_layer(x, lw, sp, consts):
    bsz, T, _ = x.shape
    n = bsz * T
    x2d = x.reshape(n, D_MODEL)
    tm = _pick_tile(T, (256, 128))
    qb = _pick_tile(T, (256, 128))
    tp = _pick_tile(T, (512, 256, 128))
    nt = T // tp
    bu_spec = pl.BlockSpec((tp, BR_WIDTH), lambda i: (i % nt, i // nt))
    a_t = ((bsz, A_CACHE_DIM, T), pl.BlockSpec((None, A_CACHE_DIM, tp), lambda i: (i // nt, 0, i % nt)))
    (xn, aq, arow, aiq, aiw, bu, cq, crow, wrow, cg, da, arow_t) = _project(
        x2d, consts["tabs_p"], lw, tm=tp, n_pos_blocks=nt, bu_shape=(T, bsz * BR_WIDTH), bu_spec=bu_spec, a_rows_t=a_t)
    arow3 = arow.reshape(bsz, T, A_CACHE_DIM)
    crow3 = crow.reshape(bsz, T, C_CACHE_DIM)
    wrow3 = wrow.reshape(bsz, T, C_WIN_DIM)
    da3 = da.reshape(bsz, T, BR_WIDTH)
    o_a = _dsa(aq, aiq, aiw, arow3, consts["tri"], qb=qb)
    kcmp, vcmp = _compress(crow3, lw["cmp_a"], lw["cmp_w"])
    o_c = _nsa(cq, cg, crow3, wrow3, kcmp, vcmp, consts["ov_p"], consts["ex_p"], qb=qb)
    tc = _pick_tile(T, (128, 64))
    o_b, hr, hi = _s5(bu.reshape(T * bsz, BR_WIDTH), sp, nb=bsz, tc=tc)
    o_d = _conv(da3, lw["conv_w"], lw["conv_b"], lw["ln_g"], lw["ln_b"], tm=tm)
    tmx = _pick_tile(T, (512, 256, 128))
    ntx = T // tmx
    ob_spec = pl.BlockSpec((tmx, BR_WIDTH), lambda r, i: (r % ntx, r // ntx))
    hm = _mix(xn, o_a, o_b.reshape(T, bsz * BR_WIDTH), ob_spec, o_c, o_d, lw["wg"], lw["w_br"], lw["w_o"], tm=tmx)
    y = _mlp(x2d, hm, lw["g_mlp"], lw["w_up"], lw["w_down"], tm=tmx)
    wk = min(C_WINDOW, T)
    return (y.reshape(bsz, T, D_MODEL), jnp.swapaxes(arow_t, 1, 2), crow3, wrow3[:, T - wk:],
            hr.reshape(bsz, B_GROUPS, B_STATE), hi.reshape(bsz, B_GROUPS, B_STATE), da3[:, T - (D_CONV - 1):])


def _sample_layer(x, l, cache_a, cache_c, cache_c_win, h_re, h_im, conv_l, page_table, lw, sp, consts):
    bsz = x.shape[0]
    cache_a_t = jnp.swapaxes(cache_a, 2, 3)
    x2d = x.reshape(bsz, D_MODEL)
    row = lambda w: pl.BlockSpec((bsz, w), lambda i: (0, 0))
    (xn, aq, arow, aiq, aiw, bu, cq, crow, wrow, cg, da) = _project(
        x2d, consts["tabs_s"], lw, tm=bsz, n_pos_blocks=1, bu_shape=(bsz, BR_WIDTH), bu_spec=row(BR_WIDTH))
    q8 = jnp.pad(aq.astype(F32).reshape(bsz, A_HEADS, HEAD_DIM), ((0, 0), (0, SUBLANES - A_HEADS), (0, 0)))
    iq8 = jnp.sum(aiq.astype(F32).reshape(bsz, A_IDX_HEADS, 2, A_IDX_DIM), axis=2)
    iq8 = jnp.pad(iq8, ((0, 0), (0, 0), (LANES - A_IDX_DIM, 0)))
    w8 = jnp.broadcast_to(aiw[:, :A_IDX_HEADS, None], (bsz, A_IDX_HEADS, LANES))
    o_a = _dsa_sample(page_table, cache_a_t, l, q8, iq8, w8, _pad_rows(arow, SUBLANES), consts["tri"], consts["tril"])
    o_a = o_a[:, :A_HEADS].reshape(bsz, BR_WIDTH).astype(BF16)
    cq8 = jnp.pad(cq.astype(F32).reshape(bsz, C_HEADS, HEAD_DIM), ((0, 0), (0, SUBLANES - C_HEADS), (0, 0)))
    g3 = jnp.transpose(cg[:, :3 * C_HEADS].reshape(bsz, C_HEADS, 3), (0, 2, 1))
    g3 = jnp.broadcast_to(jnp.pad(g3, ((0, 0), (0, 0), (0, SUBLANES - C_HEADS)))[..., None], (bsz, 3, SUBLANES, LANES))
    o_c = _nsa_sample(page_table, cache_c, l, cq8, g3, _pad_rows(crow, SUBLANES), cache_c_win,
                      _pad_rows(wrow, SUBLANES), lw["cmp_a"], lw["cmp_w"], consts["ov_s"])
    o_c = o_c[:, :C_HEADS].reshape(bsz, BR_WIDTH).astype(BF16)
    o_b, hr, hi, o_d = _sample_step(bu, h_re.reshape(bsz, B_LANES), h_im.reshape(bsz, B_LANES), sp, da, conv_l,
                                    lw["conv_w"], lw["conv_b"], lw["ln_g"], lw["ln_b"])
    hm = _mix(xn, o_a, o_b, pl.BlockSpec((bsz, BR_WIDTH), lambda r, i: (r, 0)), o_c, o_d,
              lw["wg"], lw["w_br"], lw["w_o"], tm=bsz)
    y = _mlp(x2d, hm, lw["g_mlp"], lw["w_up"], lw["w_down"], tm=bsz)
    new_win = jnp.concatenate([cache_c_win[l][:, 1:], wrow[:, None, :]], axis=1)
    new_conv = jnp.concatenate([conv_l[:, 1:], da[:, None, :]], axis=1)
    return (y.reshape(bsz, 1, D_MODEL), arow[:, None, :], crow[:, None, :], new_win,
            hr.reshape(bsz, B_GROUPS, B_STATE), hi.reshape(bsz, B_GROUPS, B_STATE), new_conv)


def kernel(x_prompt, x_sample, cache_a, cache_c, cache_c_win, state_b_re, state_b_im, state_d_conv, page_table, norm_mix, w_in, a_gq, a_gk, b_lam_re, b_lam_im, b_log_dt, b_b_re, b_b_im, b_c_re, b_c_im, b_d, b_glu_w, b_glu_b, c_gq, c_gk, c_cmp_a, c_cmp_w, d_conv_w, d_conv_b, d_ln_g, d_ln_b, w_br, w_o, norm_mlp, w_up, w_down):
    depth = w_in.shape[0]
    bsz, T, _ = x_prompt.shape
    sb, st, _ = x_sample.shape
    assert st == 1 and bsz == SUBLANES
    assert cache_a.shape[2] == PAGE and cache_c.shape[2] == PAGE
    consts = _make_consts(T, sb, page_table.shape[1] * PAGE)
    xp, xs = x_prompt, x_sample
    order_p = (0, 2, 4, 6, 7, 10)
    order_s = (1, 3, 5, 8, 9, 11)
    outs = [[] for _ in range(12)]
    for l in range(depth):
        lw = _layer_weights(l, norm_mix, w_in, a_gq, a_gk, c_gq, c_gk, c_cmp_a, c_cmp_w, d_conv_w, d_conv_b,
                            d_ln_g, d_ln_b, w_br, w_o, norm_mlp, w_up, w_down)
        sp = _s5_params(b_lam_re[l], b_lam_im[l], b_log_dt[l], b_b_re[l], b_b_im[l], b_c_re[l], b_c_im[l],
                        b_d[l], b_glu_w[l], b_glu_b[l])
        xp, *rp = _prompt_layer(xp, lw, sp, consts)
        xs, *rs = _sample_layer(xs, l, cache_a, cache_c, cache_c_win, state_b_re[l], state_b_im[l], state_d_conv[l],
                                page_table, lw, sp, consts)
        for k in range(6):
            outs[order_p[k]].append(rp[k])
            outs[order_s[k]].append(rs[k])
    return (xp, xs) + tuple(jnp.stack(o) for o in outs)
```

```python
import functools
import math

import numpy as np
import jax
import jax.numpy as jnp
from jax import lax
from jax.experimental import pallas as pl
from jax.experimental.pallas import tpu as pltpu

F32 = jnp.float32
BF16 = jnp.bfloat16
I32 = jnp.int32

D_MODEL = 2048
HEAD_DIM = 128
N_BRANCH = 4
BR_WIDTH = D_MODEL // N_BRANCH
ROPE_THETA = 10000.0
NORM_EPS = 1e-6
A_HEADS = BR_WIDTH // HEAD_DIM
A_IDX_HEADS = 8
A_IDX_DIM = 64
A_TOPK = 256
B_GROUP = 16
B_GROUPS = BR_WIDTH // B_GROUP
B_STATE = 64
B_LANES = B_GROUPS * B_STATE
C_HEADS = BR_WIDTH // HEAD_DIM
C_CMP_STRIDE = 16
C_CMP_LEN = 2 * C_CMP_STRIDE
C_SLC_BLOCK = 64
C_TOPN = 16
C_WINDOW = 512
C_FORCE = 1e4
D_CONV = 31
D_FF = 4 * D_MODEL
A_CACHE_DIM = 2 * HEAD_DIM + A_IDX_DIM
C_CACHE_DIM = 4 * HEAD_DIM
C_WIN_DIM = 2 * HEAD_DIM
PAGE = 128

LANES = 128
SUBLANES = 8
VMEM_LIMIT_MB = 56

_W = (A_HEADS * HEAD_DIM, HEAD_DIM, HEAD_DIM, A_IDX_HEADS * A_IDX_DIM, A_IDX_DIM, A_IDX_HEADS,
      BR_WIDTH, C_HEADS * HEAD_DIM, 6 * HEAD_DIM, 3 * C_HEADS, 2 * BR_WIDTH, N_BRANCH * D_MODEL)
_O = tuple(int(v) for v in np.cumsum((0,) + _W))
_P = {}
_cur = 0
for _name, _w in (("aq", 512), ("ak", 128), ("av", 128), ("aiq", 512), ("aik", 128), ("aiw", 128),
                  ("bu", 512), ("cq", 512), ("ckv", 768), ("cg", 128), ("dglu", 1024)):
    _P[_name] = (_cur, _w)
    _cur += _w
P_TOTAL = _cur

NEG = -1e30
ATT_SCALE = HEAD_DIM ** -0.5
SIGN = -2 ** 31


def _cparams(n_axes):
    return pltpu.CompilerParams(dimension_semantics=("arbitrary",) * n_axes,
                                vmem_limit_bytes=VMEM_LIMIT_MB * 1024 * 1024)


def _full(a, n_grid):
    nd = a.ndim
    return pl.BlockSpec(a.shape, lambda *_: (0,) * nd)


def _dot(a, b):
    return jnp.dot(a, b, preferred_element_type=F32)


def _dot_nt(a, b):
    return lax.dot_general(a, b, (((1,), (1,)), ((), ())), preferred_element_type=F32)


def _dot_split3(p, m):
    hi = p.astype(BF16)
    r = p - hi.astype(F32)
    mid = r.astype(BF16)
    lo = (r - mid.astype(F32)).astype(BF16)
    return _dot(hi, m) + _dot(mid, m) + _dot(lo, m)


def _rms(x, g):
    return x * lax.rsqrt(jnp.mean(x * x, axis=-1, keepdims=True) + NORM_EPS) * g


def _sigmoid(x):
    return 1.0 / (1.0 + jnp.exp(-x))


def _gelu(x):
    return x * (0.5 * (1.0 + jnp.tanh(math.sqrt(2.0 / math.pi) * (x + 0.044715 * (x * x * x)))))


def _sortable(x):
    b = pltpu.bitcast(x + 0.0, I32)
    return jnp.where(b < 0, b ^ jnp.int32(0x7FFFFFFF), b)


def _kth_key(key_ref, k, red_axes):
    shp = tuple(1 if a in red_axes else s for a, s in enumerate(key_ref.shape))

    def count(mask):
        c = jnp.where(mask, 1.0, 0.0)
        for a in sorted(red_axes):
            c = jnp.sum(c, axis=a, keepdims=True)
        return c

    def body(it, tu):
        cand_u = tu | jnp.left_shift(jnp.int32(1), 31 - it)
        cand_s = cand_u ^ jnp.int32(SIGN)
        return jnp.where(count(key_ref[...] >= cand_s) >= k, cand_u, tu)

    tu = lax.fori_loop(0, 32, body, jnp.zeros(shp, I32), unroll=4)
    return tu ^ jnp.int32(SIGN), count


def _select_rows(key_ref, k, tri):
    ts, count = _kth_key(key_ref, k, (1,))
    keys = key_ref[...]
    gt = keys > ts
    need = k - count(gt)
    eqf = jnp.where(keys == ts, 1.0, 0.0)
    base = jnp.zeros_like(need)
    pieces = []
    for c in range(keys.shape[1] // LANES):
        ch = eqf[:, c * LANES:(c + 1) * LANES]
        pref = _dot(ch.astype(BF16), tri) + base
        pieces.append(jnp.where(pref < need, ch, 0.0))
        base = base + jnp.sum(ch, axis=-1, keepdims=True)
    sel_eq = pieces[0] if len(pieces) == 1 else jnp.concatenate(pieces, axis=1)
    return jnp.where(gt, 1.0, sel_eq)


def _select_packed(key_ref, k, tri, tril):
    shp = key_ref.shape
    ts, count = _kth_key(key_ref, k, (0, 1, 2))
    keys = key_ref[...]
    gt = keys > ts
    need = k - count(gt)
    eqf = jnp.where(keys == ts, 1.0, 0.0)
    eq2 = eqf.reshape(LANES, LANES)
    within = _dot(eq2.astype(BF16), tri)
    tot = jnp.broadcast_to(jnp.sum(eq2, axis=1, keepdims=True), (LANES, LANES))
    base = _dot(tril, tot.astype(BF16))
    sel_eq = jnp.where((within + base).reshape(shp) < need, eqf, 0.0)
    return jnp.where(gt, 1.0, sel_eq)


def _select_rank(sc, n, k):
    lane = lax.broadcasted_iota(I32, sc.shape, 1)
    rank = jnp.zeros(sc.shape, F32)
    for i in range(n):
        col = sc[:, i:i + 1]
        rank = rank + jnp.where(lane > i, jnp.where(col >= sc, 1.0, 0.0), jnp.where(col > sc, 1.0, 0.0))
    return jnp.where(rank < k, 1.0, 0.0)


LOG2E = 1.4426950408889634


def _masked_attn(qh, k, v, maskf):
    s = jnp.where(maskf > 0.5, _dot_nt(qh, k) * (ATT_SCALE * LOG2E), NEG)
    e = jnp.exp2(s - jnp.max(s, axis=-1, keepdims=True)) * maskf
    den = jnp.maximum(jnp.sum(e, axis=-1, keepdims=True), 1e-30)
    return _dot(e.astype(BF16), v) / den


def _proj_kernel(x_ref, g_ref, w_ref, c128_ref, s128_ref, c64_ref, s64_ref, gqa_ref, gka_ref, gqc_ref, gkc_ref,
                 xn_ref, aq_ref, arow_ref, aiq_ref, aiw_ref, bu_ref, cq_ref, crow_ref, wrow_ref, cg_ref, da_ref,
                 arow_t_ref=None):
    xn = _rms(x_ref[...], g_ref[...]).astype(BF16)
    xn_ref[...] = xn
    cos, sin = c128_ref[...], s128_ref[...]
    cos64, sin64 = c64_ref[...], s64_ref[...]
    lane = lax.broadcasted_iota(I32, cos.shape, 1)
    lo32 = (lane & 63) < 32
    lo64 = lane < 64

    def seg(name):
        a, w = _P[name]
        return _dot_nt(xn, w_ref[a:a + w, :])

    def rope128(v):
        return v * cos + pltpu.roll(v, 64, 1) * sin

    def rope64(v):
        rot = jnp.where(lo32, pltpu.roll(v, 96, 1), pltpu.roll(v, 32, 1))
        return v * cos64 + rot * sin64

    z = seg("aq")
    for h in range(A_HEADS):
        sl = slice(h * HEAD_DIM, (h + 1) * HEAD_DIM)
        aq_ref[:, sl] = rope128(_rms(z[:, sl], gqa_ref[...])).astype(BF16)
    ak = rope128(_rms(seg("ak"), gka_ref[...]))
    av = seg("av")
    aik = rope64(seg("aik"))
    arow_ref[:, 0:128] = ak
    arow_ref[:, 128:256] = av
    arow_ref[:, 256:320] = aik[:, 0:A_IDX_DIM]
    if arow_t_ref is not None:
        arow_t_ref[0:128, :] = ak.T
        arow_t_ref[128:256, :] = av.T
        arow_t_ref[256:320, :] = aik.T[0:A_IDX_DIM, :]
    z = seg("aiq")
    for j in range(A_IDX_HEADS // 2):
        r = rope64(z[:, j * LANES:(j + 1) * LANES])
        aiq_ref[:, (2 * j) * LANES:(2 * j + 1) * LANES] = jnp.where(lo64, r, 0.0).astype(BF16)
        aiq_ref[:, (2 * j + 1) * LANES:(2 * j + 2) * LANES] = jnp.where(lo64, 0.0, r).astype(BF16)
    aiw_ref[...] = seg("aiw") * (A_IDX_HEADS ** -0.5) * (A_IDX_DIM ** -0.5)
    bu_ref[...] = seg("bu")
    z = seg("cq")
    for h in range(C_HEADS):
        sl = slice(h * HEAD_DIM, (h + 1) * HEAD_DIM)
        cq_ref[:, sl] = rope128(_rms(z[:, sl], gqc_ref[...])).astype(BF16)
    z = seg("ckv")
    for br in range(3):
        kk = rope128(_rms(z[:, (2 * br) * LANES:(2 * br + 1) * LANES], gkc_ref[br:br + 1, :]))
        vv = z[:, (2 * br + 1) * LANES:(2 * br + 2) * LANES]
        if br < 2:
            crow_ref[:, (2 * br) * LANES:(2 * br + 1) * LANES] = kk
            crow_ref[:, (2 * br + 1) * LANES:(2 * br + 2) * LANES] = vv
        else:
            wrow_ref[:, 0:LANES] = kk
            wrow_ref[:, LANES:2 * LANES] = vv
    cg_ref[...] = _sigmoid(seg("cg"))
    z = seg("dglu")
    da_ref[...] = z[:, 0:BR_WIDTH] * _sigmoid(z[:, BR_WIDTH:2 * BR_WIDTH])


def _project(x2d, tabs, lw, *, tm, n_pos_blocks, bu_shape, bu_spec, a_rows_t=None):
    n = x2d.shape[0]
    row = lambda w: pl.BlockSpec((tm, w), lambda i: (i, 0))
    tab = pl.BlockSpec((tm, LANES), lambda i: (i % n_pos_blocks, 0))
    ins = [x2d, lw["g_mix"], lw["w1"], tabs[0], tabs[1], tabs[2], tabs[3], lw["a_gq"], lw["a_gk"], lw["c_gq"], lw["c_gk"]]
    w_spec = pl.BlockSpec(ins[2].shape, lambda i: (0, 0), pipeline_mode=pl.Buffered(1))
    in_specs = [row(D_MODEL), _full(ins[1], 1), w_spec, tab, tab, tab, tab] + [_full(a, 1) for a in ins[7:]]
    outs = [((n, D_MODEL), BF16, row(D_MODEL)),
            ((n, 512), BF16, row(512)),
            ((n, A_CACHE_DIM), F32, row(A_CACHE_DIM)),
            ((n, 1024), BF16, row(1024)),
            ((n, LANES), F32, row(LANES)),
            (bu_shape, F32, bu_spec),
            ((n, 512), BF16, row(512)),
            ((n, C_CACHE_DIM), F32, row(C_CACHE_DIM)),
            ((n, C_WIN_DIM), F32, row(C_WIN_DIM)),
            ((n, LANES), F32, row(LANES)),
            ((n, BR_WIDTH), F32, row(BR_WIDTH))]
    if a_rows_t is not None:
        outs.append((a_rows_t[0], F32, a_rows_t[1]))
    return pl.pallas_call(
        _proj_kernel,
        grid=(n // tm,),
        in_specs=in_specs,
        out_specs=[o[2] for o in outs],
        out_shape=[jax.ShapeDtypeStruct(o[0], o[1]) for o in outs],
        compiler_params=_cparams(1),
        name="project",
    )(*ins)


def _dsa_kernel(aq_ref, aiq_ref, aiw_ref, arow_ref, tri_ref, *rest, qb, L, n_sel, q_first, q_count, aliased):
    o_ref, k_sc, v_sc, ik_sc, key_sc = rest[1:] if aliased else rest
    qi = pl.program_id(1) + q_first

    @pl.when(pl.program_id(1) == 0)
    def _():
        k_sc[...] = arow_ref[:, 0:128].astype(BF16)
        v_sc[...] = arow_ref[:, 128:256].astype(BF16)
        ik = arow_ref[:, 256:320]
        ik_sc[...] = jnp.concatenate([ik, ik], axis=1).astype(BF16)

    def body(le):
        qpos = qi * qb + lax.broadcasted_iota(I32, (qb, 1), 0)
        valid = lax.broadcasted_iota(I32, (qb, le), 1) <= qpos
        if le <= n_sel:
            mask = jnp.where(valid, 1.0, 0.0)
        else:
            w = aiw_ref[...]
            score = None
            for h in range(A_IDX_HEADS):
                lg = _dot_nt(aiq_ref[:, h * LANES:(h + 1) * LANES], ik_sc[0:le, :])
                t = jnp.maximum(lg, 0.0) * w[:, h:h + 1]
                score = t if score is None else score + t
            keys = key_sc.at[:, 0:le]
            keys[...] = _sortable(jnp.where(valid, score, -jnp.inf))
            mask = jnp.where(valid, _select_rows(keys, n_sel, tri_ref[...]), 0.0)
        q = aq_ref[...]
        for h in range(A_HEADS):
            sl = slice(h * HEAD_DIM, (h + 1) * HEAD_DIM)
            o_ref[:, sl] = _masked_attn(q[:, sl], k_sc[0:le, :], v_sc[0:le, :], mask).astype(BF16)

    _causal_branches(qi, qb, L, body, q_range=(q_first, q_first + q_count))


def _causal_branches(qi, qb, L, body, q_range=None):
    step = max(qb, L // 8)
    if L % step:
        step = qb
    per = step // qb
    for j in range(L // step):
        if q_range is not None and ((j + 1) * per <= q_range[0] or j * per >= q_range[1]):
            continue

        @pl.when((qi >= j * per) & (qi < (j + 1) * per))
        def _(j=j):
            body((j + 1) * step)


def _dsa(aq, aiq, aiw, arow3, tri, *, qb):
    bsz, L, _ = arow3.shape
    nq = L // qb
    bounds = sorted({0, nq // 2, (3 * nq) // 4, nq})
    n_sel = min(A_TOPK, L // 4)
    out = None
    for q0, q1 in zip(bounds[:-1], bounds[1:]):
        npq = q1 - q0
        row = lambda w, q0=q0: pl.BlockSpec((qb, w), lambda b, i: (b * nq + q0 + i, 0))
        ins = [aq, aiq, aiw, arow3, tri]
        in_specs = [row(512), row(1024), row(LANES),
                    pl.BlockSpec((None, L, A_CACHE_DIM), lambda b, i: (b, 0, 0)), _full(tri, 2)]
        aliases = {}
        if out is not None:
            ins.append(out)
            in_specs.append(pl.BlockSpec(memory_space=pl.ANY))
            aliases = {len(ins) - 1: 0}
        out = pl.pallas_call(
            functools.partial(_dsa_kernel, qb=qb, L=L, n_sel=n_sel, q_first=q0, q_count=npq,
                              aliased=out is not None),
            grid=(bsz, npq),
            in_specs=in_specs,
            out_specs=row(512),
            out_shape=jax.ShapeDtypeStruct((bsz * L, 512), BF16),
            scratch_shapes=[pltpu.VMEM((L, 128), BF16), pltpu.VMEM((L, 128), BF16), pltpu.VMEM((L, 128), BF16),
                            pltpu.VMEM((qb, L), I32)],
            input_output_aliases=aliases,
            compiler_params=_cparams(2),
            name="dsa",
        )(*ins)
    return out


def _summaries(x, a, w, n_cmp):
    ns = x.shape[0] // C_CMP_STRIDE
    x3 = x.reshape(ns, C_CMP_STRIDE, HEAD_DIM)
    lo = jnp.sum(x3 * a[0:C_CMP_STRIDE][None], axis=1)
    hi = jnp.sum(x3 * a[C_CMP_STRIDE:C_CMP_LEN][None], axis=1)
    return lo, hi


def _finish_summaries(lo, hi, w, n_cmp):
    ns = lo.shape[0]
    comb = lo + pltpu.roll(hi, ns - 1, 0)
    comb = jnp.where(lax.broadcasted_iota(I32, comb.shape, 0) < n_cmp, comb, 0.0)
    return _dot(comb.astype(BF16), w).astype(BF16)


def _cmp_kernel(x_ref, a_ref, w_ref, kc_ref, vc_ref, *, n_cmp):
    for t, out in ((0, kc_ref), (1, vc_ref)):
        lo, hi = _summaries(x_ref[:, t * LANES:(t + 1) * LANES], a_ref[t], w_ref[t], n_cmp)
        out[...] = _finish_summaries(lo, hi, w_ref[t], n_cmp)


def _compress(crow3, cmp_a, cmp_w):
    bsz, L, _ = crow3.shape
    ns = L // C_CMP_STRIDE
    n_cmp = (L - C_CMP_LEN) // C_CMP_STRIDE + 1
    out = pl.BlockSpec((None, ns, HEAD_DIM), lambda b: (b, 0, 0))
    return pl.pallas_call(
        functools.partial(_cmp_kernel, n_cmp=n_cmp),
        grid=(bsz,),
        in_specs=[pl.BlockSpec((None, L, C_CACHE_DIM), lambda b: (b, 0, 0)), _full(cmp_a, 1), _full(cmp_w, 1)],
        out_specs=[out, out],
        out_shape=[jax.ShapeDtypeStruct((bsz, ns, HEAD_DIM), BF16)] * 2,
        compiler_params=_cparams(1),
        name="compress",
    )(crow3, cmp_a, cmp_w)


def _block_scores(imp, qpos):
    j = lax.broadcasted_iota(I32, imp.shape, 1)
    cur = lax.shift_right_logical(qpos, 6)
    forced = (j == 0) | (j == cur) | (j == cur - 1)
    return jnp.where(j <= cur, jnp.where(forced, C_FORCE, imp), -jnp.inf)


def _cmp_softmax(qh, kc, cvalid):
    s = _dot_nt(qh, kc) * ATT_SCALE
    s = jnp.where(cvalid, s, NEG)
    m = jnp.max(s, axis=-1, keepdims=True)
    e = jnp.where(cvalid, jnp.exp(s - m), 0.0)
    return e / jnp.maximum(jnp.sum(e, axis=-1, keepdims=True), 1e-30)


def _nsa_kernel(cq_ref, cg_ref, crow_ref, wrow_ref, kc_ref, vc_ref, ov_ref, ex_ref, *rest,
                qb, L, n_cmp, n_blk, n_top, wsl, q_first, q_count, aliased):
    o_ref, ks_sc, vs_sc, kw_sc, vw_sc, os_sc = rest[1:] if aliased else rest
    qi = pl.program_id(1) + q_first

    @pl.when(pl.program_id(1) == 0)
    def _():
        ks_sc[...] = crow_ref[:, 256:384].astype(BF16)
        vs_sc[...] = crow_ref[:, 384:512].astype(BF16)
        kw_sc[...] = wrow_ref[:, 0:128].astype(BF16)
        vw_sc[...] = wrow_ref[:, 128:256].astype(BF16)

    q0 = qi * qb
    qpos = q0 + lax.broadcasted_iota(I32, (qb, 1), 0)
    q = cq_ref[...]
    g = cg_ref[...]
    kc, vc = kc_ref[...], vc_ref[...]
    ncp = kc.shape[0]
    n_io = lax.broadcasted_iota(I32, (qb, ncp), 1)
    cvalid = (n_io < n_cmp) & (n_io * C_CMP_STRIDE + (C_CMP_LEN - 1) <= qpos)
    o_c, psum = [], None
    for h in range(C_HEADS):
        p = _cmp_softmax(q[:, h * HEAD_DIM:(h + 1) * HEAD_DIM], kc, cvalid)
        o_c.append(_dot(p.astype(BF16), vc))
        psum = p if psum is None else psum + p
    imp = _dot_split3(psum, ov_ref[...])
    sel = _select_rank(_block_scores(imp, qpos), n_blk, n_top).astype(BF16)

    def selected(le):
        selk = _dot(sel, ex_ref[:, 0:le])
        smask = jnp.where(lax.broadcasted_iota(I32, (qb, le), 1) <= qpos, selk, 0.0)
        for h in range(C_HEADS):
            sl = slice(h * HEAD_DIM, (h + 1) * HEAD_DIM)
            os_sc[:, sl] = _masked_attn(q[:, sl], ks_sc[0:le, :], vs_sc[0:le, :], smask)

    _causal_branches(qi, qb, L, selected, q_range=(q_first, q_first + q_count))
    start = pl.multiple_of(jnp.minimum(jnp.maximum(q0 - C_WINDOW, 0), L - wsl), qb)
    dist = qpos - (start + lax.broadcasted_iota(I32, (qb, wsl), 1))
    wmask = jnp.where((dist >= 0) & (dist <= C_WINDOW), 1.0, 0.0)
    kw = kw_sc[pl.ds(start, wsl), :]
    vw = vw_sc[pl.ds(start, wsl), :]
    for h in range(C_HEADS):
        sl = slice(h * HEAD_DIM, (h + 1) * HEAD_DIM)
        o_w = _masked_attn(q[:, sl], kw, vw, wmask)
        out = g[:, 3 * h:3 * h + 1] * o_c[h] + g[:, 3 * h + 1:3 * h + 2] * os_sc[:, sl] + g[:, 3 * h + 2:3 * h + 3] * o_w
        o_ref[:, sl] = out.astype(BF16)


def _nsa(cq, cg, crow3, wrow3, kcmp, vcmp, ov, ex, *, qb):
    bsz, L, _ = crow3.shape
    nq = L // qb
    ns = kcmp.shape[1]
    n_cmp = (L - C_CMP_LEN) // C_CMP_STRIDE + 1
    n_blk = -(-L // C_SLC_BLOCK)
    wsl = min(L, C_WINDOW + qb)
    per_b = lambda r, w: pl.BlockSpec((None, r, w), lambda b, i: (b, 0, 0))
    bounds = sorted({0, nq // 2, (3 * nq) // 4, nq})
    out = None
    for q0, q1 in zip(bounds[:-1], bounds[1:]):
        row = lambda w, q0=q0: pl.BlockSpec((qb, w), lambda b, i: (b * nq + q0 + i, 0))
        ins = [cq, cg, crow3, wrow3, kcmp, vcmp, ov, ex]
        in_specs = [row(512), row(LANES), per_b(L, C_CACHE_DIM), per_b(L, C_WIN_DIM),
                    per_b(ns, HEAD_DIM), per_b(ns, HEAD_DIM), _full(ov, 2), _full(ex, 2)]
        aliases = {}
        if out is not None:
            ins.append(out)
            in_specs.append(pl.BlockSpec(memory_space=pl.ANY))
            aliases = {len(ins) - 1: 0}
        out = pl.pallas_call(
            functools.partial(_nsa_kernel, qb=qb, L=L, n_cmp=n_cmp, n_blk=n_blk, n_top=min(C_TOPN, n_blk), wsl=wsl,
                              q_first=q0, q_count=q1 - q0, aliased=out is not None),
            grid=(bsz, q1 - q0),
            in_specs=in_specs,
            out_specs=row(512),
            out_shape=jax.ShapeDtypeStruct((bsz * L, 512), BF16),
            scratch_shapes=[pltpu.VMEM((L, 128), BF16)] * 4 + [pltpu.VMEM((qb, 512), F32)],
            input_output_aliases=aliases,
            compiler_params=_cparams(2),
            name="nsa",
        )(*ins)
    return out


S5_BLOCKS = 4


def _s5_kernel(u_ref, bdr_ref, bdi_ref, ar_ref, ai_ref, cdr_ref, cdi_ref, d_ref, gw_ref, gb_ref,
               o_ref, hr_ref, hi_ref, xr_sc, xi_sc, h_sc, *, tc, nb):
    i = pl.program_id(0)

    @pl.when(i == 0)
    def _():
        h_sc[...] = jnp.zeros_like(h_sc)

    u = u_ref[...]
    ub = u.astype(BF16)
    nsb = bdr_ref.shape[0]
    wu, wx = BR_WIDTH // nsb, B_LANES // nsb
    for sb in range(nsb):
        us = ub[:, sb * wu:(sb + 1) * wu]
        xr_sc[:, sb * wx:(sb + 1) * wx] = _dot(us, bdr_ref[sb])
        xi_sc[:, sb * wx:(sb + 1) * wx] = _dot(us, bdi_ref[sb])
    ar = jnp.broadcast_to(ar_ref[...], (nb, B_LANES))
    ai = jnp.broadcast_to(ai_ref[...], (nb, B_LANES))

    def step(t, carry):
        hr, hi = carry
        r0 = pl.multiple_of(t * nb, nb)
        nhr = ar * hr - ai * hi + xr_sc[pl.ds(r0, nb), :]
        nhi = ar * hi + ai * hr + xi_sc[pl.ds(r0, nb), :]
        xr_sc[pl.ds(r0, nb), :] = nhr
        xi_sc[pl.ds(r0, nb), :] = nhi
        return nhr, nhi

    hr, hi = lax.fori_loop(0, tc, step, (h_sc[0], h_sc[1]))
    h_sc[0] = hr
    h_sc[1] = hi
    hr_ref[...] = hr
    hi_ref[...] = hi
    ch = [_dot(xr_sc[:, sb * wx:(sb + 1) * wx].astype(BF16), cdr_ref[sb])
          - _dot(xi_sc[:, sb * wx:(sb + 1) * wx].astype(BF16), cdi_ref[sb]) for sb in range(nsb)]
    y = jnp.concatenate(ch, axis=1) + d_ref[...] * u
    gl = _gelu(y)
    o_ref[...] = (gl * _sigmoid(_dot(gl.astype(BF16), gw_ref[...]) + gb_ref[...])).astype(BF16)


def _s5(u_tm, sp, *, nb, tc):
    rows = u_tm.shape[0]
    r = tc * nb
    consts = [sp["bdr4"], sp["bdi4"], sp["ar"], sp["ai"], sp["cdr4"], sp["cdi4"], sp["d"], sp["glu_w"], sp["glu_b"]]
    st = pl.BlockSpec((nb, B_LANES), lambda i: (0, 0))
    return pl.pallas_call(
        functools.partial(_s5_kernel, tc=tc, nb=nb),
        grid=(rows // r,),
        in_specs=[pl.BlockSpec((r, BR_WIDTH), lambda i: (i, 0))] + [_full(c, 1) for c in consts],
        out_specs=[pl.BlockSpec((r, BR_WIDTH), lambda i: (i, 0)), st, st],
        out_shape=[jax.ShapeDtypeStruct((rows, BR_WIDTH), BF16),
                   jax.ShapeDtypeStruct((nb, B_LANES), F32), jax.ShapeDtypeStruct((nb, B_LANES), F32)],
        scratch_shapes=[pltpu.VMEM((r, B_LANES), F32), pltpu.VMEM((r, B_LANES), F32), pltpu.VMEM((2, nb, B_LANES), F32)],
        compiler_params=_cparams(1),
        name="s5",
    )(u_tm, *consts)


HALO = 32


def _ln_swish(y, g, b):
    yc = y - jnp.mean(y, axis=-1, keepdims=True)
    yn = yc * lax.rsqrt(jnp.mean(yc * yc, axis=-1, keepdims=True) + NORM_EPS) * g + b
    return yn * _sigmoid(yn)


def _conv_kernel(cur_ref, halo_ref, w_ref, b_ref, lg_ref, lb_ref, o_ref, ext_sc, *, tm):
    i = pl.program_id(1)
    ext_sc[0, 0:HALO, :] = jnp.where(i == 0, 0.0, halo_ref[...])
    ext_sc[0, HALO:HALO + tm, :] = cur_ref[...]
    n = HALO + tm - SUBLANES
    for k in range(1, SUBLANES):
        ext_sc[k, 0:n, :] = ext_sc[0, pl.ds(k, n), :]
    acc = jnp.zeros((tm, BR_WIDTH), F32)
    for j in range(D_CONV):
        off = HALO - (D_CONV - 1) + j
        k = off % SUBLANES
        acc = acc + w_ref[j:j + 1, :] * ext_sc[k, off - k:off - k + tm, :]
    o_ref[...] = _ln_swish(acc + b_ref[...], lg_ref[...], lb_ref[...]).astype(BF16)


def _conv(da3, cw, cb, lg, lb, *, tm):
    bsz, T, _ = da3.shape
    nt = T // tm
    hb = tm // HALO
    consts = [cw, cb, lg, lb]
    return pl.pallas_call(
        functools.partial(_conv_kernel, tm=tm),
        grid=(bsz, nt),
        in_specs=[pl.BlockSpec((None, tm, BR_WIDTH), lambda b, i: (b, i, 0)),
                  pl.BlockSpec((None, HALO, BR_WIDTH), lambda b, i: (b, jnp.maximum(i * hb - 1, 0), 0))]
        + [_full(c, 2) for c in consts],
        out_specs=pl.BlockSpec((tm, BR_WIDTH), lambda b, i: (b * nt + i, 0)),
        out_shape=jax.ShapeDtypeStruct((bsz * T, BR_WIDTH), BF16),
        scratch_shapes=[pltpu.VMEM((SUBLANES, HALO + tm, BR_WIDTH), F32)],
        compiler_params=_cparams(2),
        name="conv",
    )(da3, da3, *consts)


MIX_CW = 512


def _mix_kernel(xn_ref, oa_ref, ob_ref, oc_ref, od_ref, wg_ref, wbr_ref, wo_ref, hm_ref, acc_sc):
    i = pl.program_id(1)

    @pl.when(i == 0)
    def _():
        acc_sc[...] = jnp.zeros_like(acc_sc)

    xn = xn_ref[...]
    br = jnp.where(i == 0, oa_ref[...], jnp.where(i == 1, ob_ref[...], jnp.where(i == 2, oc_ref[...], od_ref[...])))
    for c in range(D_MODEL // MIX_CW):
        sl = slice(c * MIX_CW, (c + 1) * MIX_CW)
        contrib = _sigmoid(_dot_nt(xn, wg_ref[sl, :])) * _dot(br, wbr_ref[:, sl])
        acc_sc[:, sl] = acc_sc[:, sl] + contrib

    @pl.when(i == N_BRANCH - 1)
    def _():
        hm_ref[...] = _dot(acc_sc[...].astype(BF16), wo_ref[...])


def _mix(xn, oa, ob, ob_spec, oc, od, wg, wbr, wo, *, tm):
    n = xn.shape[0]
    row = lambda w: pl.BlockSpec((tm, w), lambda r, i: (r, 0))
    return pl.pallas_call(
        _mix_kernel,
        grid=(n // tm, N_BRANCH),
        in_specs=[row(D_MODEL), row(BR_WIDTH), ob_spec, row(BR_WIDTH), row(BR_WIDTH),
                  pl.BlockSpec((D_MODEL, D_MODEL), lambda r, i: (i, 0)),
                  pl.BlockSpec((None, BR_WIDTH, D_MODEL), lambda r, i: (i, 0, 0)),
                  pl.BlockSpec((D_MODEL, D_MODEL), lambda r, i: (0, 0))],
        out_specs=row(D_MODEL),
        out_shape=jax.ShapeDtypeStruct((n, D_MODEL), F32),
        scratch_shapes=[pltpu.VMEM((tm, D_MODEL), F32)],
        compiler_params=_cparams(2),
        name="mix",
    )(xn, oa, ob, oc, od, wg, wbr, wo)


MLP_FC = 1024


def _mlp_kernel(x_ref, hm_ref, g_ref, wu_ref, wd_ref, y_ref, hn_sc, acc_sc):
    j = pl.program_id(1)

    @pl.when(j == 0)
    def _():
        hn_sc[...] = _rms(x_ref[...] + hm_ref[...], g_ref[...]).astype(BF16)
        acc_sc[...] = jnp.zeros_like(acc_sc)

    up = _dot(hn_sc[...], wu_ref[...])
    act = jnp.square(jnp.maximum(up, 0.0)).astype(BF16)
    d = _dot(act, wd_ref[...])
    acc_sc[...] = acc_sc[...] + d

    @pl.when(j == pl.num_programs(1) - 1)
    def _():
        y_ref[...] = (x_ref[...] + hm_ref[...]) + acc_sc[...]


def _mlp(x2d, hm, g, wu, wd, *, tm):
    n = x2d.shape[0]
    row = pl.BlockSpec((tm, D_MODEL), lambda r, j: (r, 0))
    return pl.pallas_call(
        _mlp_kernel,
        grid=(n // tm, D_FF // MLP_FC),
        in_specs=[row, row, _full(g, 2),
                  pl.BlockSpec((D_MODEL, MLP_FC), lambda r, j: (0, j)),
                  pl.BlockSpec((MLP_FC, D_MODEL), lambda r, j: (j, 0))],
        out_specs=row,
        out_shape=jax.ShapeDtypeStruct((n, D_MODEL), F32),
        scratch_shapes=[pltpu.VMEM((tm, D_MODEL), BF16), pltpu.VMEM((tm, D_MODEL), F32)],
        compiler_params=_cparams(2),
        name="mlp",
    )(x2d, hm, g, wu, wd)


PAGES_PER_STEP = 16


def _tile_attention(s_ref, bias_ref, v_ref, n, s_new, bias_new, v_new, v_transposed):
    c2 = ATT_SCALE * LOG2E
    s = s_ref[...] * c2 + bias_ref[...]
    sn = s_new * c2 + bias_new
    m = jnp.maximum(jnp.max(jnp.max(s, axis=0), axis=1, keepdims=True), sn)
    e = jnp.exp2(s - m)
    en = jnp.exp2(sn - m)
    den = jnp.sum(jnp.sum(e, axis=0), axis=1, keepdims=True) + en
    acc = en.astype(BF16).astype(F32) * v_new.astype(BF16).astype(F32)
    mm = _dot_nt if v_transposed else _dot
    for c in range(n):
        acc = acc + mm(e[c].astype(BF16), v_ref[c * LANES:(c + 1) * LANES, :])
    return acc / den


KEY_TILES = 16


def _dsa_s_kernel(pt_ref, q_ref, iq_ref, w_ref, new_ref, *rest, pg, n_pages, n_sel):
    pages = rest[:pg]
    tri_ref, tril_ref, o_ref, s_sc, bias_sc, vt_sc, sc_sc, key_sc = rest[pg:]
    g = pl.program_id(1)
    q = q_ref[...].astype(BF16)
    iq = iq_ref[...].astype(BF16)
    w = w_ref[...]

    @pl.when(g == 0)
    def _():
        sc_sc[...] = jnp.full(sc_sc.shape, -jnp.inf, F32)

    for i in range(pg):
        c = g * pg + i
        page = pages[i]
        s_sc[c] = _dot(q, page[0:128, :].astype(BF16))
        vt_sc[pl.ds(pl.multiple_of(c * LANES, LANES), LANES), :] = page[128:256, :].astype(BF16)
        lg = _dot(iq, page[192:320, :].astype(BF16))
        sc_sc[g * (pg // SUBLANES) + i // SUBLANES, i % SUBLANES:i % SUBLANES + 1, :] = jnp.sum(
            jnp.maximum(lg, 0.0) * w, axis=0, keepdims=True)

    @pl.when(g == pl.num_programs(1) - 1)
    def _():
        new = new_ref[...]
        k_new = new[0:1, 0:128].astype(BF16).astype(F32)
        s_new = jnp.sum(q.astype(F32) * k_new, axis=-1, keepdims=True)
        ik_new = new[0:1, 256:320].astype(BF16).astype(F32)
        lg_new = jnp.sum(iq[:, A_IDX_DIM:2 * A_IDX_DIM].astype(F32) * ik_new, axis=-1, keepdims=True)
        sc_new = jnp.sum(jnp.maximum(lg_new, 0.0) * w[:, 0:1], axis=0, keepdims=True)
        t_new = n_pages // SUBLANES
        first = (lax.broadcasted_iota(I32, (SUBLANES, LANES), 0) == 0) & (lax.broadcasted_iota(I32, (SUBLANES, LANES), 1) == 0)
        sc_sc[t_new] = jnp.where(first, sc_new, -jnp.inf)
        key_sc[...] = _sortable(sc_sc[...])
        sel = _select_packed(key_sc, n_sel, tri_ref[...], tril_ref[...])
        for c in range(n_pages):
            row = sel[c // SUBLANES][c % SUBLANES:c % SUBLANES + 1, :]
            bias_sc[c] = jnp.where(jnp.broadcast_to(row, (SUBLANES, LANES)) > 0.5, 0.0, NEG)
        bias_new = jnp.where(sel[t_new][0:1, 0:1] > 0.5, 0.0, NEG)
        o_ref[...] = _tile_attention(s_sc, bias_sc, vt_sc, n_pages, s_new, bias_new, new[0:1, 128:256], True)


def _dsa_sample(page_table, cache_t, layer, q8, iq8, w8, new8, tri, tril):
    bsz, n_pages = page_table.shape
    pg = PAGES_PER_STEP
    assert pg % SUBLANES == 0 and n_pages % pg == 0 and n_pages * PAGE + 1 <= KEY_TILES * SUBLANES * LANES
    n_sel = min(A_TOPK, (n_pages * PAGE + 1) // 4)
    per_b = lambda r, w: pl.BlockSpec((None, r, w), lambda b, g, pt: (b, 0, 0))
    cst = lambda a: pl.BlockSpec(a.shape, lambda b, g, pt: (0,) * a.ndim)
    page_spec = lambda i: pl.BlockSpec((None, None, A_CACHE_DIM, PAGE),
                                       lambda b, g, pt: (layer, pt[b, g * pg + i], 0, 0))
    grid_spec = pltpu.PrefetchScalarGridSpec(
        num_scalar_prefetch=1,
        grid=(bsz, n_pages // pg),
        in_specs=[per_b(SUBLANES, LANES), per_b(SUBLANES, LANES), per_b(SUBLANES, LANES), per_b(SUBLANES, A_CACHE_DIM)]
        + [page_spec(i) for i in range(pg)] + [cst(tri), cst(tril)],
        out_specs=per_b(SUBLANES, HEAD_DIM),
        scratch_shapes=[pltpu.VMEM((n_pages, SUBLANES, LANES), F32), pltpu.VMEM((n_pages, SUBLANES, LANES), F32),
                        pltpu.VMEM((n_pages * LANES, PAGE), BF16),
                        pltpu.VMEM((KEY_TILES, SUBLANES, LANES), F32), pltpu.VMEM((KEY_TILES, SUBLANES, LANES), I32)],
    )
    return pl.pallas_call(
        functools.partial(_dsa_s_kernel, pg=pg, n_pages=n_pages, n_sel=n_sel),
        grid_spec=grid_spec,
        out_shape=jax.ShapeDtypeStruct((bsz, SUBLANES, HEAD_DIM), F32),
        compiler_params=_cparams(2),
        name="dsa_sample",
    )(page_table, q8, iq8, w8, new8, *([cache_t] * pg), tri, tril)


def _nsa_s_kernel(pt_ref, q_ref, g3_ref, new_ref, win_ref, wnew_ref, a_ref, w_ref, ov_ref, *rest,
                  pg, n_pages, n_cmp, n_blk, n_top, past):
    pages = rest[:pg]
    o_ref, ss_sc, bias_sc, vs_sc, lok_sc, hik_sc, lov_sc, hiv_sc = rest[pg:]
    g = pl.program_id(1)
    q = q_ref[...].astype(BF16)
    sub = PAGE // C_CMP_STRIDE

    for i in range(pg):
        c = g * pg + i
        page = pages[i]
        r0 = pl.multiple_of(c * sub, sub)
        lo, hi = _summaries(page[:, 0:128], a_ref[0], None, n_cmp)
        lok_sc[pl.ds(r0, sub), :] = lo
        hik_sc[pl.ds(r0, sub), :] = hi
        lo, hi = _summaries(page[:, 128:256], a_ref[1], None, n_cmp)
        lov_sc[pl.ds(r0, sub), :] = lo
        hiv_sc[pl.ds(r0, sub), :] = hi
        ss_sc[c] = _dot_nt(q, page[:, 256:384].astype(BF16))
        vs_sc[pl.ds(pl.multiple_of(c * LANES, LANES), LANES), :] = page[:, 384:512].astype(BF16)

    @pl.when(g == pl.num_programs(1) - 1)
    def _():
        qf = q.astype(F32)
        new = new_ref[...]
        lane = lax.broadcasted_iota(I32, (SUBLANES, LANES), 1)
        qpos = jnp.full((SUBLANES, 1), past, I32)
        kc = _finish_summaries(lok_sc[...], hik_sc[...], w_ref[0], n_cmp)
        vc = _finish_summaries(lov_sc[...], hiv_sc[...], w_ref[1], n_cmp)
        ncp = kc.shape[0]
        n_io = lax.broadcasted_iota(I32, (SUBLANES, ncp), 1)
        cvalid = (n_io < n_cmp) & (n_io * C_CMP_STRIDE + (C_CMP_LEN - 1) <= qpos)
        p = _cmp_softmax(q, kc, cvalid)
        o_c = _dot(p.astype(BF16), vc)
        head = lax.broadcasted_iota(I32, p.shape, 0) < C_HEADS
        psum = jnp.broadcast_to(jnp.sum(jnp.where(head, p, 0.0), axis=0, keepdims=True), p.shape)
        imp = _dot_split3(psum, ov_ref[...])
        sel = _select_rank(_block_scores(imp, qpos), n_blk, n_top)
        for c in range(n_pages):
            pick = jnp.where(lane < C_SLC_BLOCK, sel[:, 2 * c:2 * c + 1], sel[:, 2 * c + 1:2 * c + 2])
            bias_sc[c] = jnp.where(pick > 0.5, 0.0, NEG)
        k_new = new[0:1, 256:384].astype(BF16).astype(F32)
        s_new = jnp.sum(qf * k_new, axis=-1, keepdims=True)
        bias_new = jnp.where(sel[:, 2 * n_pages:2 * n_pages + 1] > 0.5, 0.0, NEG)
        o_s = _tile_attention(ss_sc, bias_sc, vs_sc, n_pages, s_new, bias_new, new[0:1, 384:512], False)
        wb = win_ref.shape[0]
        kw = win_ref[:, 0:128].astype(BF16)
        vw = win_ref[:, 128:256].astype(BF16)
        wnew = wnew_ref[...]
        s_w = _dot_nt(q, kw) * ATT_SCALE
        dist = wb - lax.broadcasted_iota(I32, (SUBLANES, wb), 1)
        wvalid = (dist <= C_WINDOW) & (past - dist >= 0)
        s_w = jnp.where(wvalid, s_w, NEG)
        s_n = jnp.sum(qf * wnew[0:1, 0:128].astype(BF16).astype(F32), axis=-1, keepdims=True) * ATT_SCALE
        m = jnp.maximum(jnp.max(s_w, axis=-1, keepdims=True), s_n)
        e_w = jnp.where(wvalid, jnp.exp(s_w - m), 0.0)
        e_n = jnp.exp(s_n - m)
        den = jnp.sum(e_w, axis=-1, keepdims=True) + e_n
        v_n = wnew[0:1, 128:256].astype(BF16).astype(F32)
        o_w = (_dot(e_w.astype(BF16), vw) + e_n.astype(BF16).astype(F32) * v_n) / den
        o_ref[...] = g3_ref[0] * o_c + g3_ref[1] * o_s + g3_ref[2] * o_w


def _nsa_sample(page_table, cache, layer, q8, g3, new8, win, wnew8, cmp_a, cmp_w, ov):
    bsz, n_pages = page_table.shape
    pg = PAGES_PER_STEP
    past = n_pages * PAGE
    n_cmp = (past + 1 - C_CMP_LEN) // C_CMP_STRIDE + 1
    n_blk = -(-(past + 1) // C_SLC_BLOCK)
    ns = past // C_CMP_STRIDE
    wb = win.shape[2]
    cst = lambda a: pl.BlockSpec(a.shape, lambda b, g, pt: (0,) * a.ndim)
    per_b = lambda r, w: pl.BlockSpec((None, r, w), lambda b, g, pt: (b, 0, 0))
    page_spec = lambda i: pl.BlockSpec((None, None, PAGE, C_CACHE_DIM),
                                       lambda b, g, pt: (layer, pt[b, g * pg + i], 0, 0))
    grid_spec = pltpu.PrefetchScalarGridSpec(
        num_scalar_prefetch=1,
        grid=(bsz, n_pages // pg),
        in_specs=[per_b(SUBLANES, LANES),
                  pl.BlockSpec((None, 3, SUBLANES, LANES), lambda b, g, pt: (b, 0, 0, 0)),
                  per_b(SUBLANES, C_CACHE_DIM),
                  pl.BlockSpec((None, None, wb, C_WIN_DIM), lambda b, g, pt: (layer, b, 0, 0)),
                  per_b(SUBLANES, C_WIN_DIM), cst(cmp_a), cst(cmp_w), cst(ov)]
        + [page_spec(i) for i in range(pg)],
        out_specs=per_b(SUBLANES, HEAD_DIM),
        scratch_shapes=[pltpu.VMEM((n_pages, SUBLANES, LANES), F32), pltpu.VMEM((n_pages, SUBLANES, LANES), F32),
                        pltpu.VMEM((n_pages * LANES, HEAD_DIM), BF16)]
        + [pltpu.VMEM((ns, HEAD_DIM), F32)] * 4,
    )
    return pl.pallas_call(
        functools.partial(_nsa_s_kernel, pg=pg, n_pages=n_pages, n_cmp=n_cmp, n_blk=n_blk, n_top=min(C_TOPN, n_blk),
                          past=past),
        grid_spec=grid_spec,
        out_shape=jax.ShapeDtypeStruct((bsz, SUBLANES, HEAD_DIM), F32),
        compiler_params=_cparams(2),
        name="nsa_sample",
    )(page_table, q8, g3, new8, win, wnew8, cmp_a, cmp_w, ov, *([cache] * pg))


def _step_kernel(u_ref, h0r_ref, h0i_ref, bdr_h_ref, bdr_l_ref, bdi_h_ref, bdi_l_ref, ar_ref, ai_ref,
                 cdr_ref, cdi_ref, d_ref, gw_ref, gb_ref, da_ref, cst_ref, cw_ref, cb_ref, lg_ref, lb_ref,
                 ob_ref, hr_ref, hi_ref, od_ref):
    u = u_ref[...]
    uh = u.astype(BF16)
    ul = (u - uh.astype(F32)).astype(BF16)

    def bmat(h_ref, l_ref):
        return _dot(uh, h_ref[...]) + (_dot(uh, l_ref[...]) + _dot(ul, h_ref[...]))

    ar, ai = ar_ref[...], ai_ref[...]
    h0r, h0i = h0r_ref[...], h0i_ref[...]
    hr = bmat(bdr_h_ref, bdr_l_ref) + (ar * h0r - ai * h0i)
    hi = bmat(bdi_h_ref, bdi_l_ref) + (ar * h0i + ai * h0r)
    hr_ref[...] = hr
    hi_ref[...] = hi
    y = _dot(hr.astype(BF16), cdr_ref[...]) - _dot(hi.astype(BF16), cdi_ref[...]) + d_ref[...] * u
    gl = _gelu(y)
    ob_ref[...] = (gl * _sigmoid(_dot(gl.astype(BF16), gw_ref[...]) + gb_ref[...])).astype(BF16)
    cw = cw_ref[...]
    y = jnp.sum(cst_ref[...] * cw[0:D_CONV - 1][None], axis=1) + cw[D_CONV - 1:D_CONV] * da_ref[...] + cb_ref[...]
    od_ref[...] = _ln_swish(y, lg_ref[...], lb_ref[...]).astype(BF16)


def _sample_step(u, h0r, h0i, sp, da, conv_state, cw, cb, lg, lb):
    bsz = u.shape[0]
    ins = [u, h0r, h0i, sp["bdr"], sp["bdr_lo"], sp["bdi"], sp["bdi_lo"], sp["ar"], sp["ai"], sp["cdr"], sp["cdi"],
           sp["d"], sp["glu_w"], sp["glu_b"], da, conv_state, cw, cb, lg, lb]
    outs = [((bsz, BR_WIDTH), BF16), ((bsz, B_LANES), F32), ((bsz, B_LANES), F32), ((bsz, BR_WIDTH), BF16)]
    return pl.pallas_call(
        _step_kernel,
        grid=(1,),
        in_specs=[_full(a, 1) for a in ins],
        out_specs=[pl.BlockSpec(o[0], lambda i: (0, 0)) for o in outs],
        out_shape=[jax.ShapeDtypeStruct(o[0], o[1]) for o in outs],
        compiler_params=_cparams(1),
        name="sample_step",
    )(*ins)


def _rope_tables(pos):
    pos = pos.astype(F32)[:, None]

    def tab(dim):
        half = dim // 2
        inv = ROPE_THETA ** (-jnp.arange(half, dtype=F32) / half)
        ang = pos * inv
        cos, sin = jnp.cos(ang), jnp.sin(ang)
        reps = LANES // dim
        return jnp.tile(jnp.concatenate([cos, cos], axis=1), (1, reps)), jnp.tile(jnp.concatenate([-sin, sin], axis=1), (1, reps))

    c128, s128 = tab(HEAD_DIM)
    c64, s64 = tab(A_IDX_DIM)
    return c128, s128, c64, s64


def _pack_w1(wt):
    def padded(a, b, rows):
        return jnp.pad(wt[a:b], ((0, rows - (b - a)), (0, 0)))
    parts = [wt[_O[0]:_O[4]], padded(_O[4], _O[5], 128), padded(_O[5], _O[6], 128), wt[_O[6]:_O[9]],
             padded(_O[9], _O[10], 128), wt[_O[10]:_O[11]]]
    return jnp.concatenate(parts, axis=0).astype(BF16)


def _s5_params(lam_re, lam_im, log_dt, b_re, b_im, c_re, c_im, d, glu_w, glu_b):
    lr, li = lam_re.astype(F32), lam_im.astype(F32)
    dt = jnp.exp(log_dt.astype(F32))[:, None]
    mag = jnp.exp(lr * dt)
    ar, ai = mag * jnp.cos(li * dt), mag * jnp.sin(li * dt)
    den = lr * lr + li * li
    fr = ((ar - 1.0) * lr + ai * li) / den
    fi = (ai * lr - (ar - 1.0) * li) / den
    br, bi = b_re.astype(F32), b_im.astype(F32)
    bbr = fr[..., None] * br - fi[..., None] * bi
    bbi = fr[..., None] * bi + fi[..., None] * br
    eye = jnp.eye(B_GROUPS, dtype=F32)
    bd = lambda m: jnp.einsum("gpc,gh->gchp", m, eye).reshape(BR_WIDTH, B_LANES)
    cd = lambda m: jnp.einsum("gcp,gh->gphc", m.astype(F32), eye).reshape(B_LANES, BR_WIDTH)
    bdr, bdi = bd(bbr), bd(bbi)
    hi_lo = lambda m: (m.astype(BF16), (m - m.astype(BF16).astype(F32)).astype(BF16))
    bdr_h, bdr_l = hi_lo(bdr)
    bdi_h, bdi_l = hi_lo(bdi)
    nsb = S5_BLOCKS
    diag_b = lambda m: jnp.stack([m[s * (BR_WIDTH // nsb):(s + 1) * (BR_WIDTH // nsb),
                                    s * (B_LANES // nsb):(s + 1) * (B_LANES // nsb)] for s in range(nsb)])
    diag_c = lambda m: jnp.stack([m[s * (B_LANES // nsb):(s + 1) * (B_LANES // nsb),
                                    s * (BR_WIDTH // nsb):(s + 1) * (BR_WIDTH // nsb)] for s in range(nsb)])
    return dict(bdr=bdr_h, bdr_lo=bdr_l, bdi=bdi_h, bdi_lo=bdi_l,
                bdr4=diag_b(bdr_h), bdi4=diag_b(bdi_h), cdr4=diag_c(cd(c_re).astype(BF16)), cdi4=diag_c(cd(c_im).astype(BF16)),
                ar=ar.reshape(1, B_LANES), ai=ai.reshape(1, B_LANES),
                cdr=cd(c_re).astype(BF16), cdi=cd(c_im).astype(BF16), d=d.astype(F32).reshape(1, BR_WIDTH),
                glu_w=glu_w.astype(BF16), glu_b=glu_b.astype(F32).reshape(1, BR_WIDTH))


def _overlap(n_cmp, n_blk, rows, cols):
    start = np.arange(n_cmp)[:, None] * C_CMP_STRIDE
    blk = np.arange(n_blk)[None, :]
    m = (start <= (blk + 1) * C_SLC_BLOCK - 1) & (start + C_CMP_LEN - 1 >= blk * C_SLC_BLOCK)
    out = np.zeros((rows, cols), np.float32)
    out[:n_cmp, :n_blk] = m
    return jnp.asarray(out, BF16)


def _expand(n_keys):
    e = (np.arange(LANES)[:, None] == (np.arange(n_keys)[None, :] // C_SLC_BLOCK)).astype(np.float32)
    return jnp.asarray(e, BF16)


def _tri(lower=False):
    i = np.arange(LANES)
    m = (i[:, None] > i[None, :]) if lower else (i[:, None] < i[None, :])
    return jnp.asarray(m.astype(np.float32), BF16)


def _make_consts(T, sb, past):
    n_cmp_p = (T - C_CMP_LEN) // C_CMP_STRIDE + 1
    n_blk_p = -(-T // C_SLC_BLOCK)
    n_cmp_s = (past + 1 - C_CMP_LEN) // C_CMP_STRIDE + 1
    n_blk_s = -(-(past + 1) // C_SLC_BLOCK)
    return dict(
        tabs_p=_rope_tables(jnp.arange(T)),
        tabs_s=_rope_tables(jnp.full((sb,), past)),
        tri=_tri(),
        tril=_tri(lower=True),
        ov_p=_overlap(n_cmp_p, n_blk_p, T // C_CMP_STRIDE, LANES),
        ex_p=_expand(T),
        ov_s=_overlap(n_cmp_s, n_blk_s, past // C_CMP_STRIDE, -(-n_blk_s // LANES) * LANES),
    )


def _pick_tile(n, cands):
    for c in cands:
        if n % c == 0:
            return c
    return n


def _pad_rows(a, rows):
    return jnp.pad(a[:, None, :], ((0, 0), (0, rows - 1), (0, 0)))


def _layer_weights(l, norm_mix, w_in, a_gq, a_gk, c_gq, c_gk, c_cmp_a, c_cmp_w, d_conv_w, d_conv_b, d_ln_g, d_ln_b,
                   w_br, w_o, norm_mlp, w_up, w_down):
    row = lambda v: v.astype(F32).reshape(1, -1)
    return dict(
        g_mix=row(norm_mix[l]), w1=_pack_w1(jnp.transpose(w_in[l])),
        wg=jnp.transpose(w_in[l])[_O[11]:_O[12]].astype(BF16),
        a_gq=row(a_gq[l]), a_gk=row(a_gk[l]), c_gq=row(c_gq[l]), c_gk=c_gk[l].astype(F32),
        cmp_a=c_cmp_a[l].astype(F32), cmp_w=c_cmp_w[l].astype(BF16),
        conv_w=jnp.pad(d_conv_w[l].astype(F32), ((0, 1), (0, 0))), conv_b=row(d_conv_b[l]),
        ln_g=row(d_ln_g[l]), ln_b=row(d_ln_b[l]),
        w_br=w_br[l].astype(BF16), w_o=w_o[l].astype(BF16), g_mlp=row(norm_mlp[l]),
        w_up=w_up[l].astype(BF16), w_down=w_down[l].astype(BF16))


def _prompt_layer(x, lw, sp, consts):
    bsz, T, _ = x.shape
    n = bsz * T
    x2d = x.reshape(n, D_MODEL)
    tm = _pick_tile(T, (256, 128))
    qb = _pick_tile(T, (256, 128))
    tp = _pick_tile(T, (512, 256, 128))
    nt = T // tp
    bu_spec = pl.BlockSpec((tp, BR_WIDTH), lambda i: (i % nt, i // nt))
    a_t = ((bsz, A_CACHE_DIM, T), pl.BlockSpec((None, A_CACHE_DIM, tp), lambda i: (i // nt, 0, i % nt)))
    (xn, aq, arow, aiq, aiw, bu, cq, crow, wrow, cg, da, arow_t) = _project(
        x2d, consts["tabs_p"], lw, tm=tp, n_pos_blocks=nt, bu_shape=(T, bsz * BR_WIDTH), bu_spec=bu_spec, a_rows_t=a_t)
    arow3 = arow.reshape(bsz, T, A_CACHE_DIM)
    crow3 = crow.reshape(bsz, T, C_CACHE_DIM)
    wrow3 = wrow.reshape(bsz, T, C_WIN_DIM)
    da3 = da.reshape(bsz, T, BR_WIDTH)
    o_a = _dsa(aq, aiq, aiw, arow3, consts["tri"], qb=qb)
    kcmp, vcmp = _compress(crow3, lw["cmp_a"], lw["cmp_w"])
    o_c = _nsa(cq, cg, crow3, wrow3, kcmp, vcmp, consts["ov_p"], consts["ex_p"], qb=qb)
    tc = _pick_tile(T, (128, 64))
    o_b, hr, hi = _s5(bu.reshape(T * bsz, BR_WIDTH), sp, nb=bsz, tc=tc)
    o_d = _conv(da3, lw["conv_w"], lw["conv_b"], lw["ln_g"], lw["ln_b"], tm=tm)
    tmx = _pick_tile(T, (512, 256, 128))
    ntx = T // tmx
    ob_spec = pl.BlockSpec((tmx, BR_WIDTH), lambda r, i: (r % ntx, r // ntx))
    hm = _mix(xn, o_a, o_b.reshape(T, bsz * BR_WIDTH), ob_spec, o_c, o_d, lw["wg"], lw["w_br"], lw["w_o"], tm=tmx)
    y = _mlp(x2d, hm, lw["g_mlp"], lw["w_up"], lw["w_down"], tm=tmx)
    wk = min(C_WINDOW, T)
    return (y.reshape(bsz, T, D_MODEL), jnp.swapaxes(arow_t, 1, 2), crow3, wrow3[:, T - wk:],
            hr.reshape(bsz, B_GROUPS, B_STATE), hi.reshape(bsz, B_GROUPS, B_STATE), da3[:, T - (D_CONV - 1):])


def _sample_layer(x, l, cache_a, cache_c, cache_c_win, h_re, h_im, conv_l, page_table, lw, sp, consts):
    bsz = x.shape[0]
    cache_a_t = jnp.swapaxes(cache_a, 2, 3)
    x2d = x.reshape(bsz, D_MODEL)
    row = lambda w: pl.BlockSpec((bsz, w), lambda i: (0, 0))
    (xn, aq, arow, aiq, aiw, bu, cq, crow, wrow, cg, da) = _project(
        x2d, consts["tabs_s"], lw, tm=bsz, n_pos_blocks=1, bu_shape=(bsz, BR_WIDTH), bu_spec=row(BR_WIDTH))
    q8 = jnp.pad(aq.astype(F32).reshape(bsz, A_HEADS, HEAD_DIM), ((0, 0), (0, SUBLANES - A_HEADS), (0, 0)))
    iq8 = jnp.sum(aiq.astype(F32).reshape(bsz, A_IDX_HEADS, 2, A_IDX_DIM), axis=2)
    iq8 = jnp.pad(iq8, ((0, 0), (0, 0), (LANES - A_IDX_DIM, 0)))
    w8 = jnp.broadcast_to(aiw[:, :A_IDX_HEADS, None], (bsz, A_IDX_HEADS, LANES))
    o_a = _dsa_sample(page_table, cache_a_t, l, q8, iq8, w8, _pad_rows(arow, SUBLANES), consts["tri"], consts["tril"])
    o_a = o_a[:, :A_HEADS].reshape(bsz, BR_WIDTH).astype(BF16)
    cq8 = jnp.pad(cq.astype(F32).reshape(bsz, C_HEADS, HEAD_DIM), ((0, 0), (0, SUBLANES - C_HEADS), (0, 0)))
    g3 = jnp.transpose(cg[:, :3 * C_HEADS].reshape(bsz, C_HEADS, 3), (0, 2, 1))
    g3 = jnp.broadcast_to(jnp.pad(g3, ((0, 0), (0, 0), (0, SUBLANES - C_HEADS)))[..., None], (bsz, 3, SUBLANES, LANES))
    o_c = _nsa_sample(page_table, cache_c, l, cq8, g3, _pad_rows(crow, SUBLANES), cache_c_win,
                      _pad_rows(wrow, SUBLANES), lw["cmp_a"], lw["cmp_w"], consts["ov_s"])
    o_c = o_c[:, :C_HEADS].reshape(bsz, BR_WIDTH).astype(BF16)
    o_b, hr, hi, o_d = _sample_step(bu, h_re.reshape(bsz, B_LANES), h_im.reshape(bsz, B_LANES), sp, da, conv_l,
                                    lw["conv_w"], lw["conv_b"], lw["ln_g"], lw["ln_b"])
    hm = _mix(xn, o_a, o_b, pl.BlockSpec((bsz, BR_WIDTH), lambda r, i: (r, 0)), o_c, o_d,
              lw["wg"], lw["w_br"], lw["w_o"], tm=bsz)
    y = _mlp(x2d, hm, lw["g_mlp"], lw["w_up"], lw["w_down"], tm=bsz)
    new_win = jnp.concatenate([cache_c_win[l][:, 1:], wrow[:, None, :]], axis=1)
    new_conv = jnp.concatenate([conv_l[:, 1:], da[:, None, :]], axis=1)
    return (y.reshape(bsz, 1, D_MODEL), arow[:, None, :], crow[:, None, :], new_win,
            hr.reshape(bsz, B_GROUPS, B_STATE), hi.reshape(bsz, B_GROUPS, B_STATE), new_conv)


def kernel(x_prompt, x_sample, cache_a, cache_c, cache_c_win, state_b_re, state_b_im, state_d_conv, page_table, norm_mix, w_in, a_gq, a_gk, b_lam_re, b_lam_im, b_log_dt, b_b_re, b_b_im, b_c_re, b_c_im, b_d, b_glu_w, b_glu_b, c_gq, c_gk, c_cmp_a, c_cmp_w, d_conv_w, d_conv_b, d_ln_g, d_ln_b, w_br, w_o, norm_mlp, w_up, w_down):
    depth = w_in.shape[0]
    bsz, T, _ = x_prompt.shape
    sb, st, _ = x_sample.shape
    assert st == 1 and bsz == SUBLANES
    assert cache_a.shape[2] == PAGE and cache_c.shape[2] == PAGE
    consts = _make_consts(T, sb, page_table.shape[1] * PAGE)
    xp, xs = x_prompt, x_sample
    order_p = (0, 2, 4, 6, 7, 10)
    order_s = (1, 3, 5, 8, 9, 11)
    outs = [[] for _ in range(12)]
    for l in range(depth):
        lw = _layer_weights(l, norm_mix, w_in, a_gq, a_gk, c_gq, c_gk, c_cmp_a, c_cmp_w, d_conv_w, d_conv_b,
                            d_ln_g, d_ln_b, w_br, w_o, norm_mlp, w_up, w_down)
        sp = _s5_params(b_lam_re[l], b_lam_im[l], b_log_dt[l], b_b_re[l], b_b_im[l], b_c_re[l], b_c_im[l],
                        b_d[l], b_glu_w[l], b_glu_b[l])
        xp, *rp = _prompt_layer(xp, lw, sp, consts)
        xs, *rs = _sample_layer(xs, l, cache_a, cache_c, cache_c_win, state_b_re[l], state_b_im[l], state_d_conv[l],
                                page_table, lw, sp, consts)
        for k in range(6):
            outs[order_p[k]].append(rp[k])
            outs[order_s[k]].append(rs[k])
    return (xp, xs) + tuple(jnp.stack(o) for o in outs)
```

```python
import functools
import math

import numpy as np
import jax
import jax.numpy as jnp
from jax import lax
from jax.experimental import pallas as pl
from jax.experimental.pallas import tpu as pltpu

F32 = jnp.float32
BF16 = jnp.bfloat16
I32 = jnp.int32

D_MODEL = 2048
HEAD_DIM = 128
N_BRANCH = 4
BR_WIDTH = D_MODEL // N_BRANCH
ROPE_THETA = 10000.0
NORM_EPS = 1e-6
A_HEADS = BR_WIDTH // HEAD_DIM
A_IDX_HEADS = 8
A_IDX_DIM = 64
A_TOPK = 256
B_GROUP = 16
B_GROUPS = BR_WIDTH // B_GROUP
B_STATE = 64
B_LANES = B_GROUPS * B_STATE
C_HEADS = BR_WIDTH // HEAD_DIM
C_CMP_STRIDE = 16
C_CMP_LEN = 2 * C_CMP_STRIDE
C_SLC_BLOCK = 64
C_TOPN = 16
C_WINDOW = 512
C_FORCE = 1e4
D_CONV = 31
D_FF = 4 * D_MODEL
A_CACHE_DIM = 2 * HEAD_DIM + A_IDX_DIM
C_CACHE_DIM = 4 * HEAD_DIM
C_WIN_DIM = 2 * HEAD_DIM
PAGE = 128

LANES = 128
SUBLANES = 8
VMEM_LIMIT_MB = 56

_W = (A_HEADS * HEAD_DIM, HEAD_DIM, HEAD_DIM, A_IDX_HEADS * A_IDX_DIM, A_IDX_DIM, A_IDX_HEADS,
      BR_WIDTH, C_HEADS * HEAD_DIM, 6 * HEAD_DIM, 3 * C_HEADS, 2 * BR_WIDTH, N_BRANCH * D_MODEL)
_O = tuple(int(v) for v in np.cumsum((0,) + _W))
_P = {}
_cur = 0
for _name, _w in (("aq", 512), ("ak", 128), ("av", 128), ("aiq", 512), ("aik", 128), ("aiw", 128),
                  ("bu", 512), ("cq", 512), ("ckv", 768), ("cg", 128), ("dglu", 1024)):
    _P[_name] = (_cur, _w)
    _cur += _w
P_TOTAL = _cur

NEG = -1e30
ATT_SCALE = HEAD_DIM ** -0.5
SIGN = -2 ** 31


def _cparams(n_axes):
    return pltpu.CompilerParams(dimension_semantics=("arbitrary",) * n_axes,
                                vmem_limit_bytes=VMEM_LIMIT_MB * 1024 * 1024)


def _full(a, n_grid):
    nd = a.ndim
    return pl.BlockSpec(a.shape, lambda *_: (0,) * nd)


def _dot(a, b):
    return jnp.dot(a, b, preferred_element_type=F32)


def _dot_nt(a, b):
    return lax.dot_general(a, b, (((1,), (1,)), ((), ())), preferred_element_type=F32)


def _dot_split3(p, m):
    hi = p.astype(BF16)
    r = p - hi.astype(F32)
    mid = r.astype(BF16)
    lo = (r - mid.astype(F32)).astype(BF16)
    return _dot(hi, m) + _dot(mid, m) + _dot(lo, m)


def _rms(x, g):
    return x * lax.rsqrt(jnp.mean(x * x, axis=-1, keepdims=True) + NORM_EPS) * g


def _sigmoid(x):
    return 1.0 / (1.0 + jnp.exp(-x))


def _gelu(x):
    return x * (0.5 * (1.0 + jnp.tanh(math.sqrt(2.0 / math.pi) * (x + 0.044715 * (x * x * x)))))


def _sortable(x):
    b = pltpu.bitcast(x + 0.0, I32)
    return jnp.where(b < 0, b ^ jnp.int32(0x7FFFFFFF), b)


def _kth_key(key_ref, k, red_axes):
    shp = tuple(1 if a in red_axes else s for a, s in enumerate(key_ref.shape))

    def count(mask):
        c = jnp.where(mask, 1.0, 0.0)
        for a in sorted(red_axes):
            c = jnp.sum(c, axis=a, keepdims=True)
        return c

    def body(it, tu):
        cand_u = tu | jnp.left_shift(jnp.int32(1), 31 - it)
        cand_s = cand_u ^ jnp.int32(SIGN)
        return jnp.where(count(key_ref[...] >= cand_s) >= k, cand_u, tu)

    tu = lax.fori_loop(0, 32, body, jnp.zeros(shp, I32), unroll=4)
    return tu ^ jnp.int32(SIGN), count


def _select_rows(key_ref, k, tri):
    ts, count = _kth_key(key_ref, k, (1,))
    keys = key_ref[...]
    gt = keys > ts
    need = k - count(gt)
    eqf = jnp.where(keys == ts, 1.0, 0.0)
    base = jnp.zeros_like(need)
    pieces = []
    for c in range(keys.shape[1] // LANES):
        ch = eqf[:, c * LANES:(c + 1) * LANES]
        pref = _dot(ch.astype(BF16), tri) + base
        pieces.append(jnp.where(pref < need, ch, 0.0))
        base = base + jnp.sum(ch, axis=-1, keepdims=True)
    sel_eq = pieces[0] if len(pieces) == 1 else jnp.concatenate(pieces, axis=1)
    return jnp.where(gt, 1.0, sel_eq)


def _select_packed(key_ref, k, tri, tril):
    shp = key_ref.shape
    ts, count = _kth_key(key_ref, k, (0, 1, 2))
    keys = key_ref[...]
    gt = keys > ts
    need = k - count(gt)
    eqf = jnp.where(keys == ts, 1.0, 0.0)
    eq2 = eqf.reshape(LANES, LANES)
    within = _dot(eq2.astype(BF16), tri)
    tot = jnp.broadcast_to(jnp.sum(eq2, axis=1, keepdims=True), (LANES, LANES))
    base = _dot(tril, tot.astype(BF16))
    sel_eq = jnp.where((within + base).reshape(shp) < need, eqf, 0.0)
    return jnp.where(gt, 1.0, sel_eq)


def _select_rank(sc, n, k):
    lane = lax.broadcasted_iota(I32, sc.shape, 1)
    rank = jnp.zeros(sc.shape, F32)
    for i in range(n):
        col = sc[:, i:i + 1]
        rank = rank + jnp.where(lane > i, jnp.where(col >= sc, 1.0, 0.0), jnp.where(col > sc, 1.0, 0.0))
    return jnp.where(rank < k, 1.0, 0.0)


LOG2E = 1.4426950408889634


def _masked_attn(qh, k, v, maskf):
    s = jnp.where(maskf > 0.5, _dot_nt(qh, k) * (ATT_SCALE * LOG2E), NEG)
    e = jnp.exp2(s - jnp.max(s, axis=-1, keepdims=True)) * maskf
    den = jnp.maximum(jnp.sum(e, axis=-1, keepdims=True), 1e-30)
    return _dot(e.astype(BF16), v) / den


def _proj_kernel(x_ref, g_ref, w_ref, c128_ref, s128_ref, c64_ref, s64_ref, gqa_ref, gka_ref, gqc_ref, gkc_ref,
                 xn_ref, aq_ref, arow_ref, aiq_ref, aiw_ref, bu_ref, cq_ref, crow_ref, wrow_ref, cg_ref, da_ref,
                 arow_t_ref=None):
    xn = _rms(x_ref[...], g_ref[...]).astype(BF16)
    xn_ref[...] = xn
    cos, sin = c128_ref[...], s128_ref[...]
    cos64, sin64 = c64_ref[...], s64_ref[...]
    lane = lax.broadcasted_iota(I32, cos.shape, 1)
    lo32 = (lane & 63) < 32
    lo64 = lane < 64

    def seg(name):
        a, w = _P[name]
        return _dot_nt(xn, w_ref[a:a + w, :])

    def rope128(v):
        return v * cos + pltpu.roll(v, 64, 1) * sin

    def rope64(v):
        rot = jnp.where(lo32, pltpu.roll(v, 96, 1), pltpu.roll(v, 32, 1))
        return v * cos64 + rot * sin64

    z = seg("aq")
    for h in range(A_HEADS):
        sl = slice(h * HEAD_DIM, (h + 1) * HEAD_DIM)
        aq_ref[:, sl] = rope128(_rms(z[:, sl], gqa_ref[...])).astype(BF16)
    ak = rope128(_rms(seg("ak"), gka_ref[...]))
    av = seg("av")
    aik = rope64(seg("aik"))
    arow_ref[:, 0:128] = ak
    arow_ref[:, 128:256] = av
    arow_ref[:, 256:320] = aik[:, 0:A_IDX_DIM]
    if arow_t_ref is not None:
        arow_t_ref[0:128, :] = ak.T
        arow_t_ref[128:256, :] = av.T
        arow_t_ref[256:320, :] = aik.T[0:A_IDX_DIM, :]
    z = seg("aiq")
    for j in range(A_IDX_HEADS // 2):
        r = rope64(z[:, j * LANES:(j + 1) * LANES])
        aiq_ref[:, (2 * j) * LANES:(2 * j + 1) * LANES] = jnp.where(lo64, r, 0.0).astype(BF16)
        aiq_ref[:, (2 * j + 1) * LANES:(2 * j + 2) * LANES] = jnp.where(lo64, 0.0, r).astype(BF16)
    aiw_ref[...] = seg("aiw") * (A_IDX_HEADS ** -0.5) * (A_IDX_DIM ** -0.5)
    bu_ref[...] = seg("bu")
    z = seg("cq")
    for h in range(C_HEADS):
        sl = slice(h * HEAD_DIM, (h + 1) * HEAD_DIM)
        cq_ref[:, sl] = rope128(_rms(z[:, sl], gqc_ref[...])).astype(BF16)
    z = seg("ckv")
    for br in range(3):
        kk = rope128(_rms(z[:, (2 * br) * LANES:(2 * br + 1) * LANES], gkc_ref[br:br + 1, :]))
        vv = z[:, (2 * br + 1) * LANES:(2 * br + 2) * LANES]
        if br < 2:
            crow_ref[:, (2 * br) * LANES:(2 * br + 1) * LANES] = kk
            crow_ref[:, (2 * br + 1) * LANES:(2 * br + 2) * LANES] = vv
        else:
            wrow_ref[:, 0:LANES] = kk
            wrow_ref[:, LANES:2 * LANES] = vv
    cg_ref[...] = _sigmoid(seg("cg"))
    z = seg("dglu")
    da_ref[...] = z[:, 0:BR_WIDTH] * _sigmoid(z[:, BR_WIDTH:2 * BR_WIDTH])


def _project(x2d, tabs, lw, *, tm, n_pos_blocks, bu_shape, bu_spec, a_rows_t=None):
    n = x2d.shape[0]
    row = lambda w: pl.BlockSpec((tm, w), lambda i: (i, 0))
    tab = pl.BlockSpec((tm, LANES), lambda i: (i % n_pos_blocks, 0))
    ins = [x2d, lw["g_mix"], lw["w1"], tabs[0], tabs[1], tabs[2], tabs[3], lw["a_gq"], lw["a_gk"], lw["c_gq"], lw["c_gk"]]
    w_spec = pl.BlockSpec(ins[2].shape, lambda i: (0, 0), pipeline_mode=pl.Buffered(1))
    in_specs = [row(D_MODEL), _full(ins[1], 1), w_spec, tab, tab, tab, tab] + [_full(a, 1) for a in ins[7:]]
    outs = [((n, D_MODEL), BF16, row(D_MODEL)),
            ((n, 512), BF16, row(512)),
            ((n, A_CACHE_DIM), F32, row(A_CACHE_DIM)),
            ((n, 1024), BF16, row(1024)),
            ((n, LANES), F32, row(LANES)),
            (bu_shape, F32, bu_spec),
            ((n, 512), BF16, row(512)),
            ((n, C_CACHE_DIM), F32, row(C_CACHE_DIM)),
            ((n, C_WIN_DIM), F32, row(C_WIN_DIM)),
            ((n, LANES), F32, row(LANES)),
            ((n, BR_WIDTH), F32, row(BR_WIDTH))]
    if a_rows_t is not None:
        outs.append((a_rows_t[0], F32, a_rows_t[1]))
    return pl.pallas_call(
        _proj_kernel,
        grid=(n // tm,),
        in_specs=in_specs,
        out_specs=[o[2] for o in outs],
        out_shape=[jax.ShapeDtypeStruct(o[0], o[1]) for o in outs],
        compiler_params=_cparams(1),
        name="project",
    )(*ins)


def _dsa_kernel(aq_ref, aiq_ref, aiw_ref, arow_ref, tri_ref, *rest, qb, L, n_sel, q_first, q_count, aliased):
    o_ref, k_sc, v_sc, ik_sc, key_sc = rest[1:] if aliased else rest
    qi = pl.program_id(1) + q_first

    @pl.when(pl.program_id(1) == 0)
    def _():
        k_sc[...] = arow_ref[:, 0:128].astype(BF16)
        v_sc[...] = arow_ref[:, 128:256].astype(BF16)
        ik = arow_ref[:, 256:320]
        ik_sc[...] = jnp.concatenate([ik, ik], axis=1).astype(BF16)

    def body(le):
        qpos = qi * qb + lax.broadcasted_iota(I32, (qb, 1), 0)
        valid = lax.broadcasted_iota(I32, (qb, le), 1) <= qpos
        if le <= n_sel:
            mask = jnp.where(valid, 1.0, 0.0)
        else:
            w = aiw_ref[...]
            score = None
            for h in range(A_IDX_HEADS):
                lg = _dot_nt(aiq_ref[:, h * LANES:(h + 1) * LANES], ik_sc[0:le, :])
                t = jnp.maximum(lg, 0.0) * w[:, h:h + 1]
                score = t if score is None else score + t
            keys = key_sc.at[:, 0:le]
            keys[...] = _sortable(jnp.where(valid, score, -jnp.inf))
            mask = jnp.where(valid, _select_rows(keys, n_sel, tri_ref[...]), 0.0)
        q = aq_ref[...]
        for h in range(A_HEADS):
            sl = slice(h * HEAD_DIM, (h + 1) * HEAD_DIM)
            o_ref[:, sl] = _masked_attn(q[:, sl], k_sc[0:le, :], v_sc[0:le, :], mask).astype(BF16)

    _causal_branches(qi, qb, L, body, q_range=(q_first, q_first + q_count))


def _causal_branches(qi, qb, L, body, q_range=None):
    step = max(qb, L // 8)
    if L % step:
        step = qb
    per = step // qb
    for j in range(L // step):
        if q_range is not None and ((j + 1) * per <= q_range[0] or j * per >= q_range[1]):
            continue

        @pl.when((qi >= j * per) & (qi < (j + 1) * per))
        def _(j=j):
            body((j + 1) * step)


def _dsa(aq, aiq, aiw, arow3, tri, *, qb):
    bsz, L, _ = arow3.shape
    nq = L // qb
    bounds = sorted({0, nq // 2, (3 * nq) // 4, nq})
    n_sel = min(A_TOPK, L // 4)
    out = None
    for q0, q1 in zip(bounds[:-1], bounds[1:]):
        npq = q1 - q0
        row = lambda w, q0=q0: pl.BlockSpec((qb, w), lambda b, i: (b * nq + q0 + i, 0))
        ins = [aq, aiq, aiw, arow3, tri]
        in_specs = [row(512), row(1024), row(LANES),
                    pl.BlockSpec((None, L, A_CACHE_DIM), lambda b, i: (b, 0, 0)), _full(tri, 2)]
        aliases = {}
        if out is not None:
            ins.append(out)
            in_specs.append(pl.BlockSpec(memory_space=pl.ANY))
            aliases = {len(ins) - 1: 0}
        out = pl.pallas_call(
            functools.partial(_dsa_kernel, qb=qb, L=L, n_sel=n_sel, q_first=q0, q_count=npq,
                              aliased=out is not None),
            grid=(bsz, npq),
            in_specs=in_specs,
            out_specs=row(512),
            out_shape=jax.ShapeDtypeStruct((bsz * L, 512), BF16),
            scratch_shapes=[pltpu.VMEM((L, 128), BF16), pltpu.VMEM((L, 128), BF16), pltpu.VMEM((L, 128), BF16),
                            pltpu.VMEM((qb, L), I32)],
            input_output_aliases=aliases,
            compiler_params=_cparams(2),
            name="dsa",
        )(*ins)
    return out


def _summaries(x, a, w, n_cmp):
    ns = x.shape[0] // C_CMP_STRIDE
    x3 = x.reshape(ns, C_CMP_STRIDE, HEAD_DIM)
    lo = jnp.sum(x3 * a[0:C_CMP_STRIDE][None], axis=1)
    hi = jnp.sum(x3 * a[C_CMP_STRIDE:C_CMP_LEN][None], axis=1)
    return lo, hi


def _finish_summaries(lo, hi, w, n_cmp):
    ns = lo.shape[0]
    comb = lo + pltpu.roll(hi, ns - 1, 0)
    comb = jnp.where(lax.broadcasted_iota(I32, comb.shape, 0) < n_cmp, comb, 0.0)
    return _dot(comb.astype(BF16), w).astype(BF16)


def _cmp_kernel(x_ref, a_ref, w_ref, kc_ref, vc_ref, *, n_cmp):
    for t, out in ((0, kc_ref), (1, vc_ref)):
        lo, hi = _summaries(x_ref[:, t * LANES:(t + 1) * LANES], a_ref[t], w_ref[t], n_cmp)
        out[...] = _finish_summaries(lo, hi, w_ref[t], n_cmp)


def _compress(crow3, cmp_a, cmp_w):
    bsz, L, _ = crow3.shape
    ns = L // C_CMP_STRIDE
    n_cmp = (L - C_CMP_LEN) // C_CMP_STRIDE + 1
    out = pl.BlockSpec((None, ns, HEAD_DIM), lambda b: (b, 0, 0))
    return pl.pallas_call(
        functools.partial(_cmp_kernel, n_cmp=n_cmp),
        grid=(bsz,),
        in_specs=[pl.BlockSpec((None, L, C_CACHE_DIM), lambda b: (b, 0, 0)), _full(cmp_a, 1), _full(cmp_w, 1)],
        out_specs=[out, out],
        out_shape=[jax.ShapeDtypeStruct((bsz, ns, HEAD_DIM), BF16)] * 2,
        compiler_params=_cparams(1),
        name="compress",
    )(crow3, cmp_a, cmp_w)


def _block_scores(imp, qpos):
    j = lax.broadcasted_iota(I32, imp.shape, 1)
    cur = lax.shift_right_logical(qpos, 6)
    forced = (j == 0) | (j == cur) | (j == cur - 1)
    return jnp.where(j <= cur, jnp.where(forced, C_FORCE, imp), -jnp.inf)


def _cmp_softmax(qh, kc, cvalid):
    s = _dot_nt(qh, kc) * ATT_SCALE
    s = jnp.where(cvalid, s, NEG)
    m = jnp.max(s, axis=-1, keepdims=True)
    e = jnp.where(cvalid, jnp.exp(s - m), 0.0)
    return e / jnp.maximum(jnp.sum(e, axis=-1, keepdims=True), 1e-30)


def _nsa_kernel(cq_ref, cg_ref, crow_ref, wrow_ref, kc_ref, vc_ref, ov_ref, ex_ref, o_ref,
                ks_sc, vs_sc, kw_sc, vw_sc, os_sc, *, qb, L, n_cmp, n_blk, n_top, wsl):
    qi = pl.program_id(1)

    @pl.when(qi == 0)
    def _():
        ks_sc[...] = crow_ref[:, 256:384].astype(BF16)
        vs_sc[...] = crow_ref[:, 384:512].astype(BF16)
        kw_sc[...] = wrow_ref[:, 0:128].astype(BF16)
        vw_sc[...] = wrow_ref[:, 128:256].astype(BF16)

    q0 = qi * qb
    qpos = q0 + lax.broadcasted_iota(I32, (qb, 1), 0)
    q = cq_ref[...]
    g = cg_ref[...]
    kc, vc = kc_ref[...], vc_ref[...]
    ncp = kc.shape[0]
    n_io = lax.broadcasted_iota(I32, (qb, ncp), 1)
    cvalid = (n_io < n_cmp) & (n_io * C_CMP_STRIDE + (C_CMP_LEN - 1) <= qpos)
    o_c, psum = [], None
    for h in range(C_HEADS):
        p = _cmp_softmax(q[:, h * HEAD_DIM:(h + 1) * HEAD_DIM], kc, cvalid)
        o_c.append(_dot(p.astype(BF16), vc))
        psum = p if psum is None else psum + p
    imp = _dot_split3(psum, ov_ref[...])
    sel = _select_rank(_block_scores(imp, qpos), n_blk, n_top).astype(BF16)

    def selected(le):
        selk = _dot(sel, ex_ref[:, 0:le])
        smask = jnp.where(lax.broadcasted_iota(I32, (qb, le), 1) <= qpos, selk, 0.0)
        for h in range(C_HEADS):
            sl = slice(h * HEAD_DIM, (h + 1) * HEAD_DIM)
            os_sc[:, sl] = _masked_attn(q[:, sl], ks_sc[0:le, :], vs_sc[0:le, :], smask)

    _causal_branches(qi, qb, L, selected)
    start = pl.multiple_of(jnp.minimum(jnp.maximum(q0 - C_WINDOW, 0), L - wsl), qb)
    dist = qpos - (start + lax.broadcasted_iota(I32, (qb, wsl), 1))
    wmask = jnp.where((dist >= 0) & (dist <= C_WINDOW), 1.0, 0.0)
    kw = kw_sc[pl.ds(start, wsl), :]
    vw = vw_sc[pl.ds(start, wsl), :]
    for h in range(C_HEADS):
        sl = slice(h * HEAD_DIM, (h + 1) * HEAD_DIM)
        o_w = _masked_attn(q[:, sl], kw, vw, wmask)
        out = g[:, 3 * h:3 * h + 1] * o_c[h] + g[:, 3 * h + 1:3 * h + 2] * os_sc[:, sl] + g[:, 3 * h + 2:3 * h + 3] * o_w
        o_ref[:, sl] = out.astype(BF16)


def _nsa(cq, cg, crow3, wrow3, kcmp, vcmp, ov, ex, *, qb):
    bsz, L, _ = crow3.shape
    nq = L // qb
    ns = kcmp.shape[1]
    n_cmp = (L - C_CMP_LEN) // C_CMP_STRIDE + 1
    n_blk = -(-L // C_SLC_BLOCK)
    wsl = min(L, C_WINDOW + qb)
    row = lambda w: pl.BlockSpec((qb, w), lambda b, i: (b * nq + i, 0))
    per_b = lambda r, w: pl.BlockSpec((None, r, w), lambda b, i: (b, 0, 0))
    return pl.pallas_call(
        functools.partial(_nsa_kernel, qb=qb, L=L, n_cmp=n_cmp, n_blk=n_blk, n_top=min(C_TOPN, n_blk), wsl=wsl),
        grid=(bsz, nq),
        in_specs=[row(512), row(LANES), per_b(L, C_CACHE_DIM), per_b(L, C_WIN_DIM),
                  per_b(ns, HEAD_DIM), per_b(ns, HEAD_DIM), _full(ov, 2), _full(ex, 2)],
        out_specs=row(512),
        out_shape=jax.ShapeDtypeStruct((bsz * L, 512), BF16),
        scratch_shapes=[pltpu.VMEM((L, 128), BF16)] * 4 + [pltpu.VMEM((qb, 512), F32)],
        compiler_params=_cparams(2),
        name="nsa",
    )(cq, cg, crow3, wrow3, kcmp, vcmp, ov, ex)


S5_BLOCKS = 4


def _s5_kernel(u_ref, bdr_ref, bdi_ref, ar_ref, ai_ref, cdr_ref, cdi_ref, d_ref, gw_ref, gb_ref,
               o_ref, hr_ref, hi_ref, xr_sc, xi_sc, h_sc, *, tc, nb):
    i = pl.program_id(0)

    @pl.when(i == 0)
    def _():
        h_sc[...] = jnp.zeros_like(h_sc)

    u = u_ref[...]
    ub = u.astype(BF16)
    nsb = bdr_ref.shape[0]
    wu, wx = BR_WIDTH // nsb, B_LANES // nsb
    for sb in range(nsb):
        us = ub[:, sb * wu:(sb + 1) * wu]
        xr_sc[:, sb * wx:(sb + 1) * wx] = _dot(us, bdr_ref[sb])
        xi_sc[:, sb * wx:(sb + 1) * wx] = _dot(us, bdi_ref[sb])
    ar = jnp.broadcast_to(ar_ref[...], (nb, B_LANES))
    ai = jnp.broadcast_to(ai_ref[...], (nb, B_LANES))

    def step(t, carry):
        hr, hi = carry
        r0 = pl.multiple_of(t * nb, nb)
        nhr = ar * hr - ai * hi + xr_sc[pl.ds(r0, nb), :]
        nhi = ar * hi + ai * hr + xi_sc[pl.ds(r0, nb), :]
        xr_sc[pl.ds(r0, nb), :] = nhr
        xi_sc[pl.ds(r0, nb), :] = nhi
        return nhr, nhi

    hr, hi = lax.fori_loop(0, tc, step, (h_sc[0], h_sc[1]))
    h_sc[0] = hr
    h_sc[1] = hi
    hr_ref[...] = hr
    hi_ref[...] = hi
    ch = [_dot(xr_sc[:, sb * wx:(sb + 1) * wx].astype(BF16), cdr_ref[sb])
          - _dot(xi_sc[:, sb * wx:(sb + 1) * wx].astype(BF16), cdi_ref[sb]) for sb in range(nsb)]
    y = jnp.concatenate(ch, axis=1) + d_ref[...] * u
    gl = _gelu(y)
    o_ref[...] = (gl * _sigmoid(_dot(gl.astype(BF16), gw_ref[...]) + gb_ref[...])).astype(BF16)


def _s5(u_tm, sp, *, nb, tc):
    rows = u_tm.shape[0]
    r = tc * nb
    consts = [sp["bdr4"], sp["bdi4"], sp["ar"], sp["ai"], sp["cdr4"], sp["cdi4"], sp["d"], sp["glu_w"], sp["glu_b"]]
    st = pl.BlockSpec((nb, B_LANES), lambda i: (0, 0))
    return pl.pallas_call(
        functools.partial(_s5_kernel, tc=tc, nb=nb),
        grid=(rows // r,),
        in_specs=[pl.BlockSpec((r, BR_WIDTH), lambda i: (i, 0))] + [_full(c, 1) for c in consts],
        out_specs=[pl.BlockSpec((r, BR_WIDTH), lambda i: (i, 0)), st, st],
        out_shape=[jax.ShapeDtypeStruct((rows, BR_WIDTH), BF16),
                   jax.ShapeDtypeStruct((nb, B_LANES), F32), jax.ShapeDtypeStruct((nb, B_LANES), F32)],
        scratch_shapes=[pltpu.VMEM((r, B_LANES), F32), pltpu.VMEM((r, B_LANES), F32), pltpu.VMEM((2, nb, B_LANES), F32)],
        compiler_params=_cparams(1),
        name="s5",
    )(u_tm, *consts)


HALO = 32


def _ln_swish(y, g, b):
    yc = y - jnp.mean(y, axis=-1, keepdims=True)
    yn = yc * lax.rsqrt(jnp.mean(yc * yc, axis=-1, keepdims=True) + NORM_EPS) * g + b
    return yn * _sigmoid(yn)


def _conv_kernel(cur_ref, halo_ref, w_ref, b_ref, lg_ref, lb_ref, o_ref, ext_sc, *, tm):
    i = pl.program_id(1)
    ext_sc[0, 0:HALO, :] = jnp.where(i == 0, 0.0, halo_ref[...])
    ext_sc[0, HALO:HALO + tm, :] = cur_ref[...]
    n = HALO + tm - SUBLANES
    for k in range(1, SUBLANES):
        ext_sc[k, 0:n, :] = ext_sc[0, pl.ds(k, n), :]
    acc = jnp.zeros((tm, BR_WIDTH), F32)
    for j in range(D_CONV):
        off = HALO - (D_CONV - 1) + j
        k = off % SUBLANES
        acc = acc + w_ref[j:j + 1, :] * ext_sc[k, off - k:off - k + tm, :]
    o_ref[...] = _ln_swish(acc + b_ref[...], lg_ref[...], lb_ref[...]).astype(BF16)


def _conv(da3, cw, cb, lg, lb, *, tm):
    bsz, T, _ = da3.shape
    nt = T // tm
    hb = tm // HALO
    consts = [cw, cb, lg, lb]
    return pl.pallas_call(
        functools.partial(_conv_kernel, tm=tm),
        grid=(bsz, nt),
        in_specs=[pl.BlockSpec((None, tm, BR_WIDTH), lambda b, i: (b, i, 0)),
                  pl.BlockSpec((None, HALO, BR_WIDTH), lambda b, i: (b, jnp.maximum(i * hb - 1, 0), 0))]
        + [_full(c, 2) for c in consts],
        out_specs=pl.BlockSpec((tm, BR_WIDTH), lambda b, i: (b * nt + i, 0)),
        out_shape=jax.ShapeDtypeStruct((bsz * T, BR_WIDTH), BF16),
        scratch_shapes=[pltpu.VMEM((SUBLANES, HALO + tm, BR_WIDTH), F32)],
        compiler_params=_cparams(2),
        name="conv",
    )(da3, da3, *consts)


MIX_CW = 512


def _mix_kernel(xn_ref, oa_ref, ob_ref, oc_ref, od_ref, wg_ref, wbr_ref, wo_ref, hm_ref, acc_sc):
    i = pl.program_id(1)

    @pl.when(i == 0)
    def _():
        acc_sc[...] = jnp.zeros_like(acc_sc)

    xn = xn_ref[...]
    br = jnp.where(i == 0, oa_ref[...], jnp.where(i == 1, ob_ref[...], jnp.where(i == 2, oc_ref[...], od_ref[...])))
    for c in range(D_MODEL // MIX_CW):
        sl = slice(c * MIX_CW, (c + 1) * MIX_CW)
        contrib = _sigmoid(_dot_nt(xn, wg_ref[sl, :])) * _dot(br, wbr_ref[:, sl])
        acc_sc[:, sl] = acc_sc[:, sl] + contrib

    @pl.when(i == N_BRANCH - 1)
    def _():
        hm_ref[...] = _dot(acc_sc[...].astype(BF16), wo_ref[...])


def _mix(xn, oa, ob, ob_spec, oc, od, wg, wbr, wo, *, tm):
    n = xn.shape[0]
    row = lambda w: pl.BlockSpec((tm, w), lambda r, i: (r, 0))
    return pl.pallas_call(
        _mix_kernel,
        grid=(n // tm, N_BRANCH),
        in_specs=[row(D_MODEL), row(BR_WIDTH), ob_spec, row(BR_WIDTH), row(BR_WIDTH),
                  pl.BlockSpec((D_MODEL, D_MODEL), lambda r, i: (i, 0)),
                  pl.BlockSpec((None, BR_WIDTH, D_MODEL), lambda r, i: (i, 0, 0)),
                  pl.BlockSpec((D_MODEL, D_MODEL), lambda r, i: (0, 0))],
        out_specs=row(D_MODEL),
        out_shape=jax.ShapeDtypeStruct((n, D_MODEL), F32),
        scratch_shapes=[pltpu.VMEM((tm, D_MODEL), F32)],
        compiler_params=_cparams(2),
        name="mix",
    )(xn, oa, ob, oc, od, wg, wbr, wo)


MLP_FC = 1024


def _mlp_kernel(x_ref, hm_ref, g_ref, wu_ref, wd_ref, y_ref, hn_sc, acc_sc):
    j = pl.program_id(1)

    @pl.when(j == 0)
    def _():
        hn_sc[...] = _rms(x_ref[...] + hm_ref[...], g_ref[...]).astype(BF16)
        acc_sc[...] = jnp.zeros_like(acc_sc)

    up = _dot(hn_sc[...], wu_ref[...])
    act = jnp.square(jnp.maximum(up, 0.0)).astype(BF16)
    d = _dot(act, wd_ref[...])
    acc_sc[...] = acc_sc[...] + d

    @pl.when(j == pl.num_programs(1) - 1)
    def _():
        y_ref[...] = (x_ref[...] + hm_ref[...]) + acc_sc[...]


def _mlp(x2d, hm, g, wu, wd, *, tm):
    n = x2d.shape[0]
    row = pl.BlockSpec((tm, D_MODEL), lambda r, j: (r, 0))
    return pl.pallas_call(
        _mlp_kernel,
        grid=(n // tm, D_FF // MLP_FC),
        in_specs=[row, row, _full(g, 2),
                  pl.BlockSpec((D_MODEL, MLP_FC), lambda r, j: (0, j)),
                  pl.BlockSpec((MLP_FC, D_MODEL), lambda r, j: (j, 0))],
        out_specs=row,
        out_shape=jax.ShapeDtypeStruct((n, D_MODEL), F32),
        scratch_shapes=[pltpu.VMEM((tm, D_MODEL), BF16), pltpu.VMEM((tm, D_MODEL), F32)],
        compiler_params=_cparams(2),
        name="mlp",
    )(x2d, hm, g, wu, wd)


PAGES_PER_STEP = 32


def _tile_attention(s_ref, bias_ref, v_ref, n, s_new, bias_new, v_new, v_transposed):
    c2 = ATT_SCALE * LOG2E
    s = s_ref[...] * c2 + bias_ref[...]
    sn = s_new * c2 + bias_new
    m = jnp.maximum(jnp.max(jnp.max(s, axis=0), axis=1, keepdims=True), sn)
    e = jnp.exp2(s - m)
    en = jnp.exp2(sn - m)
    den = jnp.sum(jnp.sum(e, axis=0), axis=1, keepdims=True) + en
    acc = en.astype(BF16).astype(F32) * v_new.astype(BF16).astype(F32)
    mm = _dot_nt if v_transposed else _dot
    for c in range(n):
        acc = acc + mm(e[c].astype(BF16), v_ref[c * LANES:(c + 1) * LANES, :])
    return acc / den


KEY_TILES = 16


def _dsa_s_kernel(pt_ref, q_ref, iq_ref, w_ref, new_ref, *rest, pg, n_pages, n_sel):
    pages = rest[:pg]
    tri_ref, tril_ref, o_ref, s_sc, bias_sc, vt_sc, sc_sc, key_sc = rest[pg:]
    g = pl.program_id(1)
    q = q_ref[...].astype(BF16)
    iq = iq_ref[...].astype(BF16)
    w = w_ref[...]

    @pl.when(g == 0)
    def _():
        sc_sc[...] = jnp.full(sc_sc.shape, -jnp.inf, F32)

    for i in range(pg):
        c = g * pg + i
        page = pages[i]
        s_sc[c] = _dot(q, page[0:128, :].astype(BF16))
        vt_sc[pl.ds(pl.multiple_of(c * LANES, LANES), LANES), :] = page[128:256, :].astype(BF16)
        lg = _dot(iq, page[192:320, :].astype(BF16))
        sc_sc[g * (pg // SUBLANES) + i // SUBLANES, i % SUBLANES:i % SUBLANES + 1, :] = jnp.sum(
            jnp.maximum(lg, 0.0) * w, axis=0, keepdims=True)

    @pl.when(g == pl.num_programs(1) - 1)
    def _():
        new = new_ref[...]
        k_new = new[0:1, 0:128].astype(BF16).astype(F32)
        s_new = jnp.sum(q.astype(F32) * k_new, axis=-1, keepdims=True)
        ik_new = new[0:1, 256:320].astype(BF16).astype(F32)
        lg_new = jnp.sum(iq[:, A_IDX_DIM:2 * A_IDX_DIM].astype(F32) * ik_new, axis=-1, keepdims=True)
        sc_new = jnp.sum(jnp.maximum(lg_new, 0.0) * w[:, 0:1], axis=0, keepdims=True)
        t_new = n_pages // SUBLANES
        first = (lax.broadcasted_iota(I32, (SUBLANES, LANES), 0) == 0) & (lax.broadcasted_iota(I32, (SUBLANES, LANES), 1) == 0)
        sc_sc[t_new] = jnp.where(first, sc_new, -jnp.inf)
        key_sc[...] = _sortable(sc_sc[...])
        sel = _select_packed(key_sc, n_sel, tri_ref[...], tril_ref[...])
        for c in range(n_pages):
            row = sel[c // SUBLANES][c % SUBLANES:c % SUBLANES + 1, :]
            bias_sc[c] = jnp.where(jnp.broadcast_to(row, (SUBLANES, LANES)) > 0.5, 0.0, NEG)
        bias_new = jnp.where(sel[t_new][0:1, 0:1] > 0.5, 0.0, NEG)
        o_ref[...] = _tile_attention(s_sc, bias_sc, vt_sc, n_pages, s_new, bias_new, new[0:1, 128:256], True)


def _dsa_sample(page_table, cache_t, layer, q8, iq8, w8, new8, tri, tril):
    bsz, n_pages = page_table.shape
    pg = PAGES_PER_STEP
    assert pg % SUBLANES == 0 and n_pages % pg == 0 and n_pages * PAGE + 1 <= KEY_TILES * SUBLANES * LANES
    n_sel = min(A_TOPK, (n_pages * PAGE + 1) // 4)
    per_b = lambda r, w: pl.BlockSpec((None, r, w), lambda b, g, pt: (b, 0, 0))
    cst = lambda a: pl.BlockSpec(a.shape, lambda b, g, pt: (0,) * a.ndim)
    page_spec = lambda i: pl.BlockSpec((None, None, A_CACHE_DIM, PAGE),
                                       lambda b, g, pt: (layer, pt[b, g * pg + i], 0, 0))
    grid_spec = pltpu.PrefetchScalarGridSpec(
        num_scalar_prefetch=1,
        grid=(bsz, n_pages // pg),
        in_specs=[per_b(SUBLANES, LANES), per_b(SUBLANES, LANES), per_b(SUBLANES, LANES), per_b(SUBLANES, A_CACHE_DIM)]
        + [page_spec(i) for i in range(pg)] + [cst(tri), cst(tril)],
        out_specs=per_b(SUBLANES, HEAD_DIM),
        scratch_shapes=[pltpu.VMEM((n_pages, SUBLANES, LANES), F32), pltpu.VMEM((n_pages, SUBLANES, LANES), F32),
                        pltpu.VMEM((n_pages * LANES, PAGE), BF16),
                        pltpu.VMEM((KEY_TILES, SUBLANES, LANES), F32), pltpu.VMEM((KEY_TILES, SUBLANES, LANES), I32)],
    )
    return pl.pallas_call(
        functools.partial(_dsa_s_kernel, pg=pg, n_pages=n_pages, n_sel=n_sel),
        grid_spec=grid_spec,
        out_shape=jax.ShapeDtypeStruct((bsz, SUBLANES, HEAD_DIM), F32),
        compiler_params=_cparams(2),
        name="dsa_sample",
    )(page_table, q8, iq8, w8, new8, *([cache_t] * pg), tri, tril)


def _nsa_s_kernel(pt_ref, q_ref, g3_ref, new_ref, win_ref, wnew_ref, a_ref, w_ref, ov_ref, *rest,
                  pg, n_pages, n_cmp, n_blk, n_top, past):
    pages = rest[:pg]
    o_ref, ss_sc, bias_sc, vs_sc, lok_sc, hik_sc, lov_sc, hiv_sc = rest[pg:]
    g = pl.program_id(1)
    q = q_ref[...].astype(BF16)
    sub = PAGE // C_CMP_STRIDE

    for i in range(pg):
        c = g * pg + i
        page = pages[i]
        r0 = pl.multiple_of(c * sub, sub)
        lo, hi = _summaries(page[:, 0:128], a_ref[0], None, n_cmp)
        lok_sc[pl.ds(r0, sub), :] = lo
        hik_sc[pl.ds(r0, sub), :] = hi
        lo, hi = _summaries(page[:, 128:256], a_ref[1], None, n_cmp)
        lov_sc[pl.ds(r0, sub), :] = lo
        hiv_sc[pl.ds(r0, sub), :] = hi
        ss_sc[c] = _dot_nt(q, page[:, 256:384].astype(BF16))
        vs_sc[pl.ds(pl.multiple_of(c * LANES, LANES), LANES), :] = page[:, 384:512].astype(BF16)

    @pl.when(g == pl.num_programs(1) - 1)
    def _():
        qf = q.astype(F32)
        new = new_ref[...]
        lane = lax.broadcasted_iota(I32, (SUBLANES, LANES), 1)
        qpos = jnp.full((SUBLANES, 1), past, I32)
        kc = _finish_summaries(lok_sc[...], hik_sc[...], w_ref[0], n_cmp)
        vc = _finish_summaries(lov_sc[...], hiv_sc[...], w_ref[1], n_cmp)
        ncp = kc.shape[0]
        n_io = lax.broadcasted_iota(I32, (SUBLANES, ncp), 1)
        cvalid = (n_io < n_cmp) & (n_io * C_CMP_STRIDE + (C_CMP_LEN - 1) <= qpos)
        p = _cmp_softmax(q, kc, cvalid)
        o_c = _dot(p.astype(BF16), vc)
        head = lax.broadcasted_iota(I32, p.shape, 0) < C_HEADS
        psum = jnp.broadcast_to(jnp.sum(jnp.where(head, p, 0.0), axis=0, keepdims=True), p.shape)
        imp = _dot_split3(psum, ov_ref[...])
        sel = _select_rank(_block_scores(imp, qpos), n_blk, n_top)
        for c in range(n_pages):
            pick = jnp.where(lane < C_SLC_BLOCK, sel[:, 2 * c:2 * c + 1], sel[:, 2 * c + 1:2 * c + 2])
            bias_sc[c] = jnp.where(pick > 0.5, 0.0, NEG)
        k_new = new[0:1, 256:384].astype(BF16).astype(F32)
        s_new = jnp.sum(qf * k_new, axis=-1, keepdims=True)
        bias_new = jnp.where(sel[:, 2 * n_pages:2 * n_pages + 1] > 0.5, 0.0, NEG)
        o_s = _tile_attention(ss_sc, bias_sc, vs_sc, n_pages, s_new, bias_new, new[0:1, 384:512], False)
        wb = win_ref.shape[0]
        kw = win_ref[:, 0:128].astype(BF16)
        vw = win_ref[:, 128:256].astype(BF16)
        wnew = wnew_ref[...]
        s_w = _dot_nt(q, kw) * ATT_SCALE
        dist = wb - lax.broadcasted_iota(I32, (SUBLANES, wb), 1)
        wvalid = (dist <= C_WINDOW) & (past - dist >= 0)
        s_w = jnp.where(wvalid, s_w, NEG)
        s_n = jnp.sum(qf * wnew[0:1, 0:128].astype(BF16).astype(F32), axis=-1, keepdims=True) * ATT_SCALE
        m = jnp.maximum(jnp.max(s_w, axis=-1, keepdims=True), s_n)
        e_w = jnp.where(wvalid, jnp.exp(s_w - m), 0.0)
        e_n = jnp.exp(s_n - m)
        den = jnp.sum(e_w, axis=-1, keepdims=True) + e_n
        v_n = wnew[0:1, 128:256].astype(BF16).astype(F32)
        o_w = (_dot(e_w.astype(BF16), vw) + e_n.astype(BF16).astype(F32) * v_n) / den
        o_ref[...] = g3_ref[0] * o_c + g3_ref[1] * o_s + g3_ref[2] * o_w


def _nsa_sample(page_table, cache, layer, q8, g3, new8, win, wnew8, cmp_a, cmp_w, ov):
    bsz, n_pages = page_table.shape
    pg = PAGES_PER_STEP
    past = n_pages * PAGE
    n_cmp = (past + 1 - C_CMP_LEN) // C_CMP_STRIDE + 1
    n_blk = -(-(past + 1) // C_SLC_BLOCK)
    ns = past // C_CMP_STRIDE
    wb = win.shape[2]
    cst = lambda a: pl.BlockSpec(a.shape, lambda b, g, pt: (0,) * a.ndim)
    per_b = lambda r, w: pl.BlockSpec((None, r, w), lambda b, g, pt: (b, 0, 0))
    page_spec = lambda i: pl.BlockSpec((None, None, PAGE, C_CACHE_DIM),
                                       lambda b, g, pt: (layer, pt[b, g * pg + i], 0, 0))
    grid_spec = pltpu.PrefetchScalarGridSpec(
        num_scalar_prefetch=1,
        grid=(bsz, n_pages // pg),
        in_specs=[per_b(SUBLANES, LANES),
                  pl.BlockSpec((None, 3, SUBLANES, LANES), lambda b, g, pt: (b, 0, 0, 0)),
                  per_b(SUBLANES, C_CACHE_DIM),
                  pl.BlockSpec((None, None, wb, C_WIN_DIM), lambda b, g, pt: (layer, b, 0, 0)),
                  per_b(SUBLANES, C_WIN_DIM), cst(cmp_a), cst(cmp_w), cst(ov)]
        + [page_spec(i) for i in range(pg)],
        out_specs=per_b(SUBLANES, HEAD_DIM),
        scratch_shapes=[pltpu.VMEM((n_pages, SUBLANES, LANES), F32), pltpu.VMEM((n_pages, SUBLANES, LANES), F32),
                        pltpu.VMEM((n_pages * LANES, HEAD_DIM), BF16)]
        + [pltpu.VMEM((ns, HEAD_DIM), F32)] * 4,
    )
    return pl.pallas_call(
        functools.partial(_nsa_s_kernel, pg=pg, n_pages=n_pages, n_cmp=n_cmp, n_blk=n_blk, n_top=min(C_TOPN, n_blk),
                          past=past),
        grid_spec=grid_spec,
        out_shape=jax.ShapeDtypeStruct((bsz, SUBLANES, HEAD_DIM), F32),
        compiler_params=_cparams(2),
        name="nsa_sample",
    )(page_table, q8, g3, new8, win, wnew8, cmp_a, cmp_w, ov, *([cache] * pg))


def _step_kernel(u_ref, h0r_ref, h0i_ref, bdr_h_ref, bdr_l_ref, bdi_h_ref, bdi_l_ref, ar_ref, ai_ref,
                 cdr_ref, cdi_ref, d_ref, gw_ref, gb_ref, da_ref, cst_ref, cw_ref, cb_ref, lg_ref, lb_ref,
                 ob_ref, hr_ref, hi_ref, od_ref):
    u = u_ref[...]
    uh = u.astype(BF16)
    ul = (u - uh.astype(F32)).astype(BF16)

    def bmat(h_ref, l_ref):
        return _dot(uh, h_ref[...]) + (_dot(uh, l_ref[...]) + _dot(ul, h_ref[...]))

    ar, ai = ar_ref[...], ai_ref[...]
    h0r, h0i = h0r_ref[...], h0i_ref[...]
    hr = bmat(bdr_h_ref, bdr_l_ref) + (ar * h0r - ai * h0i)
    hi = bmat(bdi_h_ref, bdi_l_ref) + (ar * h0i + ai * h0r)
    hr_ref[...] = hr
    hi_ref[...] = hi
    y = _dot(hr.astype(BF16), cdr_ref[...]) - _dot(hi.astype(BF16), cdi_ref[...]) + d_ref[...] * u
    gl = _gelu(y)
    ob_ref[...] = (gl * _sigmoid(_dot(gl.astype(BF16), gw_ref[...]) + gb_ref[...])).astype(BF16)
    cw = cw_ref[...]
    y = jnp.sum(cst_ref[...] * cw[0:D_CONV - 1][None], axis=1) + cw[D_CONV - 1:D_CONV] * da_ref[...] + cb_ref[...]
    od_ref[...] = _ln_swish(y, lg_ref[...], lb_ref[...]).astype(BF16)


def _sample_step(u, h0r, h0i, sp, da, conv_state, cw, cb, lg, lb):
    bsz = u.shape[0]
    ins = [u, h0r, h0i, sp["bdr"], sp["bdr_lo"], sp["bdi"], sp["bdi_lo"], sp["ar"], sp["ai"], sp["cdr"], sp["cdi"],
           sp["d"], sp["glu_w"], sp["glu_b"], da, conv_state, cw, cb, lg, lb]
    outs = [((bsz, BR_WIDTH), BF16), ((bsz, B_LANES), F32), ((bsz, B_LANES), F32), ((bsz, BR_WIDTH), BF16)]
    return pl.pallas_call(
        _step_kernel,
        grid=(1,),
        in_specs=[_full(a, 1) for a in ins],
        out_specs=[pl.BlockSpec(o[0], lambda i: (0, 0)) for o in outs],
        out_shape=[jax.ShapeDtypeStruct(o[0], o[1]) for o in outs],
        compiler_params=_cparams(1),
        name="sample_step",
    )(*ins)


def _rope_tables(pos):
    pos = pos.astype(F32)[:, None]

    def tab(dim):
        half = dim // 2
        inv = ROPE_THETA ** (-jnp.arange(half, dtype=F32) / half)
        ang = pos * inv
        cos, sin = jnp.cos(ang), jnp.sin(ang)
        reps = LANES // dim
        return jnp.tile(jnp.concatenate([cos, cos], axis=1), (1, reps)), jnp.tile(jnp.concatenate([-sin, sin], axis=1), (1, reps))

    c128, s128 = tab(HEAD_DIM)
    c64, s64 = tab(A_IDX_DIM)
    return c128, s128, c64, s64


def _pack_w1(wt):
    def padded(a, b, rows):
        return jnp.pad(wt[a:b], ((0, rows - (b - a)), (0, 0)))
    parts = [wt[_O[0]:_O[4]], padded(_O[4], _O[5], 128), padded(_O[5], _O[6], 128), wt[_O[6]:_O[9]],
             padded(_O[9], _O[10], 128), wt[_O[10]:_O[11]]]
    return jnp.concatenate(parts, axis=0).astype(BF16)


def _s5_params(lam_re, lam_im, log_dt, b_re, b_im, c_re, c_im, d, glu_w, glu_b):
    lr, li = lam_re.astype(F32), lam_im.astype(F32)
    dt = jnp.exp(log_dt.astype(F32))[:, None]
    mag = jnp.exp(lr * dt)
    ar, ai = mag * jnp.cos(li * dt), mag * jnp.sin(li * dt)
    den = lr * lr + li * li
    fr = ((ar - 1.0) * lr + ai * li) / den
    fi = (ai * lr - (ar - 1.0) * li) / den
    br, bi = b_re.astype(F32), b_im.astype(F32)
    bbr = fr[..., None] * br - fi[..., None] * bi
    bbi = fr[..., None] * bi + fi[..., None] * br
    eye = jnp.eye(B_GROUPS, dtype=F32)
    bd = lambda m: jnp.einsum("gpc,gh->gchp", m, eye).reshape(BR_WIDTH, B_LANES)
    cd = lambda m: jnp.einsum("gcp,gh->gphc", m.astype(F32), eye).reshape(B_LANES, BR_WIDTH)
    bdr, bdi = bd(bbr), bd(bbi)
    hi_lo = lambda m: (m.astype(BF16), (m - m.astype(BF16).astype(F32)).astype(BF16))
    bdr_h, bdr_l = hi_lo(bdr)
    bdi_h, bdi_l = hi_lo(bdi)
    nsb = S5_BLOCKS
    diag_b = lambda m: jnp.stack([m[s * (BR_WIDTH // nsb):(s + 1) * (BR_WIDTH // nsb),
                                    s * (B_LANES // nsb):(s + 1) * (B_LANES // nsb)] for s in range(nsb)])
    diag_c = lambda m: jnp.stack([m[s * (B_LANES // nsb):(s + 1) * (B_LANES // nsb),
                                    s * (BR_WIDTH // nsb):(s + 1) * (BR_WIDTH // nsb)] for s in range(nsb)])
    return dict(bdr=bdr_h, bdr_lo=bdr_l, bdi=bdi_h, bdi_lo=bdi_l,
                bdr4=diag_b(bdr_h), bdi4=diag_b(bdi_h), cdr4=diag_c(cd(c_re).astype(BF16)), cdi4=diag_c(cd(c_im).astype(BF16)),
                ar=ar.reshape(1, B_LANES), ai=ai.reshape(1, B_LANES),
                cdr=cd(c_re).astype(BF16), cdi=cd(c_im).astype(BF16), d=d.astype(F32).reshape(1, BR_WIDTH),
                glu_w=glu_w.astype(BF16), glu_b=glu_b.astype(F32).reshape(1, BR_WIDTH))


def _overlap(n_cmp, n_blk, rows, cols):
    start = np.arange(n_cmp)[:, None] * C_CMP_STRIDE
    blk = np.arange(n_blk)[None, :]
    m = (start <= (blk + 1) * C_SLC_BLOCK - 1) & (start + C_CMP_LEN - 1 >= blk * C_SLC_BLOCK)
    out = np.zeros((rows, cols), np.float32)
    out[:n_cmp, :n_blk] = m
    return jnp.asarray(out, BF16)


def _expand(n_keys):
    e = (np.arange(LANES)[:, None] == (np.arange(n_keys)[None, :] // C_SLC_BLOCK)).astype(np.float32)
    return jnp.asarray(e, BF16)


def _tri(lower=False):
    i = np.arange(LANES)
    m = (i[:, None] > i[None, :]) if lower else (i[:, None] < i[None, :])
    return jnp.asarray(m.astype(np.float32), BF16)


def _make_consts(T, sb, past):
    n_cmp_p = (T - C_CMP_LEN) // C_CMP_STRIDE + 1
    n_blk_p = -(-T // C_SLC_BLOCK)
    n_cmp_s = (past + 1 - C_CMP_LEN) // C_CMP_STRIDE + 1
    n_blk_s = -(-(past + 1) // C_SLC_BLOCK)
    return dict(
        tabs_p=_rope_tables(jnp.arange(T)),
        tabs_s=_rope_tables(jnp.full((sb,), past)),
        tri=_tri(),
        tril=_tri(lower=True),
        ov_p=_overlap(n_cmp_p, n_blk_p, T // C_CMP_STRIDE, LANES),
        ex_p=_expand(T),
        ov_s=_overlap(n_cmp_s, n_blk_s, past // C_CMP_STRIDE, -(-n_blk_s // LANES) * LANES),
    )


def _pick_tile(n, cands):
    for c in cands:
        if n % c == 0:
            return c
    return n


def _pad_rows(a, rows):
    return jnp.pad(a[:, None, :], ((0, 0), (0, rows - 1), (0, 0)))


def _layer_weights(l, norm_mix, w_in, a_gq, a_gk, c_gq, c_gk, c_cmp_a, c_cmp_w, d_conv_w, d_conv_b, d_ln_g, d_ln_b,
                   w_br, w_o, norm_mlp, w_up, w_down):
    row = lambda v: v.astype(F32).reshape(1, -1)
    return dict(
        g_mix=row(norm_mix[l]), w1=_pack_w1(jnp.transpose(w_in[l])),
        wg=jnp.transpose(w_in[l])[_O[11]:_O[12]].astype(BF16),
        a_gq=row(a_gq[l]), a_gk=row(a_gk[l]), c_gq=row(c_gq[l]), c_gk=c_gk[l].astype(F32),
        cmp_a=c_cmp_a[l].astype(F32), cmp_w=c_cmp_w[l].astype(BF16),
        conv_w=jnp.pad(d_conv_w[l].astype(F32), ((0, 1), (0, 0))), conv_b=row(d_conv_b[l]),
        ln_g=row(d_ln_g[l]), ln_b=row(d_ln_b[l]),
        w_br=w_br[l].astype(BF16), w_o=w_o[l].astype(BF16), g_mlp=row(norm_mlp[l]),
        w_up=w_up[l].astype(BF16), w_down=w_down[l].astype(BF16))


def _prompt_layer(x, lw, sp, consts):
    bsz, T, _ = x.shape
    n = bsz * T
    x2d = x.reshape(n, D_MODEL)
    tm = _pick_tile(T, (256, 128))
    qb = _pick_tile(T, (256, 128))
    tp = _pick_tile(T, (512, 256, 128))
    nt = T // tp
    bu_spec = pl.BlockSpec((tp, BR_WIDTH), lambda i: (i % nt, i // nt))
    a_t = ((bsz, A_CACHE_DIM, T), pl.BlockSpec((None, A_CACHE_DIM, tp), lambda i: (i // nt, 0, i % nt)))
    (xn, aq, arow, aiq, aiw, bu, cq, crow, wrow, cg, da, arow_t) = _project(
        x2d, consts["tabs_p"], lw, tm=tp, n_pos_blocks=nt, bu_shape=(T, bsz * BR_WIDTH), bu_spec=bu_spec, a_rows_t=a_t)
    arow3 = arow.reshape(bsz, T, A_CACHE_DIM)
    crow3 = crow.reshape(bsz, T, C_CACHE_DIM)
    wrow3 = wrow.reshape(bsz, T, C_WIN_DIM)
    da3 = da.reshape(bsz, T, BR_WIDTH)
    o_a = _dsa(aq, aiq, aiw, arow3, consts["tri"], qb=qb)
    kcmp, vcmp = _compress(crow3, lw["cmp_a"], lw["cmp_w"])
    o_c = _nsa(cq, cg, crow3, wrow3, kcmp, vcmp, consts["ov_p"], consts["ex_p"], qb=qb)
    tc = _pick_tile(T, (128, 64))
    o_b, hr, hi = _s5(bu.reshape(T * bsz, BR_WIDTH), sp, nb=bsz, tc=tc)
    o_d = _conv(da3, lw["conv_w"], lw["conv_b"], lw["ln_g"], lw["ln_b"], tm=tm)
    tmx = _pick_tile(T, (512, 256, 128))
    ntx = T // tmx
    ob_spec = pl.BlockSpec((tmx, BR_WIDTH), lambda r, i: (r % ntx, r // ntx))
    hm = _mix(xn, o_a, o_b.reshape(T, bsz * BR_WIDTH), ob_spec, o_c, o_d, lw["wg"], lw["w_br"], lw["w_o"], tm=tmx)
    y = _mlp(x2d, hm, lw["g_mlp"], lw["w_up"], lw["w_down"], tm=tmx)
    wk = min(C_WINDOW, T)
    return (y.reshape(bsz, T, D_MODEL), jnp.swapaxes(arow_t, 1, 2), crow3, wrow3[:, T - wk:],
            hr.reshape(bsz, B_GROUPS, B_STATE), hi.reshape(bsz, B_GROUPS, B_STATE), da3[:, T - (D_CONV - 1):])


def _sample_layer(x, l, cache_a, cache_c, cache_c_win, h_re, h_im, conv_l, page_table, lw, sp, consts):
    bsz = x.shape[0]
    cache_a_t = jnp.swapaxes(cache_a, 2, 3)
    x2d = x.reshape(bsz, D_MODEL)
    row = lambda w: pl.BlockSpec((bsz, w), lambda i: (0, 0))
    (xn, aq, arow, aiq, aiw, bu, cq, crow, wrow, cg, da) = _project(
        x2d, consts["tabs_s"], lw, tm=bsz, n_pos_blocks=1, bu_shape=(bsz, BR_WIDTH), bu_spec=row(BR_WIDTH))
    q8 = jnp.pad(aq.astype(F32).reshape(bsz, A_HEADS, HEAD_DIM), ((0, 0), (0, SUBLANES - A_HEADS), (0, 0)))
    iq8 = jnp.sum(aiq.astype(F32).reshape(bsz, A_IDX_HEADS, 2, A_IDX_DIM), axis=2)
    iq8 = jnp.pad(iq8, ((0, 0), (0, 0), (LANES - A_IDX_DIM, 0)))
    w8 = jnp.broadcast_to(aiw[:, :A_IDX_HEADS, None], (bsz, A_IDX_HEADS, LANES))
    o_a = _dsa_sample(page_table, cache_a_t, l, q8, iq8, w8, _pad_rows(arow, SUBLANES), consts["tri"], consts["tril"])
    o_a = o_a[:, :A_HEADS].reshape(bsz, BR_WIDTH).astype(BF16)
    cq8 = jnp.pad(cq.astype(F32).reshape(bsz, C_HEADS, HEAD_DIM), ((0, 0), (0, SUBLANES - C_HEADS), (0, 0)))
    g3 = jnp.transpose(cg[:, :3 * C_HEADS].reshape(bsz, C_HEADS, 3), (0, 2, 1))
    g3 = jnp.broadcast_to(jnp.pad(g3, ((0, 0), (0, 0), (0, SUBLANES - C_HEADS)))[..., None], (bsz, 3, SUBLANES, LANES))
    o_c = _nsa_sample(page_table, cache_c, l, cq8, g3, _pad_rows(crow, SUBLANES), cache_c_win,
                      _pad_rows(wrow, SUBLANES), lw["cmp_a"], lw["cmp_w"], consts["ov_s"])
    o_c = o_c[:, :C_HEADS].reshape(bsz, BR_WIDTH).astype(BF16)
    o_b, hr, hi, o_d = _sample_step(bu, h_re.reshape(bsz, B_LANES), h_im.reshape(bsz, B_LANES), sp, da, conv_l,
                                    lw["conv_w"], lw["conv_b"], lw["ln_g"], lw["ln_b"])
    hm = _mix(xn, o_a, o_b, pl.BlockSpec((bsz, BR_WIDTH), lambda r, i: (r, 0)), o_c, o_d,
              lw["wg"], lw["w_br"], lw["w_o"], tm=bsz)
    y = _mlp(x2d, hm, lw["g_mlp"], lw["w_up"], lw["w_down"], tm=bsz)
    new_win = jnp.concatenate([cache_c_win[l][:, 1:], wrow[:, None, :]], axis=1)
    new_conv = jnp.concatenate([conv_l[:, 1:], da[:, None, :]], axis=1)
    return (y.reshape(bsz, 1, D_MODEL), arow[:, None, :], crow[:, None, :], new_win,
            hr.reshape(bsz, B_GROUPS, B_STATE), hi.reshape(bsz, B_GROUPS, B_STATE), new_conv)


def kernel(x_prompt, x_sample, cache_a, cache_c, cache_c_win, state_b_re, state_b_im, state_d_conv, page_table, norm_mix, w_in, a_gq, a_gk, b_lam_re, b_lam_im, b_log_dt, b_b_re, b_b_im, b_c_re, b_c_im, b_d, b_glu_w, b_glu_b, c_gq, c_gk, c_cmp_a, c_cmp_w, d_conv_w, d_conv_b, d_ln_g, d_ln_b, w_br, w_o, norm_mlp, w_up, w_down):
    depth = w_in.shape[0]
    bsz, T, _ = x_prompt.shape
    sb, st, _ = x_sample.shape
    assert st == 1 and bsz == SUBLANES
    assert cache_a.shape[2] == PAGE and cache_c.shape[2] == PAGE
    consts = _make_consts(T, sb, page_table.shape[1] * PAGE)
    xp, xs = x_prompt, x_sample
    order_p = (0, 2, 4, 6, 7, 10)
    order_s = (1, 3, 5, 8, 9, 11)
    outs = [[] for _ in range(12)]
    for l in range(depth):
        lw = _layer_weights(l, norm_mix, w_in, a_gq, a_gk, c_gq, c_gk, c_cmp_a, c_cmp_w, d_conv_w, d_conv_b,
                            d_ln_g, d_ln_b, w_br, w_o, norm_mlp, w_up, w_down)
        sp = _s5_params(b_lam_re[l], b_lam_im[l], b_log_dt[l], b_b_re[l], b_b_im[l], b_c_re[l], b_c_im[l],
                        b_d[l], b_glu_w[l], b_glu_b[l])
        xp, *rp = _prompt_layer(xp, lw, sp, consts)
        xs, *rs = _sample_layer(xs, l, cache_a, cache_c, cache_c_win, state_b_re[l], state_b_im[l], state_d_conv[l],
                                page_table, lw, sp, consts)
        for k in range(6):
            outs[order_p[k]].append(rp[k])
            outs[order_s[k]].append(rs[k])
    return (xp, xs) + tuple(jnp.stack(o) for o in outs)
```
